```python
import math
import jax, jax.numpy as jnp
from jax import lax
import numpy as np

D_MODEL = 1024
BATCH = 8
SEQ = 4096
DEPTH = 1

HEAD_DIM = 64
N_SB_HEADS = 8
N_FOX_HEADS = 8
SB_WIDTH = N_SB_HEADS * HEAD_DIM
FOX_WIDTH = N_FOX_HEADS * HEAD_DIM
Q_BLOCK = 128
N_GROUPS = 4
EXPERTS_PER_GROUP = 8
N_EXPERTS = N_GROUPS * EXPERTS_PER_GROUP
TOP_K_IN_GROUP = 2
D_EXPERT = D_MODEL // 4
RMS_EPS = 1e-6

_COLS = [SB_WIDTH, SB_WIDTH, SB_WIDTH,
         FOX_WIDTH, FOX_WIDTH, FOX_WIDTH,
         N_FOX_HEADS,
         D_MODEL, D_MODEL]
IN_COLS = sum(_COLS)
SPLIT_POINTS = [int(v) for v in np.cumsum(_COLS)[:-1]]

kernel_name = "hybrid_stickbreak_fox_hiermoe"


def rmsnorm(x, g):
    xf = x.astype(jnp.float32)
    y = xf * lax.rsqrt(jnp.mean(xf * xf, axis=-1, keepdims=True) + RMS_EPS)
    return (y * g.astype(jnp.float32)).astype(x.dtype)


def split_heads(t, n_heads):
    b, s, _ = t.shape
    return t.reshape(b, s, n_heads, HEAD_DIM).transpose(0, 2, 1, 3)


def merge_heads(t):
    b, h, s, d = t.shape
    return t.transpose(0, 2, 1, 3).reshape(b, s, h * d)


def to_blocks(t):
    b, h, s = t.shape[:3]
    rest = t.shape[3:]
    nb = s // Q_BLOCK
    t = t.reshape((b, h, nb, Q_BLOCK) + rest)
    return jnp.moveaxis(t, 2, 0)


def from_blocks(t):
    nb, b, h, qb, d = t.shape
    return jnp.moveaxis(t, 0, 2).reshape(b, h, nb * qb, d)


def stick_breaking_attention(q, k, v):
    s_len = q.shape[2]
    nb = s_len // Q_BLOCK
    scale = 1.0 / math.sqrt(HEAD_DIM)
    kf = k.astype(jnp.float32)
    vf = v.astype(jnp.float32)
    kpos = jnp.arange(s_len)

    def one_block(args):
        qb, bi = args
        qpos = bi * Q_BLOCK + jnp.arange(Q_BLOCK)
        z = jnp.einsum('bhqd,bhkd->bhqk', qb.astype(jnp.float32), kf) * scale
        mask = kpos[None, :] < qpos[:, None]
        log_keep = jnp.where(mask, jax.nn.log_sigmoid(-z), 0.0)
        rest = lax.cumsum(log_keep, axis=3, reverse=True) - log_keep
        a = jnp.where(mask, jnp.exp(jax.nn.log_sigmoid(z) + rest), 0.0)
        return jnp.einsum('bhqk,bhkd->bhqd', a, vf)

    out = lax.map(one_block, (to_blocks(q), jnp.arange(nb)))
    return from_blocks(out).astype(q.dtype)


def forgetting_attention(q, k, v, log_f):
    s_len = q.shape[2]
    nb = s_len // Q_BLOCK
    scale = 1.0 / math.sqrt(HEAD_DIM)
    kf = k.astype(jnp.float32)
    vf = v.astype(jnp.float32)
    c = jnp.cumsum(log_f, axis=-1)
    kpos = jnp.arange(s_len)

    def one_block(args):
        qb, cq, bi = args
        qpos = bi * Q_BLOCK + jnp.arange(Q_BLOCK)
        z = jnp.einsum('bhqd,bhkd->bhqk', qb.astype(jnp.float32), kf) * scale
        z = z + cq[..., :, None] - c[..., None, :]
        mask = kpos[None, :] <= qpos[:, None]
        z = jnp.where(mask, z, -jnp.inf)
        p = jax.nn.softmax(z, axis=-1)
        return jnp.einsum('bhqk,bhkd->bhqd', p, vf)

    out = lax.map(one_block, (to_blocks(q), to_blocks(c), jnp.arange(nb)))
    return from_blocks(out).astype(q.dtype)


def hierarchical_moe(h, w_rg, b_rg, w_re, b_re, w1, w3, w2):
    b, s, d = h.shape
    hf = h.reshape(b * s, d)
    g_logits = (hf @ w_rg).astype(jnp.float32) + b_rg.astype(jnp.float32)
    g_prob = jax.nn.softmax(g_logits, axis=-1)
    g_idx = jnp.argmax(g_prob, axis=-1)
    g_w = jnp.take_along_axis(g_prob, g_idx[:, None], axis=1)

    e_logits = (hf @ w_re).astype(jnp.float32) + b_re.astype(jnp.float32)
    e_logits = e_logits.reshape(-1, N_GROUPS, EXPERTS_PER_GROUP)
    e_sel = jnp.take_along_axis(e_logits, g_idx[:, None, None], axis=1)[:, 0]
    e_prob = jax.nn.softmax(e_sel, axis=-1)
    top_v, top_i = lax.top_k(e_prob, TOP_K_IN_GROUP)
    top_v = top_v / jnp.sum(top_v, axis=-1, keepdims=True)
    weights = g_w * top_v
    expert_id = g_idx[:, None] * EXPERTS_PER_GROUP + top_i
    combine = jnp.sum(jax.nn.one_hot(expert_id, N_EXPERTS, dtype=jnp.float32)
                      * weights[..., None], axis=1)

    y = jnp.zeros(hf.shape, jnp.float32)
    for e in range(N_EXPERTS):
        hidden = jax.nn.silu(hf @ w1[e]) * (hf @ w3[e])
        y = y + combine[:, e:e + 1] * (hidden @ w2[e]).astype(jnp.float32)
    return y.reshape(b, s, d).astype(h.dtype)


def setup_inputs(seed: int = 0) -> dict:
    key = jax.random.key(seed)
    ks = jax.random.split(key, 20)
    nrm = jax.random.normal
    L = DEPTH
    x = nrm(ks[0], (BATCH, SEQ, D_MODEL), jnp.float32)
    norm_attn = 1.0 + 0.02 * nrm(ks[1], (L, D_MODEL), jnp.float32)
    w_in = nrm(ks[2], (L, D_MODEL, IN_COLS), jnp.float32) * D_MODEL ** -0.5
    b_forget = 2.0 + 0.1 * nrm(ks[3], (L, N_FOX_HEADS), jnp.float32)
    w_o_sb = nrm(ks[4], (L, SB_WIDTH, D_MODEL), jnp.float32) * SB_WIDTH ** -0.5
    w_o_fox = nrm(ks[5], (L, FOX_WIDTH, D_MODEL), jnp.float32) * FOX_WIDTH ** -0.5
    w_out = nrm(ks[6], (L, D_MODEL, D_MODEL), jnp.float32) * D_MODEL ** -0.5
    norm_ffn = 1.0 + 0.02 * nrm(ks[7], (L, D_MODEL), jnp.float32)
    w_router_group = nrm(ks[8], (L, D_MODEL, N_GROUPS), jnp.float32) * D_MODEL ** -0.5
    b_router_group = 0.01 * nrm(ks[9], (L, N_GROUPS), jnp.float32)
    w_router_expert = nrm(ks[10], (L, D_MODEL, N_EXPERTS), jnp.float32) * D_MODEL ** -0.5
    b_router_expert = 0.01 * nrm(ks[11], (L, N_EXPERTS), jnp.float32)
    w1 = nrm(ks[12], (L, N_EXPERTS, D_MODEL, D_EXPERT), jnp.float32) * D_MODEL ** -0.5
    w3 = nrm(ks[13], (L, N_EXPERTS, D_MODEL, D_EXPERT), jnp.float32) * D_MODEL ** -0.5
    w2 = nrm(ks[14], (L, N_EXPERTS, D_EXPERT, D_MODEL), jnp.float32) * D_EXPERT ** -0.5
    norm_final = 1.0 + 0.02 * nrm(ks[15], (D_MODEL,), jnp.float32)
    return {"x": x, "norm_attn": norm_attn, "w_in": w_in, "b_forget": b_forget,
            "w_o_sb": w_o_sb, "w_o_fox": w_o_fox, "w_out": w_out, "norm_ffn": norm_ffn,
            "w_router_group": w_router_group, "b_router_group": b_router_group,
            "w_router_expert": w_router_expert, "b_router_expert": b_router_expert,
            "w1": w1, "w3": w3, "w2": w2, "norm_final": norm_final}


def reference(x, norm_attn, w_in, b_forget, w_o_sb, w_o_fox, w_out, norm_ffn,
              w_router_group, b_router_group, w_router_expert, b_router_expert,
              w1, w3, w2, norm_final):
    for l in range(DEPTH):
        h = rmsnorm(x, norm_attn[l])
        proj = h @ w_in[l]
        q_sb, k_sb, v_sb, q_fx, k_fx, v_fx, f_logit, g_sb, g_fx = jnp.split(
            proj, SPLIT_POINTS, axis=-1)

        o_sb = stick_breaking_attention(split_heads(q_sb, N_SB_HEADS),
                                        split_heads(k_sb, N_SB_HEADS),
                                        split_heads(v_sb, N_SB_HEADS))
        log_f = jax.nn.log_sigmoid((f_logit + b_forget[l]).astype(jnp.float32))
        log_f = log_f.transpose(0, 2, 1)
        o_fx = forgetting_attention(split_heads(q_fx, N_FOX_HEADS),
                                    split_heads(k_fx, N_FOX_HEADS),
                                    split_heads(v_fx, N_FOX_HEADS), log_f)

        y_sb = merge_heads(o_sb) @ w_o_sb[l]
        y_fx = merge_heads(o_fx) @ w_o_fox[l]
        mixed = jax.nn.sigmoid(g_sb) * y_sb + jax.nn.sigmoid(g_fx) * y_fx
        x = x + mixed @ w_out[l]

        h = rmsnorm(x, norm_ffn[l])
        x = x + hierarchical_moe(h, w_router_group[l], b_router_group[l],
                                 w_router_expert[l], b_router_expert[l],
                                 w1[l], w3[l], w2[l])
    return rmsnorm(x, norm_final)
```

```python
import functools
import math

import jax
import jax.numpy as jnp
from jax import lax
from jax.experimental import pallas as pl
from jax.experimental.pallas import tpu as pltpu

HEAD_DIM = 64
N_HEADS = 8
ATTN_WIDTH = N_HEADS * HEAD_DIM
N_GROUPS = 4
EXPERTS_PER_GROUP = 8
N_EXPERTS = N_GROUPS * EXPERTS_PER_GROUP
RMS_EPS = 1e-6
LANES = 128
HEADS_PER_BLOCK = LANES // HEAD_DIM
N_HEAD_BLOCKS = N_HEADS // HEADS_PER_BLOCK
ROUTER_ROWS = 8 + N_EXPERTS
VMEM_LIMIT = 56 * 1024 * 1024

TM_PROJ = 512
TQ = 256
TK = 256
TM_POST = 256
TM_MOE = 512

_NT = (((1,), (1,)), ((), ()))


def _split_bf16(v, parts):
    out = []
    r = v
    for i in range(parts):
        p = r.astype(jnp.bfloat16)
        out.append(p)
        if i + 1 < parts:
            r = r - p.astype(jnp.float32)
    return out


def _rmsnorm_f32(x, g):
    ms = jnp.mean(x * x, axis=-1, keepdims=True)
    return x * lax.rsqrt(ms + RMS_EPS) * g


def _inproj_kernel(x_ref, g_ref, w_ref, wfh_ref, wfl_ref, bf_ref,
                   proj_ref, c_ref, ct_ref, carry_ref, *, tiles_per_seq, n_chunks, chunk):
    i = pl.program_id(0)

    @pl.when(i % tiles_per_seq == 0)
    def _():
        carry_ref[...] = jnp.zeros_like(carry_ref)

    y = _rmsnorm_f32(x_ref[...], g_ref[...])
    h_hi, h_lo = _split_bf16(y, 2)
    for c in range(n_chunks):
        sl = slice(c * chunk, (c + 1) * chunk)
        proj_ref[:, sl] = jnp.dot(h_hi, w_ref[:, sl],
                                  preferred_element_type=jnp.float32).astype(proj_ref.dtype)

    f = (jnp.dot(h_hi, wfh_ref[...], preferred_element_type=jnp.float32)
         + jnp.dot(h_lo, wfh_ref[...], preferred_element_type=jnp.float32)
         + jnp.dot(h_hi, wfl_ref[...], preferred_element_type=jnp.float32))
    f = f + bf_ref[...]
    logf = jnp.minimum(f, 0.0) - jnp.log1p(jnp.exp(-jnp.abs(f)))

    tm = logf.shape[0]
    row = lax.broadcasted_iota(jnp.int32, (tm, tm), 0)
    col = lax.broadcasted_iota(jnp.int32, (tm, tm), 1)
    tri = (col <= row).astype(jnp.bfloat16)
    cum = carry_ref[...]
    for p in _split_bf16(logf, 3):
        cum = cum + jnp.dot(tri, p, preferred_element_type=jnp.float32)
    c_ref[...] = cum
    ct_ref[...] = cum.T[:N_HEADS, :]
    carry_ref[...] = cum[tm - 1:tm, :]


def _inproj(x2, g, w_main, wf_hi, wf_lo, bf, seq):
    n, d = x2.shape
    cols = w_main.shape[1]
    tm = TM_PROJ
    chunk = 512
    kern = functools.partial(_inproj_kernel, tiles_per_seq=seq // tm,
                             n_chunks=cols // chunk, chunk=chunk)
    const = dict(pipeline_mode=pl.Buffered(1))
    return pl.pallas_call(
        kern,
        grid=(n // tm,),
        in_specs=[
            pl.BlockSpec((tm, d), lambda i: (i, 0)),
            pl.BlockSpec((1, d), lambda i: (0, 0), **const),
            pl.BlockSpec((d, cols), lambda i: (0, 0), **const),
            pl.BlockSpec((d, LANES), lambda i: (0, 0), **const),
            pl.BlockSpec((d, LANES), lambda i: (0, 0), **const),
            pl.BlockSpec((1, LANES), lambda i: (0, 0), **const),
        ],
        out_specs=[
            pl.BlockSpec((tm, cols), lambda i: (i, 0)),
            pl.BlockSpec((tm, LANES), lambda i: (i, 0)),
            pl.BlockSpec((N_HEADS, tm), lambda i: (0, i)),
        ],
        out_shape=[
            jax.ShapeDtypeStruct((n, cols), jnp.bfloat16),
            jax.ShapeDtypeStruct((n, LANES), jnp.float32),
            jax.ShapeDtypeStruct((N_HEADS, n), jnp.float32),
        ],
        scratch_shapes=[pltpu.VMEM((1, LANES), jnp.float32)],
        compiler_params=pltpu.CompilerParams(
            dimension_semantics=("arbitrary",), vmem_limit_bytes=VMEM_LIMIT),
        name="inproj",
    )(x2, g, w_main, wf_hi, wf_lo, bf)


def _sb_kernel(q_ref, k_ref, v_ref, o_ref, acc_ref, carry_ref):
    qi = pl.program_id(2)
    q = q_ref[...] * jnp.asarray(1.0 / math.sqrt(HEAD_DIM), q_ref.dtype)
    lane = lax.broadcasted_iota(jnp.int32, q.shape, 1)
    qs = [jnp.where((lane // HEAD_DIM) == h, q, jnp.zeros_like(q)) for h in range(HEADS_PER_BLOCK)]
    row = lax.broadcasted_iota(jnp.int32, (TQ, TK), 0)
    col = lax.broadcasted_iota(jnp.int32, (TQ, TK), 1)
    below = col < row
    upper = below.astype(jnp.bfloat16)

    acc_ref[...] = jnp.zeros_like(acc_ref)
    carry_ref[...] = jnp.zeros_like(carry_ref)

    def tile(kj, diag):
        k = k_ref[pl.ds(kj * TK, TK), :]
        v = v_ref[pl.ds(kj * TK, TK), :]
        for h in range(HEADS_PER_BLOCK):
            z = lax.dot_general(qs[h], k, _NT, preferred_element_type=jnp.float32)
            t = jnp.log1p(jnp.exp(-jnp.abs(z)))
            log_keep = -(t + jnp.maximum(z, 0.0))
            log_beta = jnp.minimum(z, 0.0) - t
            if diag:
                log_keep = jnp.where(below, log_keep, 0.0)
            hi, lo = _split_bf16(log_keep, 2)
            rest = (jnp.dot(hi, upper, preferred_element_type=jnp.float32)
                    + jnp.dot(lo, upper, preferred_element_type=jnp.float32))
            a = jnp.exp(log_beta + rest + carry_ref[h])
            if diag:
                a = jnp.where(below, a, 0.0)
            acc_ref[h] += jnp.dot(a.astype(v.dtype), v, preferred_element_type=jnp.float32)
            carry_ref[h] += jnp.sum(log_keep, axis=-1, keepdims=True)

    tile(qi, True)

    def body(step, _):
        tile(qi - 1 - step, False)
        return 0

    lax.fori_loop(0, qi, body, 0)

    out = acc_ref[0]
    for h in range(1, HEADS_PER_BLOCK):
        out = jnp.where((lane // HEAD_DIM) == h, acc_ref[h], out)
    o_ref[...] = out.astype(o_ref.dtype)


def _sb_attention(proj3, q_blk, k_blk, v_blk):
    b, s, _ = proj3.shape
    return pl.pallas_call(
        _sb_kernel,
        grid=(b, N_HEAD_BLOCKS, s // TQ),
        in_specs=[
            pl.BlockSpec((None, TQ, LANES), lambda bi, hb, qi: (bi, qi, q_blk + hb)),
            pl.BlockSpec((None, s, LANES), lambda bi, hb, qi: (bi, 0, k_blk + hb)),
            pl.BlockSpec((None, s, LANES), lambda bi, hb, qi: (bi, 0, v_blk + hb)),
        ],
        out_specs=pl.BlockSpec((None, TQ, LANES), lambda bi, hb, qi: (bi, qi, hb)),
        out_shape=jax.ShapeDtypeStruct((b, s, ATTN_WIDTH), jnp.bfloat16),
        scratch_shapes=[pltpu.VMEM((HEADS_PER_BLOCK, TQ, LANES), jnp.float32),
                        pltpu.VMEM((HEADS_PER_BLOCK, TQ, 1), jnp.float32)],
        compiler_params=pltpu.CompilerParams(
            dimension_semantics=("arbitrary", "arbitrary", "arbitrary"),
            vmem_limit_bytes=VMEM_LIMIT),
        name="sb_attn",
    )(proj3, proj3, proj3)


def _fox_kernel(q_ref, k_ref, v_ref, c_ref, ct_ref, o_ref, acc_ref, m_ref, l_ref):
    hb = pl.program_id(1)
    qi = pl.program_id(2)
    q = q_ref[...] * jnp.asarray(1.0 / math.sqrt(HEAD_DIM), q_ref.dtype)
    lane = lax.broadcasted_iota(jnp.int32, q.shape, 1)
    qs = [jnp.where((lane // HEAD_DIM) == h, q, jnp.zeros_like(q)) for h in range(HEADS_PER_BLOCK)]
    cblk = c_ref[...]
    cqs = [jnp.sum(jnp.where(lane == hb * HEADS_PER_BLOCK + h, cblk, 0.0), axis=-1, keepdims=True)
           for h in range(HEADS_PER_BLOCK)]
    row = lax.broadcasted_iota(jnp.int32, (TQ, TK), 0)
    col = lax.broadcasted_iota(jnp.int32, (TQ, TK), 1)
    causal = col <= row

    acc_ref[...] = jnp.zeros_like(acc_ref)
    l_ref[...] = jnp.zeros_like(l_ref)
    m_ref[...] = jnp.full_like(m_ref, -jnp.inf)

    def tile(kj, diag):
        k = k_ref[pl.ds(kj * TK, TK), :]
        v = v_ref[pl.ds(kj * TK, TK), :]
        for h in range(HEADS_PER_BLOCK):
            ck = ct_ref[pl.ds(hb * HEADS_PER_BLOCK + h, 1), pl.ds(kj * TK, TK)]
            z = lax.dot_general(qs[h], k, _NT, preferred_element_type=jnp.float32)
            z = z + cqs[h] - ck
            if diag:
                z = jnp.where(causal, z, -jnp.inf)
            m_old = m_ref[h]
            m_new = jnp.maximum(m_old, jnp.max(z, axis=-1, keepdims=True))
            alpha = jnp.exp(m_old - m_new)
            p = jnp.exp(z - m_new)
            l_ref[h] = alpha * l_ref[h] + jnp.sum(p, axis=-1, keepdims=True)
            acc_ref[h] = alpha * acc_ref[h] + jnp.dot(p.astype(v.dtype), v,
                                                      preferred_element_type=jnp.float32)
            m_ref[h] = m_new

    tile(qi, True)

    def body(step, _):
        tile(qi - 1 - step, False)
        return 0

    lax.fori_loop(0, qi, body, 0)

    out = acc_ref[0] / l_ref[0]
    for h in range(1, HEADS_PER_BLOCK):
        out = jnp.where((lane // HEAD_DIM) == h, acc_ref[h] / l_ref[h], out)
    o_ref[...] = out.astype(o_ref.dtype)


def _fox_attention(proj3, c3, ct, q_blk, k_blk, v_blk):
    b, s, _ = proj3.shape
    return pl.pallas_call(
        _fox_kernel,
        grid=(b, N_HEAD_BLOCKS, s // TQ),
        in_specs=[
            pl.BlockSpec((None, TQ, LANES), lambda bi, hb, qi: (bi, qi, q_blk + hb)),
            pl.BlockSpec((None, s, LANES), lambda bi, hb, qi: (bi, 0, k_blk + hb)),
            pl.BlockSpec((None, s, LANES), lambda bi, hb, qi: (bi, 0, v_blk + hb)),
            pl.BlockSpec((None, TQ, LANES), lambda bi, hb, qi: (bi, qi, 0)),
            pl.BlockSpec((N_HEADS, s), lambda bi, hb, qi: (0, bi)),
        ],
        out_specs=pl.BlockSpec((None, TQ, LANES), lambda bi, hb, qi: (bi, qi, hb)),
        out_shape=jax.ShapeDtypeStruct((b, s, ATTN_WIDTH), jnp.bfloat16),
        scratch_shapes=[pltpu.VMEM((HEADS_PER_BLOCK, TQ, LANES), jnp.float32),
                        pltpu.VMEM((HEADS_PER_BLOCK, TQ, 1), jnp.float32),
                        pltpu.VMEM((HEADS_PER_BLOCK, TQ, 1), jnp.float32)],
        compiler_params=pltpu.CompilerParams(
            dimension_semantics=("arbitrary", "arbitrary", "arbitrary"),
            vmem_limit_bytes=VMEM_LIMIT),
        name="fox_attn",
    )(proj3, proj3, proj3, c3, ct)


def _post_kernel(osb_ref, ofx_ref, gsb_ref, gfx_ref, x_ref, wosb_ref, wofx_ref, wout_ref,
                 gn_ref, wrh_ref, wrl_ref, br_ref,
                 x1_ref, h_ref, ids_ref, wts_ref):
    y_sb = jnp.dot(osb_ref[...], wosb_ref[...], preferred_element_type=jnp.float32)
    y_fx = jnp.dot(ofx_ref[...], wofx_ref[...], preferred_element_type=jnp.float32)
    mixed = (jax.nn.sigmoid(gsb_ref[...].astype(jnp.float32)) * y_sb
             + jax.nn.sigmoid(gfx_ref[...].astype(jnp.float32)) * y_fx)
    x1 = x_ref[...] + jnp.dot(mixed.astype(jnp.bfloat16), wout_ref[...],
                              preferred_element_type=jnp.float32)
    x1_ref[...] = x1
    y = _rmsnorm_f32(x1, gn_ref[...])
    h_hi, h_lo = _split_bf16(y, 2)
    h_ref[...] = h_hi

    lg = (lax.dot_general(wrh_ref[...], h_hi, _NT, preferred_element_type=jnp.float32)
          + lax.dot_general(wrh_ref[...], h_lo, _NT, preferred_element_type=jnp.float32)
          + lax.dot_general(wrl_ref[...], h_hi, _NT, preferred_element_type=jnp.float32))
    lg = lg + br_ref[...]
    tm = lg.shape[1]
    sub = lax.broadcasted_iota(jnp.int32, (8, tm), 0)
    neg = -jnp.inf

    gl = jnp.where(sub < N_GROUPS, lg[0:8], neg)
    gm = jnp.max(gl, axis=0, keepdims=True)
    g_w = 1.0 / jnp.sum(jnp.exp(gl - gm), axis=0, keepdims=True)
    g_idx = jnp.min(jnp.where(gl == gm, sub, 8), axis=0, keepdims=True)

    e_sel = lg[8:16]
    for g in range(1, N_GROUPS):
        e_sel = jnp.where(g_idx == g, lg[8 + 8 * g:16 + 8 * g], e_sel)
    m1 = jnp.max(e_sel, axis=0, keepdims=True)
    i1 = jnp.min(jnp.where(e_sel == m1, sub, 8), axis=0, keepdims=True)
    e_rest = jnp.where(sub == i1, neg, e_sel)
    m2 = jnp.max(e_rest, axis=0, keepdims=True)
    i2 = jnp.min(jnp.where(e_rest == m2, sub, 8), axis=0, keepdims=True)
    p2 = jnp.exp(m2 - m1)
    w1 = g_w / (1.0 + p2)
    w2 = g_w * p2 / (1.0 + p2)
    base = g_idx * EXPERTS_PER_GROUP
    ids_ref[...] = jnp.where(sub == 0, base + i1, jnp.where(sub == 1, base + i2, 0))
    wts_ref[...] = jnp.where(sub == 0, w1, jnp.where(sub == 1, w2, 0.0))


def _post_attention(o_sb, o_fx, proj, x2, wosb, wofx, wout, gn, wr_hi, wr_lo, br):
    n, d = x2.shape
    tm = TM_POST
    gate_blk = (3 * ATTN_WIDTH * 2) // d
    const = dict(pipeline_mode=pl.Buffered(1))
    return pl.pallas_call(
        _post_kernel,
        grid=(n // tm,),
        in_specs=[
            pl.BlockSpec((tm, ATTN_WIDTH), lambda i: (i, 0)),
            pl.BlockSpec((tm, ATTN_WIDTH), lambda i: (i, 0)),
            pl.BlockSpec((tm, d), lambda i: (i, gate_blk)),
            pl.BlockSpec((tm, d), lambda i: (i, gate_blk + 1)),
            pl.BlockSpec((tm, d), lambda i: (i, 0)),
            pl.BlockSpec((ATTN_WIDTH, d), lambda i: (0, 0), **const),
            pl.BlockSpec((ATTN_WIDTH, d), lambda i: (0, 0), **const),
            pl.BlockSpec((d, d), lambda i: (0, 0), **const),
            pl.BlockSpec((1, d), lambda i: (0, 0), **const),
            pl.BlockSpec((ROUTER_ROWS, d), lambda i: (0, 0), **const),
            pl.BlockSpec((ROUTER_ROWS, d), lambda i: (0, 0), **const),
            pl.BlockSpec((ROUTER_ROWS, 1), lambda i: (0, 0), **const),
        ],
        out_specs=[
            pl.BlockSpec((tm, d), lambda i: (i, 0)),
            pl.BlockSpec((tm, d), lambda i: (i, 0)),
            pl.BlockSpec((8, tm), lambda i: (0, i)),
            pl.BlockSpec((8, tm), lambda i: (0, i)),
        ],
        out_shape=[
            jax.ShapeDtypeStruct((n, d), jnp.float32),
            jax.ShapeDtypeStruct((n, d), jnp.bfloat16),
            jax.ShapeDtypeStruct((8, n), jnp.int32),
            jax.ShapeDtypeStruct((8, n), jnp.float32),
        ],
        compiler_params=pltpu.CompilerParams(
            dimension_semantics=("arbitrary",), vmem_limit_bytes=VMEM_LIMIT),
        name="post_attn",
    )(o_sb, o_fx, proj, proj, x2, wosb, wofx, wout, gn, wr_hi, wr_lo, br)


def _moe_kernel(h_ref, x1_ref, comb_ref, w13_ref, w2_ref, gn_ref, o_ref, acc_ref, *, d_expert):
    e = pl.program_id(1)

    @pl.when(e == 0)
    def _():
        acc_ref[...] = x1_ref[...]

    ab = jnp.dot(h_ref[...], w13_ref[...], preferred_element_type=jnp.float32)
    a = ab[:, :d_expert]
    hidden = (a * jax.nn.sigmoid(a)) * ab[:, d_expert:]
    y = jnp.dot(hidden.astype(jnp.bfloat16), w2_ref[...], preferred_element_type=jnp.float32)
    acc_ref[...] += comb_ref[...] * y

    @pl.when(e == pl.num_programs(1) - 1)
    def _():
        o_ref[...] = _rmsnorm_f32(acc_ref[...], gn_ref[...])


def _moe_dense(h, x1, comb, w13, w2, gn):
    n, d = x1.shape
    ne, _, two_de = w13.shape
    tm = TM_MOE
    kern = functools.partial(_moe_kernel, d_expert=two_de // 2)
    return pl.pallas_call(
        kern,
        grid=(n // tm, ne),
        in_specs=[
            pl.BlockSpec((tm, d), lambda i, e: (i, 0)),
            pl.BlockSpec((tm, d), lambda i, e: (i, 0)),
            pl.BlockSpec((None, tm, 1), lambda i, e: (e, i, 0)),
            pl.BlockSpec((None, d, two_de), lambda i, e: (e, 0, 0)),
            pl.BlockSpec((None, two_de // 2, d), lambda i, e: (e, 0, 0)),
            pl.BlockSpec((1, d), lambda i, e: (0, 0)),
        ],
        out_specs=pl.BlockSpec((tm, d), lambda i, e: (i, 0)),
        out_shape=jax.ShapeDtypeStruct((n, d), jnp.float32),
        scratch_shapes=[pltpu.VMEM((tm, d), jnp.float32)],
        compiler_params=pltpu.CompilerParams(
            dimension_semantics=("arbitrary", "arbitrary"), vmem_limit_bytes=VMEM_LIMIT),
        name="moe",
    )(h, x1, comb, w13, w2, gn)


def kernel(x, norm_attn, w_in, b_forget, w_o_sb, w_o_fox, w_out, norm_ffn, w_router_group,
           b_router_group, w_router_expert, b_router_expert, w1, w3, w2, norm_final):
    b, s, d = x.shape
    n = b * s
    depth = w_in.shape[0]
    bf16 = jnp.bfloat16
    n_main = 6 * ATTN_WIDTH
    x2 = x.reshape(n, d)
    for l in range(depth):
        w_l = w_in[l]
        w_main = jnp.concatenate([w_l[:, :n_main], w_l[:, n_main + N_HEADS:]], axis=1).astype(bf16)
        wf = jnp.pad(w_l[:, n_main:n_main + N_HEADS], ((0, 0), (0, LANES - N_HEADS)))
        wf_hi = wf.astype(bf16)
        wf_lo = (wf - wf_hi.astype(jnp.float32)).astype(bf16)
        bf = jnp.pad(b_forget[l], (0, LANES - N_HEADS)).reshape(1, LANES)
        wr = jnp.concatenate([w_router_group[l].T, jnp.zeros((8 - N_GROUPS, d), jnp.float32),
                              w_router_expert[l].T], axis=0)
        wr_hi = wr.astype(bf16)
        wr_lo = (wr - wr_hi.astype(jnp.float32)).astype(bf16)
        br = jnp.concatenate([b_router_group[l], jnp.zeros((8 - N_GROUPS,), jnp.float32),
                              b_router_expert[l]]).reshape(ROUTER_ROWS, 1)
        w13 = jnp.concatenate([w1[l], w3[l]], axis=-1).astype(bf16)

        proj, c, ct = _inproj(x2, norm_attn[l].reshape(1, d), w_main, wf_hi, wf_lo, bf, s)
        proj3 = proj.reshape(b, s, proj.shape[1])
        o_sb = _sb_attention(proj3, 0, N_HEAD_BLOCKS, 2 * N_HEAD_BLOCKS)
        o_fx = _fox_attention(proj3, c.reshape(b, s, LANES), ct,
                              3 * N_HEAD_BLOCKS, 4 * N_HEAD_BLOCKS, 5 * N_HEAD_BLOCKS)
        x1, h, ids, wts = _post_attention(
            o_sb.reshape(n, ATTN_WIDTH), o_fx.reshape(n, ATTN_WIDTH), proj, x2,
            w_o_sb[l].astype(bf16), w_o_fox[l].astype(bf16), w_out[l].astype(bf16),
            norm_ffn[l].reshape(1, d), wr_hi, wr_lo, br)
        experts = jnp.arange(N_EXPERTS, dtype=jnp.int32)[:, None]
        comb = (jnp.where(ids[0][None, :] == experts, wts[0][None, :], 0.0)
                + jnp.where(ids[1][None, :] == experts, wts[1][None, :], 0.0))
        last = l == depth - 1
        gn = norm_final.reshape(1, d) if last else jnp.ones((1, d), jnp.float32)
        x2 = _moe_dense(h, x1, comb.reshape(N_EXPERTS, n, 1), w13, w2[l].astype(bf16), gn)
    return x2.reshape(b, s, d)
```

```python
import functools
import math

import jax
import jax.numpy as jnp
from jax import lax
from jax.experimental import pallas as pl
from jax.experimental.pallas import tpu as pltpu

HEAD_DIM = 64
N_HEADS = 8
ATTN_WIDTH = N_HEADS * HEAD_DIM
N_GROUPS = 4
EXPERTS_PER_GROUP = 8
N_EXPERTS = N_GROUPS * EXPERTS_PER_GROUP
RMS_EPS = 1e-6
LANES = 128
HEADS_PER_BLOCK = LANES // HEAD_DIM
N_HEAD_BLOCKS = N_HEADS // HEADS_PER_BLOCK
ROUTER_ROWS = 8 + N_EXPERTS
VMEM_LIMIT = 56 * 1024 * 1024

TM_PROJ = 512
TQ = 256
TK = 256
TM_POST = 256
TM_MOE = 512

_NT = (((1,), (1,)), ((), ()))
LOG2E = math.log2(math.e)
Q_SCALE = LOG2E / math.sqrt(HEAD_DIM)


def _split_bf16(v, parts):
    out = []
    r = v
    for i in range(parts):
        p = r.astype(jnp.bfloat16)
        out.append(p)
        if i + 1 < parts:
            r = r - p.astype(jnp.float32)
    return out


def _rmsnorm_f32(x, g):
    ms = jnp.mean(x * x, axis=-1, keepdims=True)
    return x * lax.rsqrt(ms + RMS_EPS) * g


def _inproj_kernel(x_ref, g_ref, w_ref, wfh_ref, wfl_ref, bf_ref,
                   proj_ref, c_ref, ct_ref, carry_ref, *, tiles_per_seq, n_chunks, chunk, q_chunks):
    i = pl.program_id(0)

    @pl.when(i % tiles_per_seq == 0)
    def _():
        carry_ref[...] = jnp.zeros_like(carry_ref)

    y = _rmsnorm_f32(x_ref[...], g_ref[...])
    h_hi, h_lo = _split_bf16(y, 2)
    for c in range(n_chunks):
        sl = slice(c * chunk, (c + 1) * chunk)
        p = jnp.dot(h_hi, w_ref[:, sl], preferred_element_type=jnp.float32)
        if c in q_chunks:
            p = p * Q_SCALE
        proj_ref[:, sl] = p.astype(proj_ref.dtype)

    f = (jnp.dot(h_hi, wfh_ref[...], preferred_element_type=jnp.float32)
         + jnp.dot(h_lo, wfh_ref[...], preferred_element_type=jnp.float32)
         + jnp.dot(h_hi, wfl_ref[...], preferred_element_type=jnp.float32))
    f = f + bf_ref[...]
    logf = (jnp.minimum(f, 0.0) - jnp.log1p(jnp.exp(-jnp.abs(f)))) * LOG2E

    tm = logf.shape[0]
    row = lax.broadcasted_iota(jnp.int32, (tm, tm), 0)
    col = lax.broadcasted_iota(jnp.int32, (tm, tm), 1)
    tri = (col <= row).astype(jnp.bfloat16)
    cum = carry_ref[...]
    for p in _split_bf16(logf, 3):
        cum = cum + jnp.dot(tri, p, preferred_element_type=jnp.float32)
    c_ref[...] = cum
    ct_ref[...] = cum.T[:N_HEADS, :]
    carry_ref[...] = cum[tm - 1:tm, :]


def _inproj(x2, g, w_main, wf_hi, wf_lo, bf, seq):
    n, d = x2.shape
    cols = w_main.shape[1]
    tm = TM_PROJ
    chunk = ATTN_WIDTH
    kern = functools.partial(_inproj_kernel, tiles_per_seq=seq // tm,
                             n_chunks=cols // chunk, chunk=chunk, q_chunks=(0, 3))
    const = dict(pipeline_mode=pl.Buffered(1))
    return pl.pallas_call(
        kern,
        grid=(n // tm,),
        in_specs=[
            pl.BlockSpec((tm, d), lambda i: (i, 0)),
            pl.BlockSpec((1, d), lambda i: (0, 0), **const),
            pl.BlockSpec((d, cols), lambda i: (0, 0), **const),
            pl.BlockSpec((d, LANES), lambda i: (0, 0), **const),
            pl.BlockSpec((d, LANES), lambda i: (0, 0), **const),
            pl.BlockSpec((1, LANES), lambda i: (0, 0), **const),
        ],
        out_specs=[
            pl.BlockSpec((tm, cols), lambda i: (i, 0)),
            pl.BlockSpec((tm, LANES), lambda i: (i, 0)),
            pl.BlockSpec((N_HEADS, tm), lambda i: (0, i)),
        ],
        out_shape=[
            jax.ShapeDtypeStruct((n, cols), jnp.bfloat16),
            jax.ShapeDtypeStruct((n, LANES), jnp.float32),
            jax.ShapeDtypeStruct((N_HEADS, n), jnp.float32),
        ],
        scratch_shapes=[pltpu.VMEM((1, LANES), jnp.float32)],
        compiler_params=pltpu.CompilerParams(
            dimension_semantics=("arbitrary",), vmem_limit_bytes=VMEM_LIMIT),
        name="inproj",
    )(x2, g, w_main, wf_hi, wf_lo, bf)


def _stack_heads(q):
    lane = lax.broadcasted_iota(jnp.int32, q.shape, 1)
    return jnp.concatenate(
        [jnp.where((lane // HEAD_DIM) == h, q, jnp.zeros_like(q)) for h in range(HEADS_PER_BLOCK)], axis=0)


def _unstack_heads(acc):
    tq = acc.shape[0] // HEADS_PER_BLOCK
    lane = lax.broadcasted_iota(jnp.int32, (tq, LANES), 1)
    out = acc[:tq]
    for h in range(1, HEADS_PER_BLOCK):
        out = jnp.where((lane // HEAD_DIM) == h, acc[h * tq:(h + 1) * tq], out)
    return out


def _neg_abs(x):
    bits = lax.bitcast_convert_type(x, jnp.uint32) | jnp.uint32(0x80000000)
    return lax.bitcast_convert_type(bits, jnp.float32)


def _lane_tile(x, width):
    return jnp.concatenate([x] * (width // LANES), axis=1)


def _pipelined_tiles(qi, first, overlap, last):
    first(qi, 0)

    def pair(p, _):
        kj = qi - 2 * p
        overlap(kj, 0)
        overlap(kj - 1, 1)
        return 0

    lax.fori_loop(0, qi // 2, pair, 0)

    @pl.when(qi % 2 == 1)
    def _():
        overlap(1, 0)
        last(0, 1)

    @pl.when(qi % 2 == 0)
    def _():
        last(0, 0)


def _sb_kernel(q_ref, k_ref, v_ref, o_ref, acc_ref, carry_ref, lb_ref, lk_ref, rs_ref):
    qi = pl.program_id(2)
    qcat = _stack_heads(q_ref[...])
    m = HEADS_PER_BLOCK * TQ
    urow = lax.broadcasted_iota(jnp.int32, (TK, TK), 0)
    ucol = lax.broadcasted_iota(jnp.int32, (TK, TK), 1)
    upper = (ucol < urow).astype(jnp.bfloat16)

    acc_ref[...] = jnp.zeros_like(acc_ref)
    carry_ref[...] = jnp.zeros_like(carry_ref)

    def score(kj, slot, diag):
        k = k_ref[pl.ds(kj * TK, TK), :]
        z = lax.dot_general(qcat, k, _NT, preferred_element_type=jnp.float32)
        t = jnp.log2(1.0 + jnp.exp2(_neg_abs(z)))
        log_beta = jnp.minimum(z, 0.0) - t
        log_keep = log_beta - z
        if diag:
            row = lax.broadcasted_iota(jnp.int32, (m, TK), 0)
            col = lax.broadcasted_iota(jnp.int32, (m, TK), 1)
            below = col < (row % TQ)
            log_keep = jnp.where(below, log_keep, 0.0)
            log_beta = jnp.where(below, log_beta, -jnp.inf)
        lb_ref[slot] = log_beta
        lk_ref[slot] = log_keep.astype(jnp.bfloat16)
        rs_ref[slot] = jnp.broadcast_to(jnp.sum(log_keep, axis=-1, keepdims=True), (m, LANES))

    def finish(kj, slot):
        v = v_ref[pl.ds(kj * TK, TK), :]
        rest = jnp.dot(lk_ref[slot], upper, preferred_element_type=jnp.float32)
        carry = carry_ref[...]
        a = jnp.exp2(lb_ref[slot] + rest + _lane_tile(carry, TK))
        acc_ref[...] += jnp.dot(a.astype(v.dtype), v, preferred_element_type=jnp.float32)
        carry_ref[...] = carry + rs_ref[slot]

    def overlap(kj, slot):
        score(kj - 1, 1 - slot, False)
        finish(kj, slot)

    _pipelined_tiles(qi, lambda kj, slot: score(kj, slot, True), overlap, finish)
    o_ref[...] = _unstack_heads(acc_ref[...]).astype(o_ref.dtype)


def _sb_attention(proj3, q_blk, k_blk, v_blk):
    b, s, _ = proj3.shape
    return pl.pallas_call(
        _sb_kernel,
        grid=(b, N_HEAD_BLOCKS, s // TQ),
        in_specs=[
            pl.BlockSpec((None, TQ, LANES), lambda bi, hb, qi: (bi, qi, q_blk + hb)),
            pl.BlockSpec((None, s, LANES), lambda bi, hb, qi: (bi, 0, k_blk + hb)),
            pl.BlockSpec((None, s, LANES), lambda bi, hb, qi: (bi, 0, v_blk + hb)),
        ],
        out_specs=pl.BlockSpec((None, TQ, LANES), lambda bi, hb, qi: (bi, qi, hb)),
        out_shape=jax.ShapeDtypeStruct((b, s, ATTN_WIDTH), jnp.bfloat16),
        scratch_shapes=[pltpu.VMEM((HEADS_PER_BLOCK * TQ, LANES), jnp.float32),
                        pltpu.VMEM((HEADS_PER_BLOCK * TQ, LANES), jnp.float32),
                        pltpu.VMEM((2, HEADS_PER_BLOCK * TQ, TK), jnp.float32),
                        pltpu.VMEM((2, HEADS_PER_BLOCK * TQ, TK), jnp.bfloat16),
                        pltpu.VMEM((2, HEADS_PER_BLOCK * TQ, LANES), jnp.float32)],
        compiler_params=pltpu.CompilerParams(
            dimension_semantics=("arbitrary", "arbitrary", "arbitrary"),
            vmem_limit_bytes=VMEM_LIMIT),
        name="sb_attn",
    )(proj3, proj3, proj3)


def _fox_kernel(q_ref, k_ref, v_ref, c_ref, ct_ref, o_ref, acc_ref, m_ref, l_ref, z_ref, rm_ref):
    hb = pl.program_id(1)
    qi = pl.program_id(2)
    qcat = _stack_heads(q_ref[...])
    lane = lax.broadcasted_iota(jnp.int32, (TQ, LANES), 1)
    cblk = c_ref[...]
    cq = jnp.concatenate(
        [jnp.broadcast_to(
            jnp.sum(jnp.where(lane == hb * HEADS_PER_BLOCK + h, cblk, 0.0), axis=-1, keepdims=True),
            (TQ, TK)) for h in range(HEADS_PER_BLOCK)], axis=0)
    m = HEADS_PER_BLOCK * TQ
    ones = jnp.ones((TK, LANES), jnp.bfloat16)

    acc_ref[...] = jnp.zeros_like(acc_ref)
    l_ref[...] = jnp.zeros_like(l_ref)
    m_ref[...] = jnp.full_like(m_ref, -jnp.inf)

    def score(kj, slot, diag):
        k = k_ref[pl.ds(kj * TK, TK), :]
        ck = jnp.concatenate(
            [jnp.broadcast_to(ct_ref[pl.ds(hb * HEADS_PER_BLOCK + h, 1), pl.ds(kj * TK, TK)], (TQ, TK))
             for h in range(HEADS_PER_BLOCK)], axis=0)
        z = lax.dot_general(qcat, k, _NT, preferred_element_type=jnp.float32)
        z = (z + cq) - ck
        if diag:
            row = lax.broadcasted_iota(jnp.int32, (m, TK), 0)
            col = lax.broadcasted_iota(jnp.int32, (m, TK), 1)
            z = jnp.where(col <= (row % TQ), z, -jnp.inf)
        z_ref[slot] = z
        rm_ref[slot] = jnp.broadcast_to(jnp.max(z, axis=-1, keepdims=True), (m, LANES))

    def finish(kj, slot):
        v = v_ref[pl.ds(kj * TK, TK), :]
        m_old = m_ref[...]
        m_new = jnp.maximum(m_old, rm_ref[slot])
        alpha = jnp.exp2(m_old - m_new)
        p = jnp.exp2(z_ref[slot] - _lane_tile(m_new, TK))
        pv = jnp.dot(p.astype(v.dtype), jnp.concatenate([v, ones], axis=1),
                     preferred_element_type=jnp.float32)
        acc_ref[...] = alpha * acc_ref[...] + pv[:, :LANES]
        l_ref[...] = alpha * l_ref[...] + pv[:, LANES:]
        m_ref[...] = m_new

    def overlap(kj, slot):
        score(kj - 1, 1 - slot, False)
        finish(kj, slot)

    _pipelined_tiles(qi, lambda kj, slot: score(kj, slot, True), overlap, finish)
    o_ref[...] = _unstack_heads(acc_ref[...] / l_ref[...]).astype(o_ref.dtype)


def _fox_attention(proj3, c3, ct, q_blk, k_blk, v_blk):
    b, s, _ = proj3.shape
    return pl.pallas_call(
        _fox_kernel,
        grid=(b, N_HEAD_BLOCKS, s // TQ),
        in_specs=[
            pl.BlockSpec((None, TQ, LANES), lambda bi, hb, qi: (bi, qi, q_blk + hb)),
            pl.BlockSpec((None, s, LANES), lambda bi, hb, qi: (bi, 0, k_blk + hb)),
            pl.BlockSpec((None, s, LANES), lambda bi, hb, qi: (bi, 0, v_blk + hb)),
            pl.BlockSpec((None, TQ, LANES), lambda bi, hb, qi: (bi, qi, 0)),
            pl.BlockSpec((N_HEADS, s), lambda bi, hb, qi: (0, bi)),
        ],
        out_specs=pl.BlockSpec((None, TQ, LANES), lambda bi, hb, qi: (bi, qi, hb)),
        out_shape=jax.ShapeDtypeStruct((b, s, ATTN_WIDTH), jnp.bfloat16),
        scratch_shapes=[pltpu.VMEM((HEADS_PER_BLOCK * TQ, LANES), jnp.float32),
                        pltpu.VMEM((HEADS_PER_BLOCK * TQ, LANES), jnp.float32),
                        pltpu.VMEM((HEADS_PER_BLOCK * TQ, LANES), jnp.float32),
                        pltpu.VMEM((2, HEADS_PER_BLOCK * TQ, TK), jnp.float32),
                        pltpu.VMEM((2, HEADS_PER_BLOCK * TQ, LANES), jnp.float32)],
        compiler_params=pltpu.CompilerParams(
            dimension_semantics=("arbitrary", "arbitrary", "arbitrary"),
            vmem_limit_bytes=VMEM_LIMIT),
        name="fox_attn",
    )(proj3, proj3, proj3, c3, ct)


def _post_kernel(osb_ref, ofx_ref, gsb_ref, gfx_ref, x_ref, wosb_ref, wofx_ref, wout_ref,
                 gn_ref, wrh_ref, wrl_ref, br_ref,
                 x1_ref, h_ref, ids_ref, wts_ref):
    y_sb = jnp.dot(osb_ref[...], wosb_ref[...], preferred_element_type=jnp.float32)
    y_fx = jnp.dot(ofx_ref[...], wofx_ref[...], preferred_element_type=jnp.float32)
    mixed = (jax.nn.sigmoid(gsb_ref[...].astype(jnp.float32)) * y_sb
             + jax.nn.sigmoid(gfx_ref[...].astype(jnp.float32)) * y_fx)
    x1 = x_ref[...] + jnp.dot(mixed.astype(jnp.bfloat16), wout_ref[...],
                              preferred_element_type=jnp.float32)
    x1_ref[...] = x1
    y = _rmsnorm_f32(x1, gn_ref[...])
    h_hi, h_lo = _split_bf16(y, 2)
    h_ref[...] = h_hi

    lg = (lax.dot_general(wrh_ref[...], h_hi, _NT, preferred_element_type=jnp.float32)
          + lax.dot_general(wrh_ref[...], h_lo, _NT, preferred_element_type=jnp.float32)
          + lax.dot_general(wrl_ref[...], h_hi, _NT, preferred_element_type=jnp.float32))
    lg = lg + br_ref[...]
    tm = lg.shape[1]
    sub = lax.broadcasted_iota(jnp.int32, (8, tm), 0)
    neg = -jnp.inf

    gl = jnp.where(sub < N_GROUPS, lg[0:8], neg)
    gm = jnp.max(gl, axis=0, keepdims=True)
    g_w = 1.0 / jnp.sum(jnp.exp(gl - gm), axis=0, keepdims=True)
    g_idx = jnp.min(jnp.where(gl == gm, sub, 8), axis=0, keepdims=True)

    e_sel = lg[8:16]
    for g in range(1, N_GROUPS):
        e_sel = jnp.where(g_idx == g, lg[8 + 8 * g:16 + 8 * g], e_sel)
    m1 = jnp.max(e_sel, axis=0, keepdims=True)
    i1 = jnp.min(jnp.where(e_sel == m1, sub, 8), axis=0, keepdims=True)
    e_rest = jnp.where(sub == i1, neg, e_sel)
    m2 = jnp.max(e_rest, axis=0, keepdims=True)
    i2 = jnp.min(jnp.where(e_rest == m2, sub, 8), axis=0, keepdims=True)
    p2 = jnp.exp(m2 - m1)
    w1 = g_w / (1.0 + p2)
    w2 = g_w * p2 / (1.0 + p2)
    base = g_idx * EXPERTS_PER_GROUP
    ids_ref[...] = jnp.where(sub == 0, base + i1, jnp.where(sub == 1, base + i2, 0))
    wts_ref[...] = jnp.where(sub == 0, w1, jnp.where(sub == 1, w2, 0.0))


def _post_attention(o_sb, o_fx, proj, x2, wosb, wofx, wout, gn, wr_hi, wr_lo, br):
    n, d = x2.shape
    tm = TM_POST
    gate_blk = (3 * ATTN_WIDTH * 2) // d
    const = dict(pipeline_mode=pl.Buffered(1))
    return pl.pallas_call(
        _post_kernel,
        grid=(n // tm,),
        in_specs=[
            pl.BlockSpec((tm, ATTN_WIDTH), lambda i: (i, 0)),
            pl.BlockSpec((tm, ATTN_WIDTH), lambda i: (i, 0)),
            pl.BlockSpec((tm, d), lambda i: (i, gate_blk)),
            pl.BlockSpec((tm, d), lambda i: (i, gate_blk + 1)),
            pl.BlockSpec((tm, d), lambda i: (i, 0)),
            pl.BlockSpec((ATTN_WIDTH, d), lambda i: (0, 0), **const),
            pl.BlockSpec((ATTN_WIDTH, d), lambda i: (0, 0), **const),
            pl.BlockSpec((d, d), lambda i: (0, 0), **const),
            pl.BlockSpec((1, d), lambda i: (0, 0), **const),
            pl.BlockSpec((ROUTER_ROWS, d), lambda i: (0, 0), **const),
            pl.BlockSpec((ROUTER_ROWS, d), lambda i: (0, 0), **const),
            pl.BlockSpec((ROUTER_ROWS, 1), lambda i: (0, 0), **const),
        ],
        out_specs=[
            pl.BlockSpec((tm, d), lambda i: (i, 0)),
            pl.BlockSpec((tm, d), lambda i: (i, 0)),
            pl.BlockSpec((8, tm), lambda i: (0, i)),
            pl.BlockSpec((8, tm), lambda i: (0, i)),
        ],
        out_shape=[
            jax.ShapeDtypeStruct((n, d), jnp.float32),
            jax.ShapeDtypeStruct((n, d), jnp.bfloat16),
            jax.ShapeDtypeStruct((8, n), jnp.int32),
            jax.ShapeDtypeStruct((8, n), jnp.float32),
        ],
        compiler_params=pltpu.CompilerParams(
            dimension_semantics=("arbitrary",), vmem_limit_bytes=VMEM_LIMIT),
        name="post_attn",
    )(o_sb, o_fx, proj, proj, x2, wosb, wofx, wout, gn, wr_hi, wr_lo, br)


def _moe_kernel(h_ref, x1_ref, comb_ref, w13_ref, w2_ref, gn_ref, o_ref, acc_ref, *, d_expert):
    e = pl.program_id(1)

    @pl.when(e == 0)
    def _():
        acc_ref[...] = x1_ref[...]

    ab = jnp.dot(h_ref[...], w13_ref[...], preferred_element_type=jnp.float32)
    a = ab[:, :d_expert]
    hidden = (a * jax.nn.sigmoid(a)) * ab[:, d_expert:]
    y = jnp.dot(hidden.astype(jnp.bfloat16), w2_ref[...], preferred_element_type=jnp.float32)
    acc_ref[...] += comb_ref[...] * y

    @pl.when(e == pl.num_programs(1) - 1)
    def _():
        o_ref[...] = _rmsnorm_f32(acc_ref[...], gn_ref[...])


def _moe_dense(h, x1, comb, w13, w2, gn):
    n, d = x1.shape
    ne, _, two_de = w13.shape
    tm = TM_MOE
    kern = functools.partial(_moe_kernel, d_expert=two_de // 2)
    return pl.pallas_call(
        kern,
        grid=(n // tm, ne),
        in_specs=[
            pl.BlockSpec((tm, d), lambda i, e: (i, 0)),
            pl.BlockSpec((tm, d), lambda i, e: (i, 0)),
            pl.BlockSpec((None, tm, 1), lambda i, e: (e, i, 0)),
            pl.BlockSpec((None, d, two_de), lambda i, e: (e, 0, 0)),
            pl.BlockSpec((None, two_de // 2, d), lambda i, e: (e, 0, 0)),
            pl.BlockSpec((1, d), lambda i, e: (0, 0)),
        ],
        out_specs=pl.BlockSpec((tm, d), lambda i, e: (i, 0)),
        out_shape=jax.ShapeDtypeStruct((n, d), jnp.float32),
        scratch_shapes=[pltpu.VMEM((tm, d), jnp.float32)],
        compiler_params=pltpu.CompilerParams(
            dimension_semantics=("arbitrary", "arbitrary"), vmem_limit_bytes=VMEM_LIMIT),
        name="moe",
    )(h, x1, comb, w13, w2, gn)


def kernel(x, norm_attn, w_in, b_forget, w_o_sb, w_o_fox, w_out, norm_ffn, w_router_group,
           b_router_group, w_router_expert, b_router_expert, w1, w3, w2, norm_final):
    b, s, d = x.shape
    n = b * s
    depth = w_in.shape[0]
    bf16 = jnp.bfloat16
    n_main = 6 * ATTN_WIDTH
    x2 = x.reshape(n, d)
    for l in range(depth):
        w_l = w_in[l]
        w_main = jnp.concatenate([w_l[:, :n_main], w_l[:, n_main + N_HEADS:]], axis=1).astype(bf16)
        wf = jnp.pad(w_l[:, n_main:n_main + N_HEADS], ((0, 0), (0, LANES - N_HEADS)))
        wf_hi = wf.astype(bf16)
        wf_lo = (wf - wf_hi.astype(jnp.float32)).astype(bf16)
        bf = jnp.pad(b_forget[l], (0, LANES - N_HEADS)).reshape(1, LANES)
        wr = jnp.concatenate([w_router_group[l].T, jnp.zeros((8 - N_GROUPS, d), jnp.float32),
                              w_router_expert[l].T], axis=0)
        wr_hi = wr.astype(bf16)
        wr_lo = (wr - wr_hi.astype(jnp.float32)).astype(bf16)
        br = jnp.concatenate([b_router_group[l], jnp.zeros((8 - N_GROUPS,), jnp.float32),
                              b_router_expert[l]]).reshape(ROUTER_ROWS, 1)
        w13 = jnp.concatenate([w1[l], w3[l]], axis=-1).astype(bf16)

        proj, c, ct = _inproj(x2, norm_attn[l].reshape(1, d), w_main, wf_hi, wf_lo, bf, s)
        proj3 = proj.reshape(b, s, proj.shape[1])
        o_sb = _sb_attention(proj3, 0, N_HEAD_BLOCKS, 2 * N_HEAD_BLOCKS)
        o_fx = _fox_attention(proj3, c.reshape(b, s, LANES), ct,
                              3 * N_HEAD_BLOCKS, 4 * N_HEAD_BLOCKS, 5 * N_HEAD_BLOCKS)
        x1, h, ids, wts = _post_attention(
            o_sb.reshape(n, ATTN_WIDTH), o_fx.reshape(n, ATTN_WIDTH), proj, x2,
            w_o_sb[l].astype(bf16), w_o_fox[l].astype(bf16), w_out[l].astype(bf16),
            norm_ffn[l].reshape(1, d), wr_hi, wr_lo, br)
        experts = jnp.arange(N_EXPERTS, dtype=jnp.int32)[:, None]
        comb = (jnp.where(ids[0][None, :] == experts, wts[0][None, :], 0.0)
                + jnp.where(ids[1][None, :] == experts, wts[1][None, :], 0.0))
        last = l == depth - 1
        gn = norm_final.reshape(1, d) if last else jnp.ones((1, d), jnp.float32)
        x2 = _moe_dense(h, x1, comb.reshape(N_EXPERTS, n, 1), w13, w2[l].astype(bf16), gn)
    return x2.reshape(b, s, d)
```

```python
import functools
import math

import jax
import jax.numpy as jnp
from jax import lax
from jax.experimental import pallas as pl
from jax.experimental.pallas import tpu as pltpu

HEAD_DIM = 64
N_HEADS = 8
ATTN_WIDTH = N_HEADS * HEAD_DIM
N_GROUPS = 4
EXPERTS_PER_GROUP = 8
N_EXPERTS = N_GROUPS * EXPERTS_PER_GROUP
RMS_EPS = 1e-6
LANES = 128
HEADS_PER_BLOCK = LANES // HEAD_DIM
N_HEAD_BLOCKS = N_HEADS // HEADS_PER_BLOCK
ROUTER_ROWS = 8 + N_EXPERTS
VMEM_LIMIT = 56 * 1024 * 1024

TM_PROJ = 512
TQ = 256
TK = 256
TM_POST = 256

_NT = (((1,), (1,)), ((), ()))
LOG2E = math.log2(math.e)
Q_SCALE = LOG2E / math.sqrt(HEAD_DIM)
SKIP_LOG2 = 160.0


def _split_bf16(v, parts):
    out = []
    r = v
    for i in range(parts):
        p = r.astype(jnp.bfloat16)
        out.append(p)
        if i + 1 < parts:
            r = r - p.astype(jnp.float32)
    return out


def _rmsnorm_f32(x, g):
    ms = jnp.mean(x * x, axis=-1, keepdims=True)
    return x * lax.rsqrt(ms + RMS_EPS) * g


def _inproj_kernel(x_ref, g_ref, w_ref, wfh_ref, wfl_ref, bf_ref,
                   proj_ref, c_ref, ct_ref, carry_ref, *, tiles_per_seq, n_chunks, chunk, q_chunks):
    i = pl.program_id(0)

    @pl.when(i % tiles_per_seq == 0)
    def _():
        carry_ref[...] = jnp.zeros_like(carry_ref)

    y = _rmsnorm_f32(x_ref[...], g_ref[...])
    h_hi, h_lo = _split_bf16(y, 2)
    for c in range(n_chunks):
        sl = slice(c * chunk, (c + 1) * chunk)
        p = jnp.dot(h_hi, w_ref[:, sl], preferred_element_type=jnp.float32)
        if c in q_chunks:
            p = p * Q_SCALE
        proj_ref[:, sl] = p.astype(proj_ref.dtype)

    f = (jnp.dot(h_hi, wfh_ref[...], preferred_element_type=jnp.float32)
         + jnp.dot(h_lo, wfh_ref[...], preferred_element_type=jnp.float32)
         + jnp.dot(h_hi, wfl_ref[...], preferred_element_type=jnp.float32))
    f = f + bf_ref[...]
    logf = (jnp.minimum(f, 0.0) - jnp.log1p(jnp.exp(-jnp.abs(f)))) * LOG2E

    tm = logf.shape[0]
    row = lax.broadcasted_iota(jnp.int32, (tm, tm), 0)
    col = lax.broadcasted_iota(jnp.int32, (tm, tm), 1)
    tri = (col <= row).astype(jnp.bfloat16)
    cum = carry_ref[...]
    for p in _split_bf16(logf, 3):
        cum = cum + jnp.dot(tri, p, preferred_element_type=jnp.float32)
    c_ref[...] = cum
    ct_ref[...] = cum.T[:N_HEADS, :]
    carry_ref[...] = cum[tm - 1:tm, :]


def _inproj(x2, g, w_main, wf_hi, wf_lo, bf, seq):
    n, d = x2.shape
    cols = w_main.shape[1]
    tm = TM_PROJ
    chunk = ATTN_WIDTH
    kern = functools.partial(_inproj_kernel, tiles_per_seq=seq // tm,
                             n_chunks=cols // chunk, chunk=chunk, q_chunks=(0, 3))
    const = dict(pipeline_mode=pl.Buffered(1))
    return pl.pallas_call(
        kern,
        grid=(n // tm,),
        in_specs=[
            pl.BlockSpec((tm, d), lambda i: (i, 0)),
            pl.BlockSpec((1, d), lambda i: (0, 0), **const),
            pl.BlockSpec((d, cols), lambda i: (0, 0), **const),
            pl.BlockSpec((d, LANES), lambda i: (0, 0), **const),
            pl.BlockSpec((d, LANES), lambda i: (0, 0), **const),
            pl.BlockSpec((1, LANES), lambda i: (0, 0), **const),
        ],
        out_specs=[
            pl.BlockSpec((tm, cols), lambda i: (i, 0)),
            pl.BlockSpec((tm, LANES), lambda i: (i, 0)),
            pl.BlockSpec((N_HEADS, tm), lambda i: (0, i)),
        ],
        out_shape=[
            jax.ShapeDtypeStruct((n, cols), jnp.bfloat16),
            jax.ShapeDtypeStruct((n, LANES), jnp.float32),
            jax.ShapeDtypeStruct((N_HEADS, n), jnp.float32),
        ],
        scratch_shapes=[pltpu.VMEM((1, LANES), jnp.float32)],
        compiler_params=pltpu.CompilerParams(
            dimension_semantics=("arbitrary",), vmem_limit_bytes=VMEM_LIMIT),
        name="inproj",
    )(x2, g, w_main, wf_hi, wf_lo, bf)


def _stack_heads(q):
    lane = lax.broadcasted_iota(jnp.int32, q.shape, 1)
    return jnp.concatenate(
        [jnp.where((lane // HEAD_DIM) == h, q, jnp.zeros_like(q)) for h in range(HEADS_PER_BLOCK)], axis=0)


def _unstack_heads(acc):
    tq = acc.shape[0] // HEADS_PER_BLOCK
    lane = lax.broadcasted_iota(jnp.int32, (tq, LANES), 1)
    out = acc[:tq]
    for h in range(1, HEADS_PER_BLOCK):
        out = jnp.where((lane // HEAD_DIM) == h, acc[h * tq:(h + 1) * tq], out)
    return out


def _neg_abs(x):
    bits = lax.bitcast_convert_type(x, jnp.uint32) | jnp.uint32(0x80000000)
    return lax.bitcast_convert_type(bits, jnp.float32)


def _lane_tile(x, width):
    return jnp.concatenate([x] * (width // LANES), axis=1)


def _pipelined_tiles(qi, first, overlap, last, live):
    first(qi, 0)

    def cond(state):
        p, go = state
        return jnp.logical_and(p < qi // 2, go != 0)

    def pair(state):
        p, _ = state
        kj = qi - 2 * p
        overlap(kj, 0)
        overlap(kj - 1, 1)
        return p + 1, live(kj - 1)

    _, go = lax.while_loop(cond, pair, (jnp.int32(0), jnp.int32(1)))

    @pl.when(jnp.logical_and(go != 0, qi % 2 == 1))
    def _():
        overlap(1, 0)
        last(0, 1)

    @pl.when(jnp.logical_and(go != 0, qi % 2 == 0))
    def _():
        last(0, 0)


def _sb_kernel(q_ref, k_ref, v_ref, o_ref, acc_ref, carry_ref, lb_ref, lk_ref, rs_ref):
    qi = pl.program_id(2)
    qcat = _stack_heads(q_ref[...])
    m = HEADS_PER_BLOCK * TQ
    urow = lax.broadcasted_iota(jnp.int32, (TK, TK), 0)
    ucol = lax.broadcasted_iota(jnp.int32, (TK, TK), 1)
    upper = (ucol < urow).astype(jnp.bfloat16)

    acc_ref[...] = jnp.zeros_like(acc_ref)
    carry_ref[...] = jnp.zeros_like(carry_ref)

    def score(kj, slot, diag):
        k = k_ref[pl.ds(kj * TK, TK), :]
        z = lax.dot_general(qcat, k, _NT, preferred_element_type=jnp.float32)
        t = jnp.log2(1.0 + jnp.exp2(_neg_abs(z)))
        log_beta = jnp.minimum(z, 0.0) - t
        log_keep = log_beta - z
        if diag:
            row = lax.broadcasted_iota(jnp.int32, (m, TK), 0)
            col = lax.broadcasted_iota(jnp.int32, (m, TK), 1)
            below = col < (row % TQ)
            log_keep = jnp.where(below, log_keep, 0.0)
            log_beta = jnp.where(below, log_beta, -jnp.inf)
        lb_ref[slot] = log_beta
        lk_ref[slot] = log_keep.astype(jnp.bfloat16)
        rs_ref[slot] = jnp.broadcast_to(jnp.sum(log_keep, axis=-1, keepdims=True), (m, LANES))

    def finish(kj, slot):
        v = v_ref[pl.ds(kj * TK, TK), :]
        rest = jnp.dot(lk_ref[slot], upper, preferred_element_type=jnp.float32)
        carry = carry_ref[...]
        a = jnp.exp2(lb_ref[slot] + rest + _lane_tile(carry, TK))
        acc_ref[...] += jnp.dot(a.astype(v.dtype), v, preferred_element_type=jnp.float32)
        carry_ref[...] = carry + rs_ref[slot]

    def overlap(kj, slot):
        score(kj - 1, 1 - slot, False)
        finish(kj, slot)

    def live(kj):
        return (jnp.max(carry_ref[...]) > -SKIP_LOG2).astype(jnp.int32)

    _pipelined_tiles(qi, lambda kj, slot: score(kj, slot, True), overlap, finish, live)
    o_ref[...] = _unstack_heads(acc_ref[...]).astype(o_ref.dtype)


def _sb_attention(proj3, q_blk, k_blk, v_blk):
    b, s, _ = proj3.shape
    return pl.pallas_call(
        _sb_kernel,
        grid=(b, N_HEAD_BLOCKS, s // TQ),
        in_specs=[
            pl.BlockSpec((None, TQ, LANES), lambda bi, hb, qi: (bi, qi, q_blk + hb)),
            pl.BlockSpec((None, s, LANES), lambda bi, hb, qi: (bi, 0, k_blk + hb)),
            pl.BlockSpec((None, s, LANES), lambda bi, hb, qi: (bi, 0, v_blk + hb)),
        ],
        out_specs=pl.BlockSpec((None, TQ, LANES), lambda bi, hb, qi: (bi, qi, hb)),
        out_shape=jax.ShapeDtypeStruct((b, s, ATTN_WIDTH), jnp.bfloat16),
        scratch_shapes=[pltpu.VMEM((HEADS_PER_BLOCK * TQ, LANES), jnp.float32),
                        pltpu.VMEM((HEADS_PER_BLOCK * TQ, LANES), jnp.float32),
                        pltpu.VMEM((2, HEADS_PER_BLOCK * TQ, TK), jnp.float32),
                        pltpu.VMEM((2, HEADS_PER_BLOCK * TQ, TK), jnp.bfloat16),
                        pltpu.VMEM((2, HEADS_PER_BLOCK * TQ, LANES), jnp.float32)],
        compiler_params=pltpu.CompilerParams(
            dimension_semantics=("arbitrary", "arbitrary", "arbitrary"),
            vmem_limit_bytes=VMEM_LIMIT),
        name="sb_attn",
    )(proj3, proj3, proj3)


def _fox_kernel(q_ref, k_ref, v_ref, c_ref, ct_ref, o_ref, acc_ref, m_ref, l_ref, z_ref, rm_ref,
                kn_ref):
    hb = pl.program_id(1)
    qi = pl.program_id(2)
    qcat = _stack_heads(q_ref[...])
    lane = lax.broadcasted_iota(jnp.int32, (TQ, LANES), 1)
    cblk = c_ref[...]
    cq = jnp.concatenate(
        [jnp.broadcast_to(
            jnp.sum(jnp.where(lane == hb * HEADS_PER_BLOCK + h, cblk, 0.0), axis=-1, keepdims=True),
            (TQ, TK)) for h in range(HEADS_PER_BLOCK)], axis=0)
    m = HEADS_PER_BLOCK * TQ
    ones = jnp.ones((TK, LANES), jnp.bfloat16)

    @pl.when(qi == 0)
    def _():
        kf = k_ref[...].astype(jnp.float32)
        ksq = kf * kf
        klane = lax.broadcasted_iota(jnp.int32, ksq.shape, 1)
        for h in range(HEADS_PER_BLOCK):
            n2 = jnp.sum(jnp.where((klane // HEAD_DIM) == h, ksq, 0.0), axis=-1, keepdims=True)
            kn_ref[h * TQ:(h + 1) * TQ, :] = jnp.broadcast_to(
                jnp.sqrt(jnp.max(n2, axis=0, keepdims=True)), (TQ, LANES))

    qf = qcat.astype(jnp.float32)
    qn = jnp.broadcast_to(jnp.sqrt(jnp.sum(qf * qf, axis=-1, keepdims=True)), (m, LANES))

    acc_ref[...] = jnp.zeros_like(acc_ref)
    l_ref[...] = jnp.zeros_like(l_ref)
    m_ref[...] = jnp.full_like(m_ref, -jnp.inf)

    def score(kj, slot, diag):
        k = k_ref[pl.ds(kj * TK, TK), :]
        ck = jnp.concatenate(
            [jnp.broadcast_to(ct_ref[pl.ds(hb * HEADS_PER_BLOCK + h, 1), pl.ds(kj * TK, TK)], (TQ, TK))
             for h in range(HEADS_PER_BLOCK)], axis=0)
        z = lax.dot_general(qcat, k, _NT, preferred_element_type=jnp.float32)
        z = (z + cq) - ck
        if diag:
            row = lax.broadcasted_iota(jnp.int32, (m, TK), 0)
            col = lax.broadcasted_iota(jnp.int32, (m, TK), 1)
            z = jnp.where(col <= (row % TQ), z, -jnp.inf)
        z_ref[slot] = z
        rm_ref[slot] = jnp.broadcast_to(jnp.max(z, axis=-1, keepdims=True), (m, LANES))

    def finish(kj, slot):
        v = v_ref[pl.ds(kj * TK, TK), :]
        m_old = m_ref[...]
        m_new = jnp.maximum(m_old, rm_ref[slot])
        alpha = jnp.exp2(m_old - m_new)
        p = jnp.exp2(z_ref[slot] - _lane_tile(m_new, TK))
        pv = jnp.dot(p.astype(v.dtype), jnp.concatenate([v, ones], axis=1),
                     preferred_element_type=jnp.float32)
        acc_ref[...] = alpha * acc_ref[...] + pv[:, :LANES]
        l_ref[...] = alpha * l_ref[...] + pv[:, LANES:]
        m_ref[...] = m_new

    def overlap(kj, slot):
        score(kj - 1, 1 - slot, False)
        finish(kj, slot)

    def live(kj):
        cb = jnp.concatenate(
            [jnp.broadcast_to(
                jnp.max(ct_ref[pl.ds(hb * HEADS_PER_BLOCK + h, 1), pl.ds(kj * TK, TK)],
                        axis=-1, keepdims=True), (TQ, LANES)) for h in range(HEADS_PER_BLOCK)], axis=0)
        bound = qn * kn_ref[...] + cq[:, :LANES] - cb - m_ref[...]
        return (jnp.max(bound) > -SKIP_LOG2).astype(jnp.int32)

    _pipelined_tiles(qi, lambda kj, slot: score(kj, slot, True), overlap, finish, live)
    o_ref[...] = _unstack_heads(acc_ref[...] / l_ref[...]).astype(o_ref.dtype)


def _fox_attention(proj3, c3, ct, q_blk, k_blk, v_blk):
    b, s, _ = proj3.shape
    return pl.pallas_call(
        _fox_kernel,
        grid=(b, N_HEAD_BLOCKS, s // TQ),
        in_specs=[
            pl.BlockSpec((None, TQ, LANES), lambda bi, hb, qi: (bi, qi, q_blk + hb)),
            pl.BlockSpec((None, s, LANES), lambda bi, hb, qi: (bi, 0, k_blk + hb)),
            pl.BlockSpec((None, s, LANES), lambda bi, hb, qi: (bi, 0, v_blk + hb)),
            pl.BlockSpec((None, TQ, LANES), lambda bi, hb, qi: (bi, qi, 0)),
            pl.BlockSpec((N_HEADS, s), lambda bi, hb, qi: (0, bi)),
        ],
        out_specs=pl.BlockSpec((None, TQ, LANES), lambda bi, hb, qi: (bi, qi, hb)),
        out_shape=jax.ShapeDtypeStruct((b, s, ATTN_WIDTH), jnp.bfloat16),
        scratch_shapes=[pltpu.VMEM((HEADS_PER_BLOCK * TQ, LANES), jnp.float32),
                        pltpu.VMEM((HEADS_PER_BLOCK * TQ, LANES), jnp.float32),
                        pltpu.VMEM((HEADS_PER_BLOCK * TQ, LANES), jnp.float32),
                        pltpu.VMEM((2, HEADS_PER_BLOCK * TQ, TK), jnp.float32),
                        pltpu.VMEM((2, HEADS_PER_BLOCK * TQ, LANES), jnp.float32),
                        pltpu.VMEM((HEADS_PER_BLOCK * TQ, LANES), jnp.float32)],
        compiler_params=pltpu.CompilerParams(
            dimension_semantics=("arbitrary", "arbitrary", "arbitrary"),
            vmem_limit_bytes=VMEM_LIMIT),
        name="fox_attn",
    )(proj3, proj3, proj3, c3, ct)


def _post_kernel(osb_ref, ofx_ref, gsb_ref, gfx_ref, x_ref, wosb_ref, wofx_ref, wout_ref,
                 gn_ref, wrh_ref, wrl_ref, br_ref,
                 x1_ref, h_ref, ids_ref, aux_ref):
    y_sb = jnp.dot(osb_ref[...], wosb_ref[...], preferred_element_type=jnp.float32)
    y_fx = jnp.dot(ofx_ref[...], wofx_ref[...], preferred_element_type=jnp.float32)
    mixed = (jax.nn.sigmoid(gsb_ref[...].astype(jnp.float32)) * y_sb
             + jax.nn.sigmoid(gfx_ref[...].astype(jnp.float32)) * y_fx)
    x1 = x_ref[...] + jnp.dot(mixed.astype(jnp.bfloat16), wout_ref[...],
                              preferred_element_type=jnp.float32)
    x1_ref[...] = x1
    y = _rmsnorm_f32(x1, gn_ref[...])
    h_hi, h_lo = _split_bf16(y, 2)
    h_ref[...] = h_hi

    lg = (lax.dot_general(wrh_ref[...], h_hi, _NT, preferred_element_type=jnp.float32)
          + lax.dot_general(wrh_ref[...], h_lo, _NT, preferred_element_type=jnp.float32)
          + lax.dot_general(wrl_ref[...], h_hi, _NT, preferred_element_type=jnp.float32))
    lg = lg + br_ref[...]
    tm = lg.shape[1]
    sub = lax.broadcasted_iota(jnp.int32, (8, tm), 0)
    neg = -jnp.inf

    gl = jnp.where(sub < N_GROUPS, lg[0:8], neg)
    gm = jnp.max(gl, axis=0, keepdims=True)
    g_w = 1.0 / jnp.sum(jnp.exp(gl - gm), axis=0, keepdims=True)
    g_idx = jnp.min(jnp.where(gl == gm, sub, 8), axis=0, keepdims=True)

    e_sel = lg[8:16]
    for g in range(1, N_GROUPS):
        e_sel = jnp.where(g_idx == g, lg[8 + 8 * g:16 + 8 * g], e_sel)
    m1 = jnp.max(e_sel, axis=0, keepdims=True)
    i1 = jnp.min(jnp.where(e_sel == m1, sub, 8), axis=0, keepdims=True)
    e_rest = jnp.where(sub == i1, neg, e_sel)
    m2 = jnp.max(e_rest, axis=0, keepdims=True)
    i2 = jnp.min(jnp.where(e_rest == m2, sub, 8), axis=0, keepdims=True)
    p2 = jnp.exp(m2 - m1)
    w1 = g_w / (1.0 + p2)
    w2 = g_w * p2 / (1.0 + p2)
    base = g_idx * EXPERTS_PER_GROUP
    ids_ref[...] = jnp.where(sub == 0, base + i1, jnp.where(sub == 1, base + i2, 0))
    dense_w = jnp.where(sub == i1, w1, jnp.where(sub == i2, w2, 0.0))
    group_hot = jnp.where(sub == g_idx, 1.0, 0.0)
    record = jnp.concatenate(
        [dense_w, group_hot, jnp.zeros((LANES - 16, tm), jnp.float32)], axis=0)
    aux_ref[...] = record.T


def _post_attention(o_sb, o_fx, proj, x2, wosb, wofx, wout, gn, wr_hi, wr_lo, br):
    n, d = x2.shape
    tm = TM_POST
    gate_blk = (3 * ATTN_WIDTH * 2) // d
    const = dict(pipeline_mode=pl.Buffered(1))
    return pl.pallas_call(
        _post_kernel,
        grid=(n // tm,),
        in_specs=[
            pl.BlockSpec((tm, ATTN_WIDTH), lambda i: (i, 0)),
            pl.BlockSpec((tm, ATTN_WIDTH), lambda i: (i, 0)),
            pl.BlockSpec((tm, d), lambda i: (i, gate_blk)),
            pl.BlockSpec((tm, d), lambda i: (i, gate_blk + 1)),
            pl.BlockSpec((tm, d), lambda i: (i, 0)),
            pl.BlockSpec((ATTN_WIDTH, d), lambda i: (0, 0), **const),
            pl.BlockSpec((ATTN_WIDTH, d), lambda i: (0, 0), **const),
            pl.BlockSpec((d, d), lambda i: (0, 0), **const),
            pl.BlockSpec((1, d), lambda i: (0, 0), **const),
            pl.BlockSpec((ROUTER_ROWS, d), lambda i: (0, 0), **const),
            pl.BlockSpec((ROUTER_ROWS, d), lambda i: (0, 0), **const),
            pl.BlockSpec((ROUTER_ROWS, 1), lambda i: (0, 0), **const),
        ],
        out_specs=[
            pl.BlockSpec((tm, d), lambda i: (i, 0)),
            pl.BlockSpec((tm, d), lambda i: (i, 0)),
            pl.BlockSpec((8, tm), lambda i: (0, i)),
            pl.BlockSpec((tm, LANES), lambda i: (i, 0)),
        ],
        out_shape=[
            jax.ShapeDtypeStruct((n, d), jnp.float32),
            jax.ShapeDtypeStruct((n, d), jnp.bfloat16),
            jax.ShapeDtypeStruct((8, n), jnp.int32),
            jax.ShapeDtypeStruct((n, LANES), jnp.float32),
        ],
        compiler_params=pltpu.CompilerParams(
            dimension_semantics=("arbitrary",), vmem_limit_bytes=VMEM_LIMIT),
        name="post_attn",
    )(o_sb, o_fx, proj, proj, x2, wosb, wofx, wout, gn, wr_hi, wr_lo, br)


TM_SORT = 256
T_GROUP = 256
ROW_ALIGN = 16
ZERO_TAIL = TM_SORT + T_GROUP
GROUP_LANE0 = EXPERTS_PER_GROUP


def _sort_capacity(n):
    cap = n + ROW_ALIGN * (n // TM_SORT) + ZERO_TAIL
    return -(-cap // T_GROUP) * T_GROUP


def _max_group_tiles(n):
    padded = n + (ROW_ALIGN - 1) * N_GROUPS * (n // TM_SORT)
    return padded // T_GROUP + N_GROUPS * (ZERO_TAIL // T_GROUP + 1)


def _dispatch_kernel(h_ref, ids_ref, aux_ref, hg_ref, wg_ref, toff_ref, tot_ref,
                     hbuf, wbuf, sems, off_ref, prev_ref):
    i = pl.program_id(0)
    tm = TM_SORT

    def chunk_copies(g, off):
        off = pl.multiple_of(off, ROW_ALIGN)
        return (pltpu.make_async_copy(hbuf.at[g], hg_ref.at[g, pl.ds(off, tm)], sems.at[0, g]),
                pltpu.make_async_copy(wbuf.at[g], wg_ref.at[g, pl.ds(off, tm)], sems.at[1, g]))

    @pl.when(i == 0)
    def _():
        for g in range(N_GROUPS):
            off_ref[g] = 0

    @pl.when(i > 0)
    def _():
        for g in range(N_GROUPS):
            for c in chunk_copies(g, prev_ref[g]):
                c.wait()

    gid = ids_ref[0:1, :] // EXPERTS_PER_GROUP
    sub = lax.broadcasted_iota(jnp.int32, (8, tm), 0)
    hot = sub == gid
    r_io = lax.broadcasted_iota(jnp.int32, (tm, tm), 0)
    c_io = lax.broadcasted_iota(jnp.int32, (tm, tm), 1)
    before = (r_io < c_io).astype(jnp.bfloat16)
    seen = jnp.dot(jnp.where(hot, 1.0, 0.0).astype(jnp.bfloat16), before,
                   preferred_element_type=jnp.float32)
    rank = jnp.sum(jnp.where(hot, seen, 0.0), axis=0, keepdims=True).astype(jnp.int32)

    h = h_ref[...]
    aux_parts = _split_bf16(aux_ref[...], 3)
    sub_out = lax.broadcasted_iota(jnp.int32, (8, LANES), 0)
    offs = jnp.zeros((8, LANES), jnp.int32)
    for g in range(N_GROUPS):
        sel = jnp.where(jnp.logical_and(rank == r_io, gid == g), 1.0, 0.0).astype(jnp.bfloat16)
        hbuf[g] = jnp.dot(sel, h, preferred_element_type=jnp.float32).astype(hbuf.dtype)
        w = jnp.dot(sel, aux_parts[0], preferred_element_type=jnp.float32)
        for part in aux_parts[1:]:
            w = w + jnp.dot(sel, part, preferred_element_type=jnp.float32)
        wbuf[g] = w
        off = off_ref[g]
        for c in chunk_copies(g, off):
            c.start()
        offs = jnp.where(sub_out == g, off, offs)
        count = jnp.sum(jnp.where(gid == g, 1.0, 0.0)).astype(jnp.int32)
        prev_ref[g] = off
        off_ref[g] = off + (count + (ROW_ALIGN - 1)) // ROW_ALIGN * ROW_ALIGN
    toff_ref[...] = offs

    total = jnp.zeros((8, LANES), jnp.int32)
    for g in range(N_GROUPS):
        total = jnp.where(sub_out == g, off_ref[g], total)
    tot_ref[...] = total

    @pl.when(i == pl.num_programs(0) - 1)
    def _():
        for g in range(N_GROUPS):
            for c in chunk_copies(g, prev_ref[g]):
                c.wait()
        hbuf[...] = jnp.zeros_like(hbuf)
        wbuf[...] = jnp.zeros_like(wbuf)
        for g in range(N_GROUPS):
            for z in range(ZERO_TAIL // tm):
                for c in chunk_copies(g, off_ref[g] + z * tm):
                    c.start()
                for c in chunk_copies(g, off_ref[g] + z * tm):
                    c.wait()


def _dispatch(h, ids, aux):
    n, d = h.shape
    tm = TM_SORT
    cap = _sort_capacity(n)
    steps = n // tm
    return pl.pallas_call(
        _dispatch_kernel,
        grid=(steps,),
        in_specs=[
            pl.BlockSpec((tm, d), lambda i: (i, 0)),
            pl.BlockSpec((8, tm), lambda i: (0, i)),
            pl.BlockSpec((tm, LANES), lambda i: (i, 0)),
        ],
        out_specs=[
            pl.BlockSpec(memory_space=pl.ANY),
            pl.BlockSpec(memory_space=pl.ANY),
            pl.BlockSpec((8, LANES), lambda i: (i, 0)),
            pl.BlockSpec((8, LANES), lambda i: (0, 0)),
        ],
        out_shape=[
            jax.ShapeDtypeStruct((N_GROUPS, cap, d), jnp.bfloat16),
            jax.ShapeDtypeStruct((N_GROUPS, cap, LANES), jnp.float32),
            jax.ShapeDtypeStruct((steps * 8, LANES), jnp.int32),
            jax.ShapeDtypeStruct((8, LANES), jnp.int32),
        ],
        scratch_shapes=[pltpu.VMEM((N_GROUPS, tm, d), jnp.bfloat16),
                        pltpu.VMEM((N_GROUPS, tm, LANES), jnp.float32),
                        pltpu.SemaphoreType.DMA((2, N_GROUPS)),
                        pltpu.SMEM((N_GROUPS,), jnp.int32),
                        pltpu.SMEM((N_GROUPS,), jnp.int32)],
        compiler_params=pltpu.CompilerParams(
            dimension_semantics=("arbitrary",), vmem_limit_bytes=VMEM_LIMIT),
        name="moe_sort",
    )(h, ids, aux)


def _group_moe_kernel(tg_ref, tj_ref, tv_ref, h_ref, w_ref, w13_ref, w2_ref, y_ref):
    i = pl.program_id(0)

    @pl.when(tv_ref[i] != 0)
    def _():
        de = w2_ref.shape[0] // EXPERTS_PER_GROUP
        half = EXPERTS_PER_GROUP * de
        ab = jnp.dot(h_ref[...], w13_ref[...], preferred_element_type=jnp.float32)
        a = ab[:, :half]
        hidden = (a * jax.nn.sigmoid(a)) * ab[:, half:]
        w = w_ref[...]
        scaled = jnp.concatenate(
            [hidden[:, e * de:(e + 1) * de] * w[:, e:e + 1] for e in range(EXPERTS_PER_GROUP)], axis=1)
        y_ref[...] = jnp.dot(scaled.astype(jnp.bfloat16), w2_ref[...],
                             preferred_element_type=jnp.float32).astype(y_ref.dtype)


def _group_moe(tile_group, tile_index, tile_valid, hg, wg, w13g, w2g):
    ng, cap, d = hg.shape
    t = T_GROUP
    grid_spec = pltpu.PrefetchScalarGridSpec(
        num_scalar_prefetch=3,
        grid=(tile_group.shape[0],),
        in_specs=[
            pl.BlockSpec((None, t, d), lambda i, tg, tj, tv: (tg[i], tj[i], 0)),
            pl.BlockSpec((None, t, LANES), lambda i, tg, tj, tv: (tg[i], tj[i], 0)),
            pl.BlockSpec((None,) + w13g.shape[1:], lambda i, tg, tj, tv: (tg[i], 0, 0)),
            pl.BlockSpec((None,) + w2g.shape[1:], lambda i, tg, tj, tv: (tg[i], 0, 0)),
        ],
        out_specs=pl.BlockSpec((None, t, d), lambda i, tg, tj, tv: (tg[i], tj[i], 0)),
    )
    return pl.pallas_call(
        _group_moe_kernel,
        grid_spec=grid_spec,
        out_shape=jax.ShapeDtypeStruct((ng, cap, d), jnp.bfloat16),
        compiler_params=pltpu.CompilerParams(
            dimension_semantics=("arbitrary",), vmem_limit_bytes=VMEM_LIMIT),
        name="moe_experts",
    )(tile_group, tile_index, tile_valid, hg, wg, w13g, w2g)


def _combine_kernel(toff_ref, x1_ref, aux_ref, gn_ref, yg_ref, o_ref, ybuf, sems):
    i = pl.program_id(0)
    tm = TM_SORT
    slot = i % 2

    def fetch(step, slot):
        return [pltpu.make_async_copy(
            yg_ref.at[g, pl.ds(pl.multiple_of(toff_ref[step * N_GROUPS + g], ROW_ALIGN), tm)],
            ybuf.at[slot, g], sems.at[slot, g]) for g in range(N_GROUPS)]

    @pl.when(i == 0)
    def _():
        for c in fetch(0, 0):
            c.start()

    @pl.when(i + 1 < pl.num_programs(0))
    def _():
        for c in fetch(i + 1, 1 - slot):
            c.start()

    for c in fetch(i, slot):
        c.wait()

    aux = aux_ref[...]
    lane = lax.broadcasted_iota(jnp.int32, (tm, LANES), 1)
    is_group_lane = jnp.logical_and(lane >= GROUP_LANE0, lane < GROUP_LANE0 + N_GROUPS)
    hot = jnp.where(is_group_lane, aux, 0.0)
    r_io = lax.broadcasted_iota(jnp.int32, (tm, tm), 0)
    c_io = lax.broadcasted_iota(jnp.int32, (tm, tm), 1)
    before = (c_io < r_io).astype(jnp.bfloat16)
    seen = jnp.dot(before, hot.astype(jnp.bfloat16), preferred_element_type=jnp.float32)
    rank = jnp.sum(hot * seen, axis=-1, keepdims=True).astype(jnp.int32)
    y = jnp.zeros((tm, x1_ref.shape[1]), jnp.float32)
    for g in range(N_GROUPS):
        mine = aux[:, GROUP_LANE0 + g:GROUP_LANE0 + g + 1] > 0.5
        sel = jnp.where(jnp.logical_and(rank == c_io, mine), 1.0, 0.0).astype(jnp.bfloat16)
        y = y + jnp.dot(sel, ybuf[slot, g], preferred_element_type=jnp.float32)
    o_ref[...] = _rmsnorm_f32(x1_ref[...] + y, gn_ref[...])


def _combine(toff, x1, aux, gn, yg):
    n, d = x1.shape
    tm = TM_SORT
    grid_spec = pltpu.PrefetchScalarGridSpec(
        num_scalar_prefetch=1,
        grid=(n // tm,),
        in_specs=[
            pl.BlockSpec((tm, d), lambda i, toff: (i, 0)),
            pl.BlockSpec((tm, LANES), lambda i, toff: (i, 0)),
            pl.BlockSpec((1, d), lambda i, toff: (0, 0)),
            pl.BlockSpec(memory_space=pl.ANY),
        ],
        out_specs=pl.BlockSpec((tm, d), lambda i, toff: (i, 0)),
        scratch_shapes=[pltpu.VMEM((2, N_GROUPS, tm, d), jnp.bfloat16),
                        pltpu.SemaphoreType.DMA((2, N_GROUPS))],
    )
    return pl.pallas_call(
        _combine_kernel,
        grid_spec=grid_spec,
        out_shape=jax.ShapeDtypeStruct((n, d), jnp.float32),
        compiler_params=pltpu.CompilerParams(
            dimension_semantics=("arbitrary",), vmem_limit_bytes=VMEM_LIMIT),
        name="moe_combine",
    )(toff, x1, aux, gn, yg)


def _group_tile_table(totals, n):
    tiles = (totals + (TM_SORT + T_GROUP - 1)) // T_GROUP
    ends = jnp.cumsum(tiles)
    steps = jnp.arange(_max_group_tiles(n), dtype=jnp.int32)
    valid = steps < ends[-1]
    step = jnp.minimum(steps, ends[-1] - 1)
    group = jnp.sum((step[:, None] >= ends[None, :]).astype(jnp.int32), axis=1)
    index = step - (ends - tiles)[group]
    return group.astype(jnp.int32), index.astype(jnp.int32), valid.astype(jnp.int32)


def kernel(x, norm_attn, w_in, b_forget, w_o_sb, w_o_fox, w_out, norm_ffn, w_router_group,
           b_router_group, w_router_expert, b_router_expert, w1, w3, w2, norm_final):
    b, s, d = x.shape
    n = b * s
    depth = w_in.shape[0]
    assert depth == 1, "the final norm is fused into the MoE un-sort of a single layer"
    assert s % TM_PROJ == 0 and s % TQ == 0 and n % TM_SORT == 0 and ZERO_TAIL % TM_SORT == 0
    bf16 = jnp.bfloat16
    n_main = 6 * ATTN_WIDTH
    x2 = x.reshape(n, d)
    for l in range(depth):
        w_l = w_in[l]
        w_main = jnp.concatenate([w_l[:, :n_main], w_l[:, n_main + N_HEADS:]], axis=1).astype(bf16)
        wf = jnp.pad(w_l[:, n_main:n_main + N_HEADS], ((0, 0), (0, LANES - N_HEADS)))
        wf_hi = wf.astype(bf16)
        wf_lo = (wf - wf_hi.astype(jnp.float32)).astype(bf16)
        bf = jnp.pad(b_forget[l], (0, LANES - N_HEADS)).reshape(1, LANES)
        wr = jnp.concatenate([w_router_group[l].T, jnp.zeros((8 - N_GROUPS, d), jnp.float32),
                              w_router_expert[l].T], axis=0)
        wr_hi = wr.astype(bf16)
        wr_lo = (wr - wr_hi.astype(jnp.float32)).astype(bf16)
        br = jnp.concatenate([b_router_group[l], jnp.zeros((8 - N_GROUPS,), jnp.float32),
                              b_router_expert[l]]).reshape(ROUTER_ROWS, 1)
        de = w1.shape[-1]

        def by_group(w):
            return (w.astype(bf16).reshape(N_GROUPS, EXPERTS_PER_GROUP, d, de)
                    .transpose(0, 2, 1, 3).reshape(N_GROUPS, d, EXPERTS_PER_GROUP * de))

        w13g = jnp.concatenate([by_group(w1[l]), by_group(w3[l])], axis=-1)
        w2g = w2[l].astype(bf16).reshape(N_GROUPS, EXPERTS_PER_GROUP * de, d)

        proj, c, ct = _inproj(x2, norm_attn[l].reshape(1, d), w_main, wf_hi, wf_lo, bf, s)
        proj3 = proj.reshape(b, s, proj.shape[1])
        o_sb = _sb_attention(proj3, 0, N_HEAD_BLOCKS, 2 * N_HEAD_BLOCKS)
        o_fx = _fox_attention(proj3, c.reshape(b, s, LANES), ct,
                              3 * N_HEAD_BLOCKS, 4 * N_HEAD_BLOCKS, 5 * N_HEAD_BLOCKS)
        x1, h, ids, aux = _post_attention(
            o_sb.reshape(n, ATTN_WIDTH), o_fx.reshape(n, ATTN_WIDTH), proj, x2,
            w_o_sb[l].astype(bf16), w_o_fox[l].astype(bf16), w_out[l].astype(bf16),
            norm_ffn[l].reshape(1, d), wr_hi, wr_lo, br)
        hg, wg, toff, totals = _dispatch(h, ids, aux)
        tile_group, tile_index, tile_valid = _group_tile_table(totals[:N_GROUPS, 0], n)
        yg = _group_moe(tile_group, tile_index, tile_valid, hg, wg, w13g, w2g)
        chunk_starts = toff.reshape(n // TM_SORT, 8, LANES)[:, :N_GROUPS, 0].reshape(-1)
        x2 = _combine(chunk_starts, x1, aux, norm_final.reshape(1, d), yg)
    return x2.reshape(b, s, d)
```

```python
import functools
import math

import jax
import jax.numpy as jnp
from jax import lax
from jax.experimental import pallas as pl
from jax.experimental.pallas import tpu as pltpu

HEAD_DIM = 64
N_HEADS = 8
ATTN_WIDTH = N_HEADS * HEAD_DIM
N_GROUPS = 4
EXPERTS_PER_GROUP = 8
N_EXPERTS = N_GROUPS * EXPERTS_PER_GROUP
RMS_EPS = 1e-6
LANES = 128
HEADS_PER_BLOCK = LANES // HEAD_DIM
N_HEAD_BLOCKS = N_HEADS // HEADS_PER_BLOCK
ROUTER_ROWS = 8 + N_EXPERTS
VMEM_LIMIT = 56 * 1024 * 1024

TM_PROJ = 512
TQ = 256
TK = 256
TM_POST = 512

_NT = (((1,), (1,)), ((), ()))
LOG2E = math.log2(math.e)
Q_SCALE = LOG2E / math.sqrt(HEAD_DIM)
SKIP_LOG2 = 160.0


def _split_bf16(v, parts):
    out = []
    r = v
    for i in range(parts):
        p = r.astype(jnp.bfloat16)
        out.append(p)
        if i + 1 < parts:
            r = r - p.astype(jnp.float32)
    return out


def _rmsnorm_f32(x, g):
    ms = jnp.mean(x * x, axis=-1, keepdims=True)
    return x * lax.rsqrt(ms + RMS_EPS) * g


def _inproj_kernel(x_ref, g_ref, w_ref, wfh_ref, wfl_ref, bf_ref,
                   proj_ref, c_ref, ct_ref, carry_ref, *, tiles_per_seq, n_chunks, chunk, q_chunks):
    i = pl.program_id(0)

    @pl.when(i % tiles_per_seq == 0)
    def _():
        carry_ref[...] = jnp.zeros_like(carry_ref)

    y = _rmsnorm_f32(x_ref[...], g_ref[...])
    h_hi, h_lo = _split_bf16(y, 2)
    for c in range(n_chunks):
        sl = slice(c * chunk, (c + 1) * chunk)
        p = jnp.dot(h_hi, w_ref[:, sl], preferred_element_type=jnp.float32)
        if c in q_chunks:
            p = p * Q_SCALE
        proj_ref[:, sl] = p.astype(proj_ref.dtype)

    f = (jnp.dot(h_hi, wfh_ref[...], preferred_element_type=jnp.float32)
         + jnp.dot(h_lo, wfh_ref[...], preferred_element_type=jnp.float32)
         + jnp.dot(h_hi, wfl_ref[...], preferred_element_type=jnp.float32))
    f = f + bf_ref[...]
    logf = (jnp.minimum(f, 0.0) - jnp.log1p(jnp.exp(-jnp.abs(f)))) * LOG2E

    tm = logf.shape[0]
    row = lax.broadcasted_iota(jnp.int32, (tm, tm), 0)
    col = lax.broadcasted_iota(jnp.int32, (tm, tm), 1)
    tri = (col <= row).astype(jnp.bfloat16)
    cum = carry_ref[...]
    for p in _split_bf16(logf, 3):
        cum = cum + jnp.dot(tri, p, preferred_element_type=jnp.float32)
    c_ref[...] = cum
    ct_ref[...] = cum.T[:N_HEADS, :]
    carry_ref[...] = cum[tm - 1:tm, :]


def _inproj(x2, g, w_main, wf_hi, wf_lo, bf, seq):
    n, d = x2.shape
    cols = w_main.shape[1]
    tm = TM_PROJ
    chunk = ATTN_WIDTH
    kern = functools.partial(_inproj_kernel, tiles_per_seq=seq // tm,
                             n_chunks=cols // chunk, chunk=chunk, q_chunks=(0, 3))
    const = dict(pipeline_mode=pl.Buffered(1))
    return pl.pallas_call(
        kern,
        grid=(n // tm,),
        in_specs=[
            pl.BlockSpec((tm, d), lambda i: (i, 0)),
            pl.BlockSpec((1, d), lambda i: (0, 0), **const),
            pl.BlockSpec((d, cols), lambda i: (0, 0), **const),
            pl.BlockSpec((d, LANES), lambda i: (0, 0), **const),
            pl.BlockSpec((d, LANES), lambda i: (0, 0), **const),
            pl.BlockSpec((1, LANES), lambda i: (0, 0), **const),
        ],
        out_specs=[
            pl.BlockSpec((tm, cols), lambda i: (i, 0)),
            pl.BlockSpec((tm, LANES), lambda i: (i, 0)),
            pl.BlockSpec((N_HEADS, tm), lambda i: (0, i)),
        ],
        out_shape=[
            jax.ShapeDtypeStruct((n, cols), jnp.bfloat16),
            jax.ShapeDtypeStruct((n, LANES), jnp.float32),
            jax.ShapeDtypeStruct((N_HEADS, n), jnp.float32),
        ],
        scratch_shapes=[pltpu.VMEM((1, LANES), jnp.float32)],
        compiler_params=pltpu.CompilerParams(
            dimension_semantics=("arbitrary",), vmem_limit_bytes=VMEM_LIMIT),
        name="inproj",
    )(x2, g, w_main, wf_hi, wf_lo, bf)


def _stack_heads(q):
    lane = lax.broadcasted_iota(jnp.int32, q.shape, 1)
    return jnp.concatenate(
        [jnp.where((lane // HEAD_DIM) == h, q, jnp.zeros_like(q)) for h in range(HEADS_PER_BLOCK)], axis=0)


def _unstack_heads(acc):
    tq = acc.shape[0] // HEADS_PER_BLOCK
    lane = lax.broadcasted_iota(jnp.int32, (tq, LANES), 1)
    out = acc[:tq]
    for h in range(1, HEADS_PER_BLOCK):
        out = jnp.where((lane // HEAD_DIM) == h, acc[h * tq:(h + 1) * tq], out)
    return out


def _lane_tile(x, width):
    return jnp.concatenate([x] * (width // LANES), axis=1)


def _pipelined_tiles(qi, first, overlap, last, live):
    first(qi, 0)

    def stop_at(kj, slot):
        last(kj, slot)
        return jnp.int32(0)

    def two_tiles(kj):
        def second():
            overlap(kj - 1, 1)
            return jnp.int32(1)

        def both():
            overlap(kj, 0)
            return lax.cond(live(kj - 1, 1), second, lambda: stop_at(kj - 1, 1))

        return lax.cond(live(kj, 0), both, lambda: stop_at(kj, 0))

    def cond(state):
        p, go = state
        return jnp.logical_and(p < qi // 2, go != 0)

    def pair(state):
        p, _ = state
        return p + 1, two_tiles(qi - 2 * p)

    _, go = lax.while_loop(cond, pair, (jnp.int32(0), jnp.int32(1)))

    @pl.when(jnp.logical_and(go != 0, qi % 2 == 1))
    def _():
        def both():
            overlap(1, 0)
            return stop_at(0, 1)

        lax.cond(live(1, 0), both, lambda: stop_at(1, 0))

    @pl.when(jnp.logical_and(go != 0, qi % 2 == 0))
    def _():
        last(0, 0)


def _sb_kernel(q_ref, k_ref, v_ref, o_ref, acc_ref, carry_ref, lb_ref, lk_ref, rs_ref):
    qi = pl.program_id(2)
    qcat = _stack_heads(q_ref[...])
    m = HEADS_PER_BLOCK * TQ
    urow = lax.broadcasted_iota(jnp.int32, (TK, TK), 0)
    ucol = lax.broadcasted_iota(jnp.int32, (TK, TK), 1)
    upper = (ucol < urow).astype(jnp.bfloat16)

    acc_ref[...] = jnp.zeros_like(acc_ref)
    carry_ref[...] = jnp.zeros_like(carry_ref)

    def score(kj, slot, diag):
        k = k_ref[pl.ds(kj * TK, TK), :]
        z = lax.dot_general(qcat, k, _NT, preferred_element_type=jnp.float32)
        t = jnp.log2(1.0 + jnp.exp2(-jnp.abs(z)))
        log_beta = jnp.minimum(z, 0.0) - t
        log_keep = log_beta - z
        if diag:
            row = lax.broadcasted_iota(jnp.int32, (m, TK), 0)
            col = lax.broadcasted_iota(jnp.int32, (m, TK), 1)
            below = col < (row % TQ)
            log_keep = jnp.where(below, log_keep, 0.0)
            log_beta = jnp.where(below, log_beta, -jnp.inf)
        lb_ref[slot] = log_beta
        lk_ref[slot] = log_keep.astype(jnp.bfloat16)
        rs_ref[slot] = jnp.broadcast_to(jnp.sum(log_keep, axis=-1, keepdims=True), (m, LANES))

    def finish(kj, slot):
        v = v_ref[pl.ds(kj * TK, TK), :]
        rest = jnp.dot(lk_ref[slot], upper, preferred_element_type=jnp.float32)
        carry = carry_ref[...]
        a = jnp.exp2(lb_ref[slot] + rest + _lane_tile(carry, TK))
        acc_ref[...] += jnp.dot(a.astype(v.dtype), v, preferred_element_type=jnp.float32)
        carry_ref[...] = carry + rs_ref[slot]

    def overlap(kj, slot):
        score(kj - 1, 1 - slot, False)
        finish(kj, slot)

    def live(kj, slot):
        return jnp.max(carry_ref[...] + rs_ref[slot]) > -SKIP_LOG2

    _pipelined_tiles(qi, lambda kj, slot: score(kj, slot, True), overlap, finish, live)
    o_ref[...] = _unstack_heads(acc_ref[...]).astype(o_ref.dtype)


def _sb_attention(proj3, q_blk, k_blk, v_blk):
    b, s, _ = proj3.shape
    return pl.pallas_call(
        _sb_kernel,
        grid=(b, N_HEAD_BLOCKS, s // TQ),
        in_specs=[
            pl.BlockSpec((None, TQ, LANES), lambda bi, hb, qi: (bi, qi, q_blk + hb)),
            pl.BlockSpec((None, s, LANES), lambda bi, hb, qi: (bi, 0, k_blk + hb)),
            pl.BlockSpec((None, s, LANES), lambda bi, hb, qi: (bi, 0, v_blk + hb)),
        ],
        out_specs=pl.BlockSpec((None, TQ, LANES), lambda bi, hb, qi: (bi, qi, hb)),
        out_shape=jax.ShapeDtypeStruct((b, s, ATTN_WIDTH), jnp.bfloat16),
        scratch_shapes=[pltpu.VMEM((HEADS_PER_BLOCK * TQ, LANES), jnp.float32),
                        pltpu.VMEM((HEADS_PER_BLOCK * TQ, LANES), jnp.float32),
                        pltpu.VMEM((2, HEADS_PER_BLOCK * TQ, TK), jnp.float32),
                        pltpu.VMEM((2, HEADS_PER_BLOCK * TQ, TK), jnp.bfloat16),
                        pltpu.VMEM((2, HEADS_PER_BLOCK * TQ, LANES), jnp.float32)],
        compiler_params=pltpu.CompilerParams(
            dimension_semantics=("arbitrary", "arbitrary", "arbitrary"),
            vmem_limit_bytes=VMEM_LIMIT),
        name="sb_attn",
    )(proj3, proj3, proj3)


def _fox_kernel(q_ref, k_ref, v_ref, c_ref, ct_ref, o_ref, acc_ref, m_ref, l_ref, z_ref, rm_ref,
                kn_ref):
    hb = pl.program_id(1)
    qi = pl.program_id(2)
    qcat = _stack_heads(q_ref[...])
    lane = lax.broadcasted_iota(jnp.int32, (TQ, LANES), 1)
    cblk = c_ref[...]
    cq = jnp.concatenate(
        [jnp.broadcast_to(
            jnp.sum(jnp.where(lane == hb * HEADS_PER_BLOCK + h, cblk, 0.0), axis=-1, keepdims=True),
            (TQ, TK)) for h in range(HEADS_PER_BLOCK)], axis=0)
    m = HEADS_PER_BLOCK * TQ
    ones = jnp.ones((TK, LANES), jnp.bfloat16)

    @pl.when(qi == 0)
    def _():
        kf = k_ref[...].astype(jnp.float32)
        ksq = kf * kf
        klane = lax.broadcasted_iota(jnp.int32, ksq.shape, 1)
        for h in range(HEADS_PER_BLOCK):
            n2 = jnp.sum(jnp.where((klane // HEAD_DIM) == h, ksq, 0.0), axis=-1, keepdims=True)
            kn_ref[h * TQ:(h + 1) * TQ, :] = jnp.broadcast_to(
                jnp.sqrt(jnp.max(n2, axis=0, keepdims=True)), (TQ, LANES))

    qf = qcat.astype(jnp.float32)
    qn = jnp.broadcast_to(jnp.sqrt(jnp.sum(qf * qf, axis=-1, keepdims=True)), (m, LANES))

    acc_ref[...] = jnp.zeros_like(acc_ref)
    l_ref[...] = jnp.zeros_like(l_ref)
    m_ref[...] = jnp.full_like(m_ref, -jnp.inf)

    def score(kj, slot, diag):
        k = k_ref[pl.ds(kj * TK, TK), :]
        ck = jnp.concatenate(
            [jnp.broadcast_to(ct_ref[pl.ds(hb * HEADS_PER_BLOCK + h, 1), pl.ds(kj * TK, TK)], (TQ, TK))
             for h in range(HEADS_PER_BLOCK)], axis=0)
        z = lax.dot_general(qcat, k, _NT, preferred_element_type=jnp.float32)
        z = (z + cq) - ck
        if diag:
            row = lax.broadcasted_iota(jnp.int32, (m, TK), 0)
            col = lax.broadcasted_iota(jnp.int32, (m, TK), 1)
            z = jnp.where(col <= (row % TQ), z, -jnp.inf)
        z_ref[slot] = z
        rm_ref[slot] = jnp.broadcast_to(jnp.max(z, axis=-1, keepdims=True), (m, LANES))

    def finish(kj, slot):
        v = v_ref[pl.ds(kj * TK, TK), :]
        m_old = m_ref[...]
        m_new = jnp.maximum(m_old, rm_ref[slot])
        alpha = jnp.exp2(m_old - m_new)
        p = jnp.exp2(z_ref[slot] - _lane_tile(m_new, TK))
        pv = jnp.dot(p.astype(v.dtype), jnp.concatenate([v, ones], axis=1),
                     preferred_element_type=jnp.float32)
        acc_ref[...] = alpha * acc_ref[...] + pv[:, :LANES]
        l_ref[...] = alpha * l_ref[...] + pv[:, LANES:]
        m_ref[...] = m_new

    def overlap(kj, slot):
        score(kj - 1, 1 - slot, False)
        finish(kj, slot)

    def live(kj, slot):
        cb = jnp.concatenate(
            [jnp.broadcast_to(
                jnp.max(ct_ref[pl.ds(hb * HEADS_PER_BLOCK + h, 1), pl.ds(kj * TK, TK)],
                        axis=-1, keepdims=True), (TQ, LANES)) for h in range(HEADS_PER_BLOCK)], axis=0)
        m_after = jnp.maximum(m_ref[...], rm_ref[slot])
        bound = qn * kn_ref[...] + cq[:, :LANES] - cb - m_after
        return jnp.max(bound) > -SKIP_LOG2

    _pipelined_tiles(qi, lambda kj, slot: score(kj, slot, True), overlap, finish, live)
    o_ref[...] = _unstack_heads(acc_ref[...] / l_ref[...]).astype(o_ref.dtype)


def _fox_attention(proj3, c3, ct, q_blk, k_blk, v_blk):
    b, s, _ = proj3.shape
    return pl.pallas_call(
        _fox_kernel,
        grid=(b, N_HEAD_BLOCKS, s // TQ),
        in_specs=[
            pl.BlockSpec((None, TQ, LANES), lambda bi, hb, qi: (bi, qi, q_blk + hb)),
            pl.BlockSpec((None, s, LANES), lambda bi, hb, qi: (bi, 0, k_blk + hb)),
            pl.BlockSpec((None, s, LANES), lambda bi, hb, qi: (bi, 0, v_blk + hb)),
            pl.BlockSpec((None, TQ, LANES), lambda bi, hb, qi: (bi, qi, 0)),
            pl.BlockSpec((N_HEADS, s), lambda bi, hb, qi: (0, bi)),
        ],
        out_specs=pl.BlockSpec((None, TQ, LANES), lambda bi, hb, qi: (bi, qi, hb)),
        out_shape=jax.ShapeDtypeStruct((b, s, ATTN_WIDTH), jnp.bfloat16),
        scratch_shapes=[pltpu.VMEM((HEADS_PER_BLOCK * TQ, LANES), jnp.float32),
                        pltpu.VMEM((HEADS_PER_BLOCK * TQ, LANES), jnp.float32),
                        pltpu.VMEM((HEADS_PER_BLOCK * TQ, LANES), jnp.float32),
                        pltpu.VMEM((2, HEADS_PER_BLOCK * TQ, TK), jnp.float32),
                        pltpu.VMEM((2, HEADS_PER_BLOCK * TQ, LANES), jnp.float32),
                        pltpu.VMEM((HEADS_PER_BLOCK * TQ, LANES), jnp.float32)],
        compiler_params=pltpu.CompilerParams(
            dimension_semantics=("arbitrary", "arbitrary", "arbitrary"),
            vmem_limit_bytes=VMEM_LIMIT),
        name="fox_attn",
    )(proj3, proj3, proj3, c3, ct)


def _post_kernel(osb_ref, ofx_ref, gsb_ref, gfx_ref, x_ref, wosb_ref, wofx_ref, wout_ref,
                 gn_ref, wrh_ref, wrl_ref, br_ref,
                 x1_ref, h_ref, ids_ref, aux_ref):
    y_sb = jnp.dot(osb_ref[...], wosb_ref[...], preferred_element_type=jnp.float32)
    y_fx = jnp.dot(ofx_ref[...], wofx_ref[...], preferred_element_type=jnp.float32)
    mixed = (jax.nn.sigmoid(gsb_ref[...].astype(jnp.float32)) * y_sb
             + jax.nn.sigmoid(gfx_ref[...].astype(jnp.float32)) * y_fx)
    x1 = x_ref[...] + jnp.dot(mixed.astype(jnp.bfloat16), wout_ref[...],
                              preferred_element_type=jnp.float32)
    x1_ref[...] = x1
    y = _rmsnorm_f32(x1, gn_ref[...])
    h_hi, h_lo = _split_bf16(y, 2)
    h_ref[...] = h_hi

    lg = (lax.dot_general(wrh_ref[...], h_hi, _NT, preferred_element_type=jnp.float32)
          + lax.dot_general(wrh_ref[...], h_lo, _NT, preferred_element_type=jnp.float32)
          + lax.dot_general(wrl_ref[...], h_hi, _NT, preferred_element_type=jnp.float32))
    lg = lg + br_ref[...]
    tm = lg.shape[1]
    sub = lax.broadcasted_iota(jnp.int32, (8, tm), 0)
    neg = -jnp.inf

    gl = jnp.where(sub < N_GROUPS, lg[0:8], neg)
    gm = jnp.max(gl, axis=0, keepdims=True)
    g_w = 1.0 / jnp.sum(jnp.exp(gl - gm), axis=0, keepdims=True)
    g_idx = jnp.min(jnp.where(gl == gm, sub, 8), axis=0, keepdims=True)

    e_sel = lg[8:16]
    for g in range(1, N_GROUPS):
        e_sel = jnp.where(g_idx == g, lg[8 + 8 * g:16 + 8 * g], e_sel)
    m1 = jnp.max(e_sel, axis=0, keepdims=True)
    i1 = jnp.min(jnp.where(e_sel == m1, sub, 8), axis=0, keepdims=True)
    e_rest = jnp.where(sub == i1, neg, e_sel)
    m2 = jnp.max(e_rest, axis=0, keepdims=True)
    i2 = jnp.min(jnp.where(e_rest == m2, sub, 8), axis=0, keepdims=True)
    p2 = jnp.exp(m2 - m1)
    w1 = g_w / (1.0 + p2)
    w2 = g_w * p2 / (1.0 + p2)
    base = g_idx * EXPERTS_PER_GROUP
    ids_ref[...] = jnp.where(sub == 0, base + i1, jnp.where(sub == 1, base + i2, 0))
    dense_w = jnp.where(sub == i1, w1, jnp.where(sub == i2, w2, 0.0))
    group_hot = jnp.where(sub == g_idx, 1.0, 0.0)
    record = jnp.concatenate(
        [dense_w, group_hot, jnp.zeros((LANES - 16, tm), jnp.float32)], axis=0)
    aux_ref[...] = record.T


def _post_attention(o_sb, o_fx, proj, x2, wosb, wofx, wout, gn, wr_hi, wr_lo, br):
    n, d = x2.shape
    tm = TM_POST
    gate_blk = (3 * ATTN_WIDTH * 2) // d
    const = dict(pipeline_mode=pl.Buffered(1))
    return pl.pallas_call(
        _post_kernel,
        grid=(n // tm,),
        in_specs=[
            pl.BlockSpec((tm, ATTN_WIDTH), lambda i: (i, 0)),
            pl.BlockSpec((tm, ATTN_WIDTH), lambda i: (i, 0)),
            pl.BlockSpec((tm, d), lambda i: (i, gate_blk)),
            pl.BlockSpec((tm, d), lambda i: (i, gate_blk + 1)),
            pl.BlockSpec((tm, d), lambda i: (i, 0)),
            pl.BlockSpec((ATTN_WIDTH, d), lambda i: (0, 0), **const),
            pl.BlockSpec((ATTN_WIDTH, d), lambda i: (0, 0), **const),
            pl.BlockSpec((d, d), lambda i: (0, 0), **const),
            pl.BlockSpec((1, d), lambda i: (0, 0), **const),
            pl.BlockSpec((ROUTER_ROWS, d), lambda i: (0, 0), **const),
            pl.BlockSpec((ROUTER_ROWS, d), lambda i: (0, 0), **const),
            pl.BlockSpec((ROUTER_ROWS, 1), lambda i: (0, 0), **const),
        ],
        out_specs=[
            pl.BlockSpec((tm, d), lambda i: (i, 0)),
            pl.BlockSpec((tm, d), lambda i: (i, 0)),
            pl.BlockSpec((8, tm), lambda i: (0, i)),
            pl.BlockSpec((tm, LANES), lambda i: (i, 0)),
        ],
        out_shape=[
            jax.ShapeDtypeStruct((n, d), jnp.float32),
            jax.ShapeDtypeStruct((n, d), jnp.bfloat16),
            jax.ShapeDtypeStruct((8, n), jnp.int32),
            jax.ShapeDtypeStruct((n, LANES), jnp.float32),
        ],
        compiler_params=pltpu.CompilerParams(
            dimension_semantics=("arbitrary",), vmem_limit_bytes=VMEM_LIMIT),
        name="post_attn",
    )(o_sb, o_fx, proj, proj, x2, wosb, wofx, wout, gn, wr_hi, wr_lo, br)


TM_SORT = 256
T_GROUP = 256
ROW_ALIGN = 16
ZERO_TAIL = TM_SORT + T_GROUP
SORT_ROWS = TM_SORT + N_GROUPS * ROW_ALIGN
GROUP_LANE0 = EXPERTS_PER_GROUP


def _sort_capacity(n):
    cap = n + ROW_ALIGN * (n // TM_SORT) + ZERO_TAIL
    return -(-cap // T_GROUP) * T_GROUP


def _max_group_tiles(n):
    padded = n + (ROW_ALIGN - 1) * N_GROUPS * (n // TM_SORT)
    return padded // T_GROUP + N_GROUPS * (ZERO_TAIL // T_GROUP + 1)


def _dispatch_kernel(h_ref, ids_ref, aux_ref, hg_ref, wg_ref, toff_ref, tot_ref,
                     hbuf, wbuf, sems, off_ref, prev_ref, src_ref):
    i = pl.program_id(0)
    tm = TM_SORT

    rows = SORT_ROWS

    def chunk_copies(g, src, off):
        src = src if isinstance(src, int) else pl.multiple_of(src, ROW_ALIGN)
        off = pl.multiple_of(off, ROW_ALIGN)
        return (pltpu.make_async_copy(hbuf.at[pl.ds(src, tm)], hg_ref.at[g, pl.ds(off, tm)],
                                      sems.at[0, g]),
                pltpu.make_async_copy(wbuf.at[pl.ds(src, tm)], wg_ref.at[g, pl.ds(off, tm)],
                                      sems.at[1, g]))

    @pl.when(i == 0)
    def _():
        for g in range(N_GROUPS):
            off_ref[g] = 0
        hbuf[...] = jnp.zeros_like(hbuf)
        wbuf[...] = jnp.zeros_like(wbuf)

    @pl.when(i > 0)
    def _():
        for g in range(N_GROUPS):
            for c in chunk_copies(g, src_ref[g], prev_ref[g]):
                c.wait()

    gid = ids_ref[0:1, :] // EXPERTS_PER_GROUP
    sub = lax.broadcasted_iota(jnp.int32, (8, tm), 0)
    hot = sub == gid
    r_io = lax.broadcasted_iota(jnp.int32, (tm, tm), 0)
    c_io = lax.broadcasted_iota(jnp.int32, (tm, tm), 1)
    before = (r_io < c_io).astype(jnp.bfloat16)
    seen = jnp.dot(jnp.where(hot, 1.0, 0.0).astype(jnp.bfloat16), before,
                   preferred_element_type=jnp.float32)
    pos = jnp.sum(jnp.where(hot, seen, 0.0), axis=0, keepdims=True).astype(jnp.int32)

    sub_out = lax.broadcasted_iota(jnp.int32, (8, LANES), 0)
    offs = jnp.zeros((8, LANES), jnp.int32)
    start = jnp.int32(0)
    for g in range(N_GROUPS):
        count = jnp.sum(jnp.where(gid == g, 1.0, 0.0)).astype(jnp.int32)
        pos = pos + jnp.where(gid == g, start, 0)
        off = off_ref[g]
        offs = jnp.where(sub_out == g, off, offs)
        src_ref[g] = start
        prev_ref[g] = off
        padded = (count + (ROW_ALIGN - 1)) // ROW_ALIGN * ROW_ALIGN
        off_ref[g] = off + padded
        start = start + padded
    toff_ref[...] = offs

    s_io = lax.broadcasted_iota(jnp.int32, (rows, tm), 0)
    sel = jnp.where(pos == s_io, 1.0, 0.0).astype(jnp.bfloat16)
    hbuf[0:rows] = jnp.dot(sel, h_ref[...], preferred_element_type=jnp.float32).astype(hbuf.dtype)
    aux3 = jnp.concatenate(_split_bf16(aux_ref[...], 3), axis=1)
    w3 = jnp.dot(sel, aux3, preferred_element_type=jnp.float32)
    wbuf[0:rows] = w3[:, :LANES] + w3[:, LANES:2 * LANES] + w3[:, 2 * LANES:]
    for g in range(N_GROUPS):
        for c in chunk_copies(g, src_ref[g], prev_ref[g]):
            c.start()

    total = jnp.zeros((8, LANES), jnp.int32)
    for g in range(N_GROUPS):
        total = jnp.where(sub_out == g, off_ref[g], total)
    tot_ref[...] = total

    @pl.when(i == pl.num_programs(0) - 1)
    def _():
        for g in range(N_GROUPS):
            for c in chunk_copies(g, src_ref[g], prev_ref[g]):
                c.wait()
        hbuf[...] = jnp.zeros_like(hbuf)
        wbuf[...] = jnp.zeros_like(wbuf)
        for g in range(N_GROUPS):
            for z in range(ZERO_TAIL // tm):
                for c in chunk_copies(g, 0, off_ref[g] + z * tm):
                    c.start()
                for c in chunk_copies(g, 0, off_ref[g] + z * tm):
                    c.wait()


def _dispatch(h, ids, aux):
    n, d = h.shape
    tm = TM_SORT
    cap = _sort_capacity(n)
    steps = n // tm
    return pl.pallas_call(
        _dispatch_kernel,
        grid=(steps,),
        in_specs=[
            pl.BlockSpec((tm, d), lambda i: (i, 0)),
            pl.BlockSpec((8, tm), lambda i: (0, i)),
            pl.BlockSpec((tm, LANES), lambda i: (i, 0)),
        ],
        out_specs=[
            pl.BlockSpec(memory_space=pl.ANY),
            pl.BlockSpec(memory_space=pl.ANY),
            pl.BlockSpec((8, LANES), lambda i: (i, 0)),
            pl.BlockSpec((8, LANES), lambda i: (0, 0)),
        ],
        out_shape=[
            jax.ShapeDtypeStruct((N_GROUPS, cap, d), jnp.bfloat16),
            jax.ShapeDtypeStruct((N_GROUPS, cap, LANES), jnp.float32),
            jax.ShapeDtypeStruct((steps * 8, LANES), jnp.int32),
            jax.ShapeDtypeStruct((8, LANES), jnp.int32),
        ],
        scratch_shapes=[pltpu.VMEM((SORT_ROWS + tm, d), jnp.bfloat16),
                        pltpu.VMEM((SORT_ROWS + tm, LANES), jnp.float32),
                        pltpu.SemaphoreType.DMA((2, N_GROUPS)),
                        pltpu.SMEM((N_GROUPS,), jnp.int32),
                        pltpu.SMEM((N_GROUPS,), jnp.int32),
                        pltpu.SMEM((N_GROUPS,), jnp.int32)],
        compiler_params=pltpu.CompilerParams(
            dimension_semantics=("arbitrary",), vmem_limit_bytes=VMEM_LIMIT),
        name="moe_sort",
    )(h, ids, aux)


def _group_moe_kernel(tg_ref, tj_ref, tv_ref, h_ref, w_ref, w1_ref, w3_ref, w2_ref, y_ref):
    i = pl.program_id(0)

    @pl.when(tv_ref[i] != 0)
    def _():
        h = h_ref[...]
        w = w_ref[...]
        hidden = []
        for e in range(EXPERTS_PER_GROUP):
            a = jnp.dot(h, w1_ref[e], preferred_element_type=jnp.float32)
            b = jnp.dot(h, w3_ref[e], preferred_element_type=jnp.float32)
            hidden.append(((a * jax.nn.sigmoid(a)) * b * w[:, e:e + 1]).astype(jnp.bfloat16))
        ne, de, d = w2_ref.shape
        y = jnp.dot(jnp.concatenate(hidden, axis=1), w2_ref[...].reshape(ne * de, d),
                    preferred_element_type=jnp.float32)
        y_ref[...] = y.astype(y_ref.dtype)


def _group_moe(tile_group, tile_index, tile_valid, hg, wg, w1, w3, w2):
    ng, cap, d = hg.shape
    t = T_GROUP
    de = w1.shape[-1]
    grid_spec = pltpu.PrefetchScalarGridSpec(
        num_scalar_prefetch=3,
        grid=(tile_group.shape[0],),
        in_specs=[
            pl.BlockSpec((None, t, d), lambda i, tg, tj, tv: (tg[i], tj[i], 0)),
            pl.BlockSpec((None, t, LANES), lambda i, tg, tj, tv: (tg[i], tj[i], 0)),
            pl.BlockSpec((EXPERTS_PER_GROUP, d, de), lambda i, tg, tj, tv: (tg[i], 0, 0)),
            pl.BlockSpec((EXPERTS_PER_GROUP, d, de), lambda i, tg, tj, tv: (tg[i], 0, 0)),
            pl.BlockSpec((EXPERTS_PER_GROUP, de, d), lambda i, tg, tj, tv: (tg[i], 0, 0)),
        ],
        out_specs=pl.BlockSpec((None, t, d), lambda i, tg, tj, tv: (tg[i], tj[i], 0)),
    )
    return pl.pallas_call(
        _group_moe_kernel,
        grid_spec=grid_spec,
        out_shape=jax.ShapeDtypeStruct((ng, cap, d), jnp.bfloat16),
        compiler_params=pltpu.CompilerParams(
            dimension_semantics=("arbitrary",), vmem_limit_bytes=VMEM_LIMIT),
        name="moe_experts",
    )(tile_group, tile_index, tile_valid, hg, wg, w1, w3, w2)


def _combine_kernel(toff_ref, x1_ref, aux_ref, gn_ref, yg_ref, o_ref, ybuf, sems):
    i = pl.program_id(0)
    tm = TM_SORT
    slot = i % 2

    def fetch(step, slot):
        return [pltpu.make_async_copy(
            yg_ref.at[g, pl.ds(pl.multiple_of(toff_ref[step * N_GROUPS + g], ROW_ALIGN), tm)],
            ybuf.at[slot, g], sems.at[slot, g]) for g in range(N_GROUPS)]

    @pl.when(i == 0)
    def _():
        for c in fetch(0, 0):
            c.start()

    @pl.when(i + 1 < pl.num_programs(0))
    def _():
        for c in fetch(i + 1, 1 - slot):
            c.start()

    for c in fetch(i, slot):
        c.wait()

    aux = aux_ref[...]
    lane = lax.broadcasted_iota(jnp.int32, (tm, LANES), 1)
    is_group_lane = jnp.logical_and(lane >= GROUP_LANE0, lane < GROUP_LANE0 + N_GROUPS)
    hot = jnp.where(is_group_lane, aux, 0.0)
    r_io = lax.broadcasted_iota(jnp.int32, (tm, tm), 0)
    c_io = lax.broadcasted_iota(jnp.int32, (tm, tm), 1)
    before = (c_io < r_io).astype(jnp.bfloat16)
    seen = jnp.dot(before, hot.astype(jnp.bfloat16), preferred_element_type=jnp.float32)
    rank = jnp.sum(hot * seen, axis=-1, keepdims=True).astype(jnp.int32)
    y = jnp.zeros((tm, x1_ref.shape[1]), jnp.float32)
    for g in range(N_GROUPS):
        mine = aux[:, GROUP_LANE0 + g:GROUP_LANE0 + g + 1] > 0.5
        sel = jnp.where(jnp.logical_and(rank == c_io, mine), 1.0, 0.0).astype(jnp.bfloat16)
        y = y + jnp.dot(sel, ybuf[slot, g], preferred_element_type=jnp.float32)
    o_ref[...] = _rmsnorm_f32(x1_ref[...] + y, gn_ref[...])


def _combine(toff, x1, aux, gn, yg):
    n, d = x1.shape
    tm = TM_SORT
    grid_spec = pltpu.PrefetchScalarGridSpec(
        num_scalar_prefetch=1,
        grid=(n // tm,),
        in_specs=[
            pl.BlockSpec((tm, d), lambda i, toff: (i, 0)),
            pl.BlockSpec((tm, LANES), lambda i, toff: (i, 0)),
            pl.BlockSpec((1, d), lambda i, toff: (0, 0)),
            pl.BlockSpec(memory_space=pl.ANY),
        ],
        out_specs=pl.BlockSpec((tm, d), lambda i, toff: (i, 0)),
        scratch_shapes=[pltpu.VMEM((2, N_GROUPS, tm, d), jnp.bfloat16),
                        pltpu.SemaphoreType.DMA((2, N_GROUPS))],
    )
    return pl.pallas_call(
        _combine_kernel,
        grid_spec=grid_spec,
        out_shape=jax.ShapeDtypeStruct((n, d), jnp.float32),
        compiler_params=pltpu.CompilerParams(
            dimension_semantics=("arbitrary",), vmem_limit_bytes=VMEM_LIMIT),
        name="moe_combine",
    )(toff, x1, aux, gn, yg)


def _group_tile_table(totals, n):
    tiles = (totals + (TM_SORT + T_GROUP - 1)) // T_GROUP
    ends = jnp.cumsum(tiles)
    steps = jnp.arange(_max_group_tiles(n), dtype=jnp.int32)
    valid = steps < ends[-1]
    step = jnp.minimum(steps, ends[-1] - 1)
    group = jnp.sum((step[:, None] >= ends[None, :]).astype(jnp.int32), axis=1)
    index = step - (ends - tiles)[group]
    return group.astype(jnp.int32), index.astype(jnp.int32), valid.astype(jnp.int32)


def kernel(x, norm_attn, w_in, b_forget, w_o_sb, w_o_fox, w_out, norm_ffn, w_router_group,
           b_router_group, w_router_expert, b_router_expert, w1, w3, w2, norm_final):
    b, s, d = x.shape
    n = b * s
    depth = w_in.shape[0]
    assert depth == 1, "the final norm is fused into the MoE un-sort of a single layer"
    assert s % TM_PROJ == 0 and s % TQ == 0 and n % TM_SORT == 0 and ZERO_TAIL % TM_SORT == 0
    bf16 = jnp.bfloat16
    n_main = 6 * ATTN_WIDTH
    x2 = x.reshape(n, d)
    for l in range(depth):
        w_l = w_in[l]
        w_main = jnp.concatenate([w_l[:, :n_main], w_l[:, n_main + N_HEADS:]], axis=1).astype(bf16)
        wf = jnp.pad(w_l[:, n_main:n_main + N_HEADS], ((0, 0), (0, LANES - N_HEADS)))
        wf_hi = wf.astype(bf16)
        wf_lo = (wf - wf_hi.astype(jnp.float32)).astype(bf16)
        bf = jnp.pad(b_forget[l], (0, LANES - N_HEADS)).reshape(1, LANES)
        wr = jnp.concatenate([w_router_group[l].T, jnp.zeros((8 - N_GROUPS, d), jnp.float32),
                              w_router_expert[l].T], axis=0)
        wr_hi = wr.astype(bf16)
        wr_lo = (wr - wr_hi.astype(jnp.float32)).astype(bf16)
        br = jnp.concatenate([b_router_group[l], jnp.zeros((8 - N_GROUPS,), jnp.float32),
                              b_router_expert[l]]).reshape(ROUTER_ROWS, 1)

        proj, c, ct = _inproj(x2, norm_attn[l].reshape(1, d), w_main, wf_hi, wf_lo, bf, s)
        proj3 = proj.reshape(b, s, proj.shape[1])
        o_sb = _sb_attention(proj3, 0, N_HEAD_BLOCKS, 2 * N_HEAD_BLOCKS)
        o_fx = _fox_attention(proj3, c.reshape(b, s, LANES), ct,
                              3 * N_HEAD_BLOCKS, 4 * N_HEAD_BLOCKS, 5 * N_HEAD_BLOCKS)
        x1, h, ids, aux = _post_attention(
            o_sb.reshape(n, ATTN_WIDTH), o_fx.reshape(n, ATTN_WIDTH), proj, x2,
            w_o_sb[l].astype(bf16), w_o_fox[l].astype(bf16), w_out[l].astype(bf16),
            norm_ffn[l].reshape(1, d), wr_hi, wr_lo, br)
        hg, wg, toff, totals = _dispatch(h, ids, aux)
        tile_group, tile_index, tile_valid = _group_tile_table(totals[:N_GROUPS, 0], n)
        yg = _group_moe(tile_group, tile_index, tile_valid, hg, wg,
                        w1[l].astype(bf16), w3[l].astype(bf16), w2[l].astype(bf16))
        chunk_starts = toff.reshape(n // TM_SORT, 8, LANES)[:, :N_GROUPS, 0].reshape(-1)
        x2 = _combine(chunk_starts, x1, aux, norm_final.reshape(1, d), yg)
    return x2.reshape(b, s, d)
```

```python
import functools
import math
from typing import Callable, NamedTuple

import jax
import jax.numpy as jnp
from jax import lax
from jax.experimental import pallas as pl
from jax.experimental.pallas import tpu as pltpu

HEAD_DIM = 64
N_HEADS = 8
ATTN_WIDTH = N_HEADS * HEAD_DIM
N_GROUPS = 4
EXPERTS_PER_GROUP = 8
N_EXPERTS = N_GROUPS * EXPERTS_PER_GROUP
RMS_EPS = 1e-6
LANES = 128
HEADS_PER_BLOCK = LANES // HEAD_DIM
N_HEAD_BLOCKS = N_HEADS // HEADS_PER_BLOCK
ROUTER_ROWS = 8 + N_EXPERTS
VMEM_LIMIT = 56 * 1024 * 1024

TM_PROJ = 512
TQ = 256
TK = 256
TM_POST = 512

_NT = (((1,), (1,)), ((), ()))
LOG2E = math.log2(math.e)
Q_SCALE = LOG2E / math.sqrt(HEAD_DIM)
SKIP_LOG2 = 160.0


def _split_bf16(v, parts):
    out = []
    r = v
    for i in range(parts):
        p = r.astype(jnp.bfloat16)
        out.append(p)
        if i + 1 < parts:
            r = r - p.astype(jnp.float32)
    return out


def _rmsnorm_f32(x, g):
    ms = jnp.mean(x * x, axis=-1, keepdims=True)
    return x * lax.rsqrt(ms + RMS_EPS) * g


def _inproj_kernel(x_ref, g_ref, w_ref, wfh_ref, wfl_ref, bf_ref,
                   proj_ref, c_ref, ct_ref, carry_ref, *, tiles_per_seq, n_chunks, chunk, q_chunks):
    i = pl.program_id(0)

    @pl.when(i % tiles_per_seq == 0)
    def _():
        carry_ref[...] = jnp.zeros_like(carry_ref)

    y = _rmsnorm_f32(x_ref[...], g_ref[...])
    h_hi, h_lo = _split_bf16(y, 2)
    for c in range(n_chunks):
        sl = slice(c * chunk, (c + 1) * chunk)
        p = jnp.dot(h_hi, w_ref[:, sl], preferred_element_type=jnp.float32)
        if c in q_chunks:
            p = p * Q_SCALE
        proj_ref[:, sl] = p.astype(proj_ref.dtype)

    f = (jnp.dot(h_hi, wfh_ref[...], preferred_element_type=jnp.float32)
         + jnp.dot(h_lo, wfh_ref[...], preferred_element_type=jnp.float32)
         + jnp.dot(h_hi, wfl_ref[...], preferred_element_type=jnp.float32))
    f = f + bf_ref[...]
    logf = (jnp.minimum(f, 0.0) - jnp.log1p(jnp.exp(-jnp.abs(f)))) * LOG2E

    tm = logf.shape[0]
    row = lax.broadcasted_iota(jnp.int32, (tm, tm), 0)
    col = lax.broadcasted_iota(jnp.int32, (tm, tm), 1)
    tri = (col <= row).astype(jnp.bfloat16)
    cum = carry_ref[...]
    for p in _split_bf16(logf, 3):
        cum = cum + jnp.dot(tri, p, preferred_element_type=jnp.float32)
    c_ref[...] = cum
    ct_ref[...] = cum.T[:N_HEADS, :]
    carry_ref[...] = cum[tm - 1:tm, :]


def _inproj(x2, g, w_main, wf_hi, wf_lo, bf, seq):
    n, d = x2.shape
    cols = w_main.shape[1]
    tm = TM_PROJ
    chunk = ATTN_WIDTH
    kern = functools.partial(_inproj_kernel, tiles_per_seq=seq // tm,
                             n_chunks=cols // chunk, chunk=chunk, q_chunks=(0, 3))
    const = dict(pipeline_mode=pl.Buffered(1))
    return pl.pallas_call(
        kern,
        grid=(n // tm,),
        in_specs=[
            pl.BlockSpec((tm, d), lambda i: (i, 0)),
            pl.BlockSpec((1, d), lambda i: (0, 0), **const),
            pl.BlockSpec((d, cols), lambda i: (0, 0), **const),
            pl.BlockSpec((d, LANES), lambda i: (0, 0), **const),
            pl.BlockSpec((d, LANES), lambda i: (0, 0), **const),
            pl.BlockSpec((1, LANES), lambda i: (0, 0), **const),
        ],
        out_specs=[
            pl.BlockSpec((tm, cols), lambda i: (i, 0)),
            pl.BlockSpec((tm, LANES), lambda i: (i, 0)),
            pl.BlockSpec((N_HEADS, tm), lambda i: (0, i)),
        ],
        out_shape=[
            jax.ShapeDtypeStruct((n, cols), jnp.bfloat16),
            jax.ShapeDtypeStruct((n, LANES), jnp.float32),
            jax.ShapeDtypeStruct((N_HEADS, n), jnp.float32),
        ],
        scratch_shapes=[pltpu.VMEM((1, LANES), jnp.float32)],
        compiler_params=pltpu.CompilerParams(
            dimension_semantics=("arbitrary",), vmem_limit_bytes=VMEM_LIMIT),
        name="inproj",
    )(x2, g, w_main, wf_hi, wf_lo, bf)


def _stack_heads(q):
    lane = lax.broadcasted_iota(jnp.int32, q.shape, 1)
    return jnp.concatenate(
        [jnp.where((lane // HEAD_DIM) == h, q, jnp.zeros_like(q)) for h in range(HEADS_PER_BLOCK)], axis=0)


def _unstack_heads(acc):
    tq = acc.shape[0] // HEADS_PER_BLOCK
    lane = lax.broadcasted_iota(jnp.int32, (tq, LANES), 1)
    out = acc[:tq]
    for h in range(1, HEADS_PER_BLOCK):
        out = jnp.where((lane // HEAD_DIM) == h, acc[h * tq:(h + 1) * tq], out)
    return out


def _lane_tile(x, width):
    return jnp.concatenate([x] * (width // LANES), axis=1)


class _Chain(NamedTuple):
    first: Callable
    overlap: Callable
    last: Callable
    live: Callable


def _pipelined_tiles(qi, chains):
    for ch in chains:
        ch.first(qi, 0)

    def cond(state):
        p, gos = state[0], state[1:]
        return jnp.logical_and(p < qi // 2, functools.reduce(jnp.logical_or, [g != 0 for g in gos]))

    def pair(state):
        p, gos = state[0], state[1:]
        kj = qi - 2 * p
        for mask in range(1, 2 ** len(chains)):
            active = [c for i, c in enumerate(chains) if mask >> i & 1]
            preds = [(gos[i] != 0) if mask >> i & 1 else (gos[i] == 0) for i in range(len(chains))]

            @pl.when(functools.reduce(jnp.logical_and, preds))
            def _():
                for slot in (0, 1):
                    for ch in active:
                        ch.overlap(kj - slot, slot)

        new = [jnp.where(jnp.logical_and(g != 0, ch.live(kj - 1)), 1, 0).astype(jnp.int32)
               for g, ch in zip(gos, chains)]
        return (p + 1, *new)

    state = lax.while_loop(cond, pair, (jnp.int32(0),) + (jnp.int32(1),) * len(chains))

    for go, ch in zip(state[1:], chains):
        @pl.when(jnp.logical_and(go != 0, qi % 2 == 1))
        def _():
            ch.overlap(1, 0)
            ch.last(0, 1)

        @pl.when(jnp.logical_and(go != 0, qi % 2 == 0))
        def _():
            ch.last(0, 0)


def _sb_chain(q_ref, k_ref, v_ref, acc_ref, carry_ref, lb_ref, lk_ref, rs_ref):
    qcat = _stack_heads(q_ref[...])
    m = HEADS_PER_BLOCK * TQ
    urow = lax.broadcasted_iota(jnp.int32, (TK, TK), 0)
    ucol = lax.broadcasted_iota(jnp.int32, (TK, TK), 1)
    upper = (ucol < urow).astype(jnp.bfloat16)

    acc_ref[...] = jnp.zeros_like(acc_ref)
    carry_ref[...] = jnp.zeros_like(carry_ref)

    def score(kj, slot, diag):
        k = k_ref[pl.ds(kj * TK, TK), :]
        z = lax.dot_general(qcat, k, _NT, preferred_element_type=jnp.float32)
        t = jnp.log2(1.0 + jnp.exp2(-jnp.abs(z)))
        log_beta = jnp.minimum(z, 0.0) - t
        log_keep = log_beta - z
        if diag:
            row = lax.broadcasted_iota(jnp.int32, (m, TK), 0)
            col = lax.broadcasted_iota(jnp.int32, (m, TK), 1)
            below = col < (row % TQ)
            log_keep = jnp.where(below, log_keep, 0.0)
            log_beta = jnp.where(below, log_beta, -jnp.inf)
        lb_ref[slot] = log_beta
        lk_ref[slot] = log_keep.astype(jnp.bfloat16)
        rs_ref[slot] = jnp.broadcast_to(jnp.sum(log_keep, axis=-1, keepdims=True), (m, LANES))

    def finish(kj, slot):
        v = v_ref[pl.ds(kj * TK, TK), :]
        rest = jnp.dot(lk_ref[slot], upper, preferred_element_type=jnp.float32)
        carry = carry_ref[...]
        a = jnp.exp2(lb_ref[slot] + rest + _lane_tile(carry, TK))
        acc_ref[...] += jnp.dot(a.astype(v.dtype), v, preferred_element_type=jnp.float32)
        carry_ref[...] = carry + rs_ref[slot]

    def overlap(kj, slot):
        score(kj - 1, 1 - slot, False)
        finish(kj, slot)

    def live(kj):
        return jnp.max(carry_ref[...]) > -SKIP_LOG2

    return _Chain(lambda kj, slot: score(kj, slot, True), overlap, finish, live)


def _fox_chain(hb, qi, q_ref, k_ref, v_ref, c_ref, ct_ref, acc_ref, m_ref, l_ref, z_ref, rm_ref,
               kn_ref):
    qcat = _stack_heads(q_ref[...])
    lane = lax.broadcasted_iota(jnp.int32, (TQ, LANES), 1)
    cblk = c_ref[...]
    cq = jnp.concatenate(
        [jnp.broadcast_to(
            jnp.sum(jnp.where(lane == hb * HEADS_PER_BLOCK + h, cblk, 0.0), axis=-1, keepdims=True),
            (TQ, TK)) for h in range(HEADS_PER_BLOCK)], axis=0)
    m = HEADS_PER_BLOCK * TQ
    ones = jnp.ones((TK, LANES), jnp.bfloat16)

    @pl.when(qi == 0)
    def _():
        kf = k_ref[...].astype(jnp.float32)
        ksq = kf * kf
        klane = lax.broadcasted_iota(jnp.int32, ksq.shape, 1)
        for h in range(HEADS_PER_BLOCK):
            n2 = jnp.sum(jnp.where((klane // HEAD_DIM) == h, ksq, 0.0), axis=-1, keepdims=True)
            kn_ref[h * TQ:(h + 1) * TQ, :] = jnp.broadcast_to(
                jnp.sqrt(jnp.max(n2, axis=0, keepdims=True)), (TQ, LANES))

    qf = qcat.astype(jnp.float32)
    qn = jnp.broadcast_to(jnp.sqrt(jnp.sum(qf * qf, axis=-1, keepdims=True)), (m, LANES))

    acc_ref[...] = jnp.zeros_like(acc_ref)
    l_ref[...] = jnp.zeros_like(l_ref)
    m_ref[...] = jnp.full_like(m_ref, -jnp.inf)

    def score(kj, slot, diag):
        k = k_ref[pl.ds(kj * TK, TK), :]
        ck = jnp.concatenate(
            [jnp.broadcast_to(ct_ref[pl.ds(hb * HEADS_PER_BLOCK + h, 1), pl.ds(kj * TK, TK)], (TQ, TK))
             for h in range(HEADS_PER_BLOCK)], axis=0)
        z = lax.dot_general(qcat, k, _NT, preferred_element_type=jnp.float32)
        z = (z + cq) - ck
        if diag:
            row = lax.broadcasted_iota(jnp.int32, (m, TK), 0)
            col = lax.broadcasted_iota(jnp.int32, (m, TK), 1)
            z = jnp.where(col <= (row % TQ), z, -jnp.inf)
        z_ref[slot] = z
        rm_ref[slot] = jnp.broadcast_to(jnp.max(z, axis=-1, keepdims=True), (m, LANES))

    def finish(kj, slot):
        v = v_ref[pl.ds(kj * TK, TK), :]
        m_old = m_ref[...]
        m_new = jnp.maximum(m_old, rm_ref[slot])
        alpha = jnp.exp2(m_old - m_new)
        p = jnp.exp2(z_ref[slot] - _lane_tile(m_new, TK))
        pv = jnp.dot(p.astype(v.dtype), jnp.concatenate([v, ones], axis=1),
                     preferred_element_type=jnp.float32)
        acc_ref[...] = alpha * acc_ref[...] + pv[:, :LANES]
        l_ref[...] = alpha * l_ref[...] + pv[:, LANES:]
        m_ref[...] = m_new

    def overlap(kj, slot):
        score(kj - 1, 1 - slot, False)
        finish(kj, slot)

    def live(kj):
        cb = jnp.concatenate(
            [jnp.broadcast_to(
                jnp.max(ct_ref[pl.ds(hb * HEADS_PER_BLOCK + h, 1), pl.ds(kj * TK, TK)],
                        axis=-1, keepdims=True), (TQ, LANES)) for h in range(HEADS_PER_BLOCK)], axis=0)
        bound = qn * kn_ref[...] + cq[:, :LANES] - cb - m_ref[...]
        return jnp.max(bound) > -SKIP_LOG2

    return _Chain(lambda kj, slot: score(kj, slot, True), overlap, finish, live)


def _attn_kernel(qs_ref, ks_ref, vs_ref, qf_ref, kf_ref, vf_ref, c_ref, ct_ref, osb_ref, ofx_ref,
                 sb_acc, sb_carry, sb_lb, sb_lk, sb_rs, fx_acc, fx_m, fx_l, fx_z, fx_rm, fx_kn):
    hb = pl.program_id(1)
    qi = pl.program_id(2)
    sb = _sb_chain(qs_ref, ks_ref, vs_ref, sb_acc, sb_carry, sb_lb, sb_lk, sb_rs)
    fx = _fox_chain(hb, qi, qf_ref, kf_ref, vf_ref, c_ref, ct_ref, fx_acc, fx_m, fx_l, fx_z, fx_rm, fx_kn)
    _pipelined_tiles(qi, [sb, fx])
    osb_ref[...] = _unstack_heads(sb_acc[...]).astype(osb_ref.dtype)
    ofx_ref[...] = _unstack_heads(fx_acc[...] / fx_l[...]).astype(ofx_ref.dtype)


def _attention(proj3, c3, ct):
    b, s, _ = proj3.shape
    m = HEADS_PER_BLOCK * TQ

    def q_spec(blk):
        return pl.BlockSpec((None, TQ, LANES), lambda bi, hb, qi: (bi, qi, blk * N_HEAD_BLOCKS + hb))

    def kv_spec(blk):
        return pl.BlockSpec((None, s, LANES), lambda bi, hb, qi: (bi, 0, blk * N_HEAD_BLOCKS + hb))

    out_spec = pl.BlockSpec((None, TQ, LANES), lambda bi, hb, qi: (bi, qi, hb))
    out_shape = jax.ShapeDtypeStruct((b, s, ATTN_WIDTH), jnp.bfloat16)
    f32 = jnp.float32
    return pl.pallas_call(
        _attn_kernel,
        grid=(b, N_HEAD_BLOCKS, s // TQ),
        in_specs=[q_spec(0), kv_spec(1), kv_spec(2), q_spec(3), kv_spec(4), kv_spec(5),
                  pl.BlockSpec((None, TQ, LANES), lambda bi, hb, qi: (bi, qi, 0)),
                  pl.BlockSpec((N_HEADS, s), lambda bi, hb, qi: (0, bi))],
        out_specs=[out_spec, out_spec],
        out_shape=[out_shape, out_shape],
        scratch_shapes=[pltpu.VMEM((m, LANES), f32),
                        pltpu.VMEM((m, LANES), f32),
                        pltpu.VMEM((2, m, TK), f32),
                        pltpu.VMEM((2, m, TK), jnp.bfloat16),
                        pltpu.VMEM((2, m, LANES), f32),
                        pltpu.VMEM((m, LANES), f32),
                        pltpu.VMEM((m, LANES), f32),
                        pltpu.VMEM((m, LANES), f32),
                        pltpu.VMEM((2, m, TK), f32),
                        pltpu.VMEM((2, m, LANES), f32),
                        pltpu.VMEM((m, LANES), f32)],
        compiler_params=pltpu.CompilerParams(
            dimension_semantics=("arbitrary", "arbitrary", "arbitrary"),
            vmem_limit_bytes=VMEM_LIMIT),
        name="attn",
    )(proj3, proj3, proj3, proj3, proj3, proj3, c3, ct)


def _post_kernel(osb_ref, ofx_ref, gsb_ref, gfx_ref, x_ref, wosb_ref, wofx_ref, wout_ref,
                 gn_ref, wrh_ref, wrl_ref, br_ref,
                 x1_ref, h_ref, ids_ref, aux_ref):
    y_sb = jnp.dot(osb_ref[...], wosb_ref[...], preferred_element_type=jnp.float32)
    y_fx = jnp.dot(ofx_ref[...], wofx_ref[...], preferred_element_type=jnp.float32)
    mixed = (jax.nn.sigmoid(gsb_ref[...].astype(jnp.float32)) * y_sb
             + jax.nn.sigmoid(gfx_ref[...].astype(jnp.float32)) * y_fx)
    x1 = x_ref[...] + jnp.dot(mixed.astype(jnp.bfloat16), wout_ref[...],
                              preferred_element_type=jnp.float32)
    x1_ref[...] = x1
    y = _rmsnorm_f32(x1, gn_ref[...])
    h_hi, h_lo = _split_bf16(y, 2)
    h_ref[...] = h_hi

    lg = (lax.dot_general(wrh_ref[...], h_hi, _NT, preferred_element_type=jnp.float32)
          + lax.dot_general(wrh_ref[...], h_lo, _NT, preferred_element_type=jnp.float32)
          + lax.dot_general(wrl_ref[...], h_hi, _NT, preferred_element_type=jnp.float32))
    lg = lg + br_ref[...]
    tm = lg.shape[1]
    sub = lax.broadcasted_iota(jnp.int32, (8, tm), 0)
    neg = -jnp.inf

    gl = jnp.where(sub < N_GROUPS, lg[0:8], neg)
    gm = jnp.max(gl, axis=0, keepdims=True)
    g_w = 1.0 / jnp.sum(jnp.exp(gl - gm), axis=0, keepdims=True)
    g_idx = jnp.min(jnp.where(gl == gm, sub, 8), axis=0, keepdims=True)

    e_sel = lg[8:16]
    for g in range(1, N_GROUPS):
        e_sel = jnp.where(g_idx == g, lg[8 + 8 * g:16 + 8 * g], e_sel)
    m1 = jnp.max(e_sel, axis=0, keepdims=True)
    i1 = jnp.min(jnp.where(e_sel == m1, sub, 8), axis=0, keepdims=True)
    e_rest = jnp.where(sub == i1, neg, e_sel)
    m2 = jnp.max(e_rest, axis=0, keepdims=True)
    i2 = jnp.min(jnp.where(e_rest == m2, sub, 8), axis=0, keepdims=True)
    p2 = jnp.exp(m2 - m1)
    w1 = g_w / (1.0 + p2)
    w2 = g_w * p2 / (1.0 + p2)
    base = g_idx * EXPERTS_PER_GROUP
    ids_ref[...] = jnp.where(sub == 0, base + i1, jnp.where(sub == 1, base + i2, 0))
    dense_w = jnp.where(sub == i1, w1, jnp.where(sub == i2, w2, 0.0))
    group_hot = jnp.where(sub == g_idx, 1.0, 0.0)
    record = jnp.concatenate(
        [dense_w, group_hot, jnp.zeros((LANES - 16, tm), jnp.float32)], axis=0)
    aux_ref[...] = record.T


def _post_attention(o_sb, o_fx, proj, x2, wosb, wofx, wout, gn, wr_hi, wr_lo, br):
    n, d = x2.shape
    tm = TM_POST
    gate_blk = (3 * ATTN_WIDTH * 2) // d
    const = dict(pipeline_mode=pl.Buffered(1))
    return pl.pallas_call(
        _post_kernel,
        grid=(n // tm,),
        in_specs=[
            pl.BlockSpec((tm, ATTN_WIDTH), lambda i: (i, 0)),
            pl.BlockSpec((tm, ATTN_WIDTH), lambda i: (i, 0)),
            pl.BlockSpec((tm, d), lambda i: (i, gate_blk)),
            pl.BlockSpec((tm, d), lambda i: (i, gate_blk + 1)),
            pl.BlockSpec((tm, d), lambda i: (i, 0)),
            pl.BlockSpec((ATTN_WIDTH, d), lambda i: (0, 0), **const),
            pl.BlockSpec((ATTN_WIDTH, d), lambda i: (0, 0), **const),
            pl.BlockSpec((d, d), lambda i: (0, 0), **const),
            pl.BlockSpec((1, d), lambda i: (0, 0), **const),
            pl.BlockSpec((ROUTER_ROWS, d), lambda i: (0, 0), **const),
            pl.BlockSpec((ROUTER_ROWS, d), lambda i: (0, 0), **const),
            pl.BlockSpec((ROUTER_ROWS, 1), lambda i: (0, 0), **const),
        ],
        out_specs=[
            pl.BlockSpec((tm, d), lambda i: (i, 0)),
            pl.BlockSpec((tm, d), lambda i: (i, 0)),
            pl.BlockSpec((8, tm), lambda i: (0, i)),
            pl.BlockSpec((tm, LANES), lambda i: (i, 0)),
        ],
        out_shape=[
            jax.ShapeDtypeStruct((n, d), jnp.float32),
            jax.ShapeDtypeStruct((n, d), jnp.bfloat16),
            jax.ShapeDtypeStruct((8, n), jnp.int32),
            jax.ShapeDtypeStruct((n, LANES), jnp.float32),
        ],
        compiler_params=pltpu.CompilerParams(
            dimension_semantics=("arbitrary",), vmem_limit_bytes=VMEM_LIMIT),
        name="post_attn",
    )(o_sb, o_fx, proj, proj, x2, wosb, wofx, wout, gn, wr_hi, wr_lo, br)


TM_SORT = 256
T_GROUP = 256
ROW_ALIGN = 16
ZERO_TAIL = TM_SORT + T_GROUP
SORT_ROWS = TM_SORT + N_GROUPS * ROW_ALIGN
GROUP_LANE0 = EXPERTS_PER_GROUP


def _sort_capacity(n):
    cap = n + ROW_ALIGN * (n // TM_SORT) + ZERO_TAIL
    return -(-cap // T_GROUP) * T_GROUP


def _max_group_tiles(n):
    padded = n + (ROW_ALIGN - 1) * N_GROUPS * (n // TM_SORT)
    return padded // T_GROUP + N_GROUPS * (ZERO_TAIL // T_GROUP + 1)


def _dispatch_kernel(h_ref, ids_ref, aux_ref, hg_ref, wg_ref, toff_ref, tot_ref,
                     hbuf, wbuf, sems, off_ref, prev_ref, src_ref):
    i = pl.program_id(0)
    tm = TM_SORT
    rows = SORT_ROWS
    slot = i % 2

    def chunk_copies(g, slot, src, off):
        src = src if isinstance(src, int) else pl.multiple_of(src, ROW_ALIGN)
        off = pl.multiple_of(off, ROW_ALIGN)
        return (pltpu.make_async_copy(hbuf.at[slot, pl.ds(src, tm)], hg_ref.at[g, pl.ds(off, tm)],
                                      sems.at[0, g]),
                pltpu.make_async_copy(wbuf.at[slot, pl.ds(src, tm)], wg_ref.at[g, pl.ds(off, tm)],
                                      sems.at[1, g]))

    @pl.when(i == 0)
    def _():
        for g in range(N_GROUPS):
            off_ref[g] = 0
        hbuf[...] = jnp.zeros_like(hbuf)
        wbuf[...] = jnp.zeros_like(wbuf)

    gid = ids_ref[0:1, :] // EXPERTS_PER_GROUP
    sub = lax.broadcasted_iota(jnp.int32, (8, tm), 0)
    hot = sub == gid
    r_io = lax.broadcasted_iota(jnp.int32, (tm, tm), 0)
    c_io = lax.broadcasted_iota(jnp.int32, (tm, tm), 1)
    before = (r_io < c_io).astype(jnp.bfloat16)
    seen = jnp.dot(jnp.where(hot, 1.0, 0.0).astype(jnp.bfloat16), before,
                   preferred_element_type=jnp.float32)
    pos = jnp.sum(jnp.where(hot, seen, 0.0), axis=0, keepdims=True).astype(jnp.int32)

    sub_out = lax.broadcasted_iota(jnp.int32, (8, LANES), 0)
    offs = jnp.zeros((8, LANES), jnp.int32)
    start = jnp.int32(0)
    starts, dests = [], []
    for g in range(N_GROUPS):
        count = jnp.sum(jnp.where(gid == g, 1.0, 0.0)).astype(jnp.int32)
        pos = pos + jnp.where(gid == g, start, 0)
        off = off_ref[g]
        offs = jnp.where(sub_out == g, off, offs)
        starts.append(start)
        dests.append(off)
        padded = (count + (ROW_ALIGN - 1)) // ROW_ALIGN * ROW_ALIGN
        off_ref[g] = off + padded
        start = start + padded
    toff_ref[...] = offs

    s_io = lax.broadcasted_iota(jnp.int32, (rows, tm), 0)
    sel = jnp.where(pos == s_io, 1.0, 0.0).astype(jnp.bfloat16)
    hbuf[slot, 0:rows] = jnp.dot(sel, h_ref[...], preferred_element_type=jnp.float32).astype(hbuf.dtype)
    aux3 = jnp.concatenate(_split_bf16(aux_ref[...], 3), axis=1)
    w3 = jnp.dot(sel, aux3, preferred_element_type=jnp.float32)
    wbuf[slot, 0:rows] = w3[:, :LANES] + w3[:, LANES:2 * LANES] + w3[:, 2 * LANES:]

    @pl.when(i > 0)
    def _():
        for g in range(N_GROUPS):
            for c in chunk_copies(g, 1 - slot, src_ref[g], prev_ref[g]):
                c.wait()

    for g in range(N_GROUPS):
        src_ref[g] = starts[g]
        prev_ref[g] = dests[g]
        for c in chunk_copies(g, slot, starts[g], dests[g]):
            c.start()

    total = jnp.zeros((8, LANES), jnp.int32)
    for g in range(N_GROUPS):
        total = jnp.where(sub_out == g, off_ref[g], total)
    tot_ref[...] = total

    @pl.when(i == pl.num_programs(0) - 1)
    def _():
        for g in range(N_GROUPS):
            for c in chunk_copies(g, slot, src_ref[g], prev_ref[g]):
                c.wait()
        hbuf[...] = jnp.zeros_like(hbuf)
        wbuf[...] = jnp.zeros_like(wbuf)
        for g in range(N_GROUPS):
            for z in range(ZERO_TAIL // tm):
                for c in chunk_copies(g, 0, 0, off_ref[g] + z * tm):
                    c.start()
                for c in chunk_copies(g, 0, 0, off_ref[g] + z * tm):
                    c.wait()


def _dispatch(h, ids, aux):
    n, d = h.shape
    tm = TM_SORT
    cap = _sort_capacity(n)
    steps = n // tm
    return pl.pallas_call(
        _dispatch_kernel,
        grid=(steps,),
        in_specs=[
            pl.BlockSpec((tm, d), lambda i: (i, 0)),
            pl.BlockSpec((8, tm), lambda i: (0, i)),
            pl.BlockSpec((tm, LANES), lambda i: (i, 0)),
        ],
        out_specs=[
            pl.BlockSpec(memory_space=pl.ANY),
            pl.BlockSpec(memory_space=pl.ANY),
            pl.BlockSpec((8, LANES), lambda i: (i, 0)),
            pl.BlockSpec((8, LANES), lambda i: (0, 0)),
        ],
        out_shape=[
            jax.ShapeDtypeStruct((N_GROUPS, cap, d), jnp.bfloat16),
            jax.ShapeDtypeStruct((N_GROUPS, cap, LANES), jnp.float32),
            jax.ShapeDtypeStruct((steps * 8, LANES), jnp.int32),
            jax.ShapeDtypeStruct((8, LANES), jnp.int32),
        ],
        scratch_shapes=[pltpu.VMEM((2, SORT_ROWS + tm, d), jnp.bfloat16),
                        pltpu.VMEM((2, SORT_ROWS + tm, LANES), jnp.float32),
                        pltpu.SemaphoreType.DMA((2, N_GROUPS)),
                        pltpu.SMEM((N_GROUPS,), jnp.int32),
                        pltpu.SMEM((N_GROUPS,), jnp.int32),
                        pltpu.SMEM((N_GROUPS,), jnp.int32)],
        compiler_params=pltpu.CompilerParams(
            dimension_semantics=("arbitrary",), vmem_limit_bytes=VMEM_LIMIT),
        name="moe_sort",
    )(h, ids, aux)


def _group_moe_kernel(tg_ref, tj_ref, tv_ref, h_ref, w_ref, w1_ref, w3_ref, w2_ref, y_ref):
    i = pl.program_id(0)

    @pl.when(tv_ref[i] != 0)
    def _():
        h = h_ref[...]
        w = w_ref[...]
        hidden = []
        for e in range(EXPERTS_PER_GROUP):
            a = jnp.dot(h, w1_ref[e], preferred_element_type=jnp.float32)
            b = jnp.dot(h, w3_ref[e], preferred_element_type=jnp.float32)
            hidden.append(((a * jax.nn.sigmoid(a)) * b * w[:, e:e + 1]).astype(jnp.bfloat16))
        ne, de, d = w2_ref.shape
        y = jnp.dot(jnp.concatenate(hidden, axis=1), w2_ref[...].reshape(ne * de, d),
                    preferred_element_type=jnp.float32)
        y_ref[...] = y.astype(y_ref.dtype)


def _group_moe(tile_group, tile_index, tile_valid, hg, wg, w1, w3, w2):
    ng, cap, d = hg.shape
    t = T_GROUP
    de = w1.shape[-1]
    grid_spec = pltpu.PrefetchScalarGridSpec(
        num_scalar_prefetch=3,
        grid=(tile_group.shape[0],),
        in_specs=[
            pl.BlockSpec((None, t, d), lambda i, tg, tj, tv: (tg[i], tj[i], 0)),
            pl.BlockSpec((None, t, LANES), lambda i, tg, tj, tv: (tg[i], tj[i], 0)),
            pl.BlockSpec((EXPERTS_PER_GROUP, d, de), lambda i, tg, tj, tv: (tg[i], 0, 0)),
            pl.BlockSpec((EXPERTS_PER_GROUP, d, de), lambda i, tg, tj, tv: (tg[i], 0, 0)),
            pl.BlockSpec((EXPERTS_PER_GROUP, de, d), lambda i, tg, tj, tv: (tg[i], 0, 0)),
        ],
        out_specs=pl.BlockSpec((None, t, d), lambda i, tg, tj, tv: (tg[i], tj[i], 0)),
    )
    return pl.pallas_call(
        _group_moe_kernel,
        grid_spec=grid_spec,
        out_shape=jax.ShapeDtypeStruct((ng, cap, d), jnp.bfloat16),
        compiler_params=pltpu.CompilerParams(
            dimension_semantics=("arbitrary",), vmem_limit_bytes=VMEM_LIMIT),
        name="moe_experts",
    )(tile_group, tile_index, tile_valid, hg, wg, w1, w3, w2)


def _combine_kernel(toff_ref, x1_ref, aux_ref, gn_ref, yg_ref, o_ref, ybuf, sems):
    i = pl.program_id(0)
    tm = TM_SORT
    slot = i % 2

    def fetch(step, slot):
        return [pltpu.make_async_copy(
            yg_ref.at[g, pl.ds(pl.multiple_of(toff_ref[step * N_GROUPS + g], ROW_ALIGN), tm)],
            ybuf.at[slot, g], sems.at[slot, g]) for g in range(N_GROUPS)]

    @pl.when(i == 0)
    def _():
        for c in fetch(0, 0):
            c.start()

    @pl.when(i + 1 < pl.num_programs(0))
    def _():
        for c in fetch(i + 1, 1 - slot):
            c.start()

    for c in fetch(i, slot):
        c.wait()

    aux = aux_ref[...]
    lane = lax.broadcasted_iota(jnp.int32, (tm, LANES), 1)
    is_group_lane = jnp.logical_and(lane >= GROUP_LANE0, lane < GROUP_LANE0 + N_GROUPS)
    hot = jnp.where(is_group_lane, aux, 0.0)
    r_io = lax.broadcasted_iota(jnp.int32, (tm, tm), 0)
    c_io = lax.broadcasted_iota(jnp.int32, (tm, tm), 1)
    before = (c_io < r_io).astype(jnp.bfloat16)
    seen = jnp.dot(before, hot.astype(jnp.bfloat16), preferred_element_type=jnp.float32)
    rank = jnp.sum(hot * seen, axis=-1, keepdims=True).astype(jnp.int32)
    y = jnp.zeros((tm, x1_ref.shape[1]), jnp.float32)
    for g in range(N_GROUPS):
        mine = aux[:, GROUP_LANE0 + g:GROUP_LANE0 + g + 1] > 0.5
        sel = jnp.where(jnp.logical_and(rank == c_io, mine), 1.0, 0.0).astype(jnp.bfloat16)
        y = y + jnp.dot(sel, ybuf[slot, g], preferred_element_type=jnp.float32)
    o_ref[...] = _rmsnorm_f32(x1_ref[...] + y, gn_ref[...])


def _combine(toff, x1, aux, gn, yg):
    n, d = x1.shape
    tm = TM_SORT
    grid_spec = pltpu.PrefetchScalarGridSpec(
        num_scalar_prefetch=1,
        grid=(n // tm,),
        in_specs=[
            pl.BlockSpec((tm, d), lambda i, toff: (i, 0)),
            pl.BlockSpec((tm, LANES), lambda i, toff: (i, 0)),
            pl.BlockSpec((1, d), lambda i, toff: (0, 0)),
            pl.BlockSpec(memory_space=pl.ANY),
        ],
        out_specs=pl.BlockSpec((tm, d), lambda i, toff: (i, 0)),
        scratch_shapes=[pltpu.VMEM((2, N_GROUPS, tm, d), jnp.bfloat16),
                        pltpu.SemaphoreType.DMA((2, N_GROUPS))],
    )
    return pl.pallas_call(
        _combine_kernel,
        grid_spec=grid_spec,
        out_shape=jax.ShapeDtypeStruct((n, d), jnp.float32),
        compiler_params=pltpu.CompilerParams(
            dimension_semantics=("arbitrary",), vmem_limit_bytes=VMEM_LIMIT),
        name="moe_combine",
    )(toff, x1, aux, gn, yg)


def _group_tile_table(totals, n):
    tiles = (totals + (TM_SORT + T_GROUP - 1)) // T_GROUP
    ends = jnp.cumsum(tiles)
    steps = jnp.arange(_max_group_tiles(n), dtype=jnp.int32)
    valid = steps < ends[-1]
    step = jnp.minimum(steps, ends[-1] - 1)
    group = jnp.sum((step[:, None] >= ends[None, :]).astype(jnp.int32), axis=1)
    index = step - (ends - tiles)[group]
    return group.astype(jnp.int32), index.astype(jnp.int32), valid.astype(jnp.int32)


def kernel(x, norm_attn, w_in, b_forget, w_o_sb, w_o_fox, w_out, norm_ffn, w_router_group,
           b_router_group, w_router_expert, b_router_expert, w1, w3, w2, norm_final):
    b, s, d = x.shape
    n = b * s
    depth = w_in.shape[0]
    assert depth == 1, "the final norm is fused into the MoE un-sort of a single layer"
    assert s % TM_PROJ == 0 and s % TQ == 0 and n % TM_SORT == 0 and ZERO_TAIL % TM_SORT == 0
    bf16 = jnp.bfloat16
    n_main = 6 * ATTN_WIDTH
    x2 = x.reshape(n, d)
    for l in range(depth):
        w_l = w_in[l]
        w_main = jnp.concatenate([w_l[:, :n_main], w_l[:, n_main + N_HEADS:]], axis=1).astype(bf16)
        wf = jnp.pad(w_l[:, n_main:n_main + N_HEADS], ((0, 0), (0, LANES - N_HEADS)))
        wf_hi = wf.astype(bf16)
        wf_lo = (wf - wf_hi.astype(jnp.float32)).astype(bf16)
        bf = jnp.pad(b_forget[l], (0, LANES - N_HEADS)).reshape(1, LANES)
        wr = jnp.concatenate([w_router_group[l].T, jnp.zeros((8 - N_GROUPS, d), jnp.float32),
                              w_router_expert[l].T], axis=0)
        wr_hi = wr.astype(bf16)
        wr_lo = (wr - wr_hi.astype(jnp.float32)).astype(bf16)
        br = jnp.concatenate([b_router_group[l], jnp.zeros((8 - N_GROUPS,), jnp.float32),
                              b_router_expert[l]]).reshape(ROUTER_ROWS, 1)

        proj, c, ct = _inproj(x2, norm_attn[l].reshape(1, d), w_main, wf_hi, wf_lo, bf, s)
        proj3 = proj.reshape(b, s, proj.shape[1])
        o_sb, o_fx = _attention(proj3, c.reshape(b, s, LANES), ct)
        x1, h, ids, aux = _post_attention(
            o_sb.reshape(n, ATTN_WIDTH), o_fx.reshape(n, ATTN_WIDTH), proj, x2,
            w_o_sb[l].astype(bf16), w_o_fox[l].astype(bf16), w_out[l].astype(bf16),
            norm_ffn[l].reshape(1, d), wr_hi, wr_lo, br)
        hg, wg, toff, totals = _dispatch(h, ids, aux)
        tile_group, tile_index, tile_valid = _group_tile_table(totals[:N_GROUPS, 0], n)
        yg = _group_moe(tile_group, tile_index, tile_valid, hg, wg,
                        w1[l].astype(bf16), w3[l].astype(bf16), w2[l].astype(bf16))
        chunk_starts = toff.reshape(n // TM_SORT, 8, LANES)[:, :N_GROUPS, 0].reshape(-1)
        x2 = _combine(chunk_starts, x1, aux, norm_final.reshape(1, d), yg)
    return x2.reshape(b, s, d)
```

```python
import functools
import math
from typing import Callable, NamedTuple

import jax
import jax.numpy as jnp
from jax import lax
from jax.experimental import pallas as pl
from jax.experimental.pallas import tpu as pltpu

HEAD_DIM = 64
N_HEADS = 8
ATTN_WIDTH = N_HEADS * HEAD_DIM
N_GROUPS = 4
EXPERTS_PER_GROUP = 8
N_EXPERTS = N_GROUPS * EXPERTS_PER_GROUP
RMS_EPS = 1e-6
LANES = 128
HEADS_PER_BLOCK = LANES // HEAD_DIM
N_HEAD_BLOCKS = N_HEADS // HEADS_PER_BLOCK
ROUTER_ROWS = 8 + N_EXPERTS
VMEM_LIMIT = 56 * 1024 * 1024

TM_PROJ = 512
TQ = 256
TK = 256
TM_POST = 512

_NT = (((1,), (1,)), ((), ()))
LOG2E = math.log2(math.e)
Q_SCALE = LOG2E / math.sqrt(HEAD_DIM)
SKIP_LOG2 = 160.0


def _split_bf16(v, parts):
    out = []
    r = v
    for i in range(parts):
        p = r.astype(jnp.bfloat16)
        out.append(p)
        if i + 1 < parts:
            r = r - p.astype(jnp.float32)
    return out


def _rmsnorm_f32(x, g):
    ms = jnp.mean(x * x, axis=-1, keepdims=True)
    return x * lax.rsqrt(ms + RMS_EPS) * g


def _inproj_kernel(x_ref, g_ref, w_ref, wfh_ref, wfl_ref, bf_ref,
                   proj_ref, c_ref, ct_ref, carry_ref, *, tiles_per_seq, n_chunks, chunk, q_chunks):
    i = pl.program_id(0)

    @pl.when(i % tiles_per_seq == 0)
    def _():
        carry_ref[...] = jnp.zeros_like(carry_ref)

    y = _rmsnorm_f32(x_ref[...], g_ref[...])
    h_hi, h_lo = _split_bf16(y, 2)
    for c in range(n_chunks):
        sl = slice(c * chunk, (c + 1) * chunk)
        p = jnp.dot(h_hi, w_ref[:, sl], preferred_element_type=jnp.float32)
        if c in q_chunks:
            p = p * Q_SCALE
        proj_ref[:, sl] = p.astype(proj_ref.dtype)

    f = (jnp.dot(h_hi, wfh_ref[...], preferred_element_type=jnp.float32)
         + jnp.dot(h_lo, wfh_ref[...], preferred_element_type=jnp.float32)
         + jnp.dot(h_hi, wfl_ref[...], preferred_element_type=jnp.float32))
    f = f + bf_ref[...]
    logf = (jnp.minimum(f, 0.0) - jnp.log1p(jnp.exp(-jnp.abs(f)))) * LOG2E

    tm = logf.shape[0]
    row = lax.broadcasted_iota(jnp.int32, (tm, tm), 0)
    col = lax.broadcasted_iota(jnp.int32, (tm, tm), 1)
    tri = (col <= row).astype(jnp.bfloat16)
    cum = carry_ref[...]
    for p in _split_bf16(logf, 3):
        cum = cum + jnp.dot(tri, p, preferred_element_type=jnp.float32)
    c_ref[...] = cum
    ct_ref[...] = cum.T[:N_HEADS, :]
    carry_ref[...] = cum[tm - 1:tm, :]


def _inproj(x2, g, w_main, wf_hi, wf_lo, bf, seq):
    n, d = x2.shape
    cols = w_main.shape[1]
    tm = TM_PROJ
    chunk = ATTN_WIDTH
    kern = functools.partial(_inproj_kernel, tiles_per_seq=seq // tm,
                             n_chunks=cols // chunk, chunk=chunk, q_chunks=(0, 3))
    const = dict(pipeline_mode=pl.Buffered(1))
    return pl.pallas_call(
        kern,
        grid=(n // tm,),
        in_specs=[
            pl.BlockSpec((tm, d), lambda i: (i, 0)),
            pl.BlockSpec((1, d), lambda i: (0, 0), **const),
            pl.BlockSpec((d, cols), lambda i: (0, 0), **const),
            pl.BlockSpec((d, LANES), lambda i: (0, 0), **const),
            pl.BlockSpec((d, LANES), lambda i: (0, 0), **const),
            pl.BlockSpec((1, LANES), lambda i: (0, 0), **const),
        ],
        out_specs=[
            pl.BlockSpec((tm, cols), lambda i: (i, 0)),
            pl.BlockSpec((tm, LANES), lambda i: (i, 0)),
            pl.BlockSpec((N_HEADS, tm), lambda i: (0, i)),
        ],
        out_shape=[
            jax.ShapeDtypeStruct((n, cols), jnp.bfloat16),
            jax.ShapeDtypeStruct((n, LANES), jnp.float32),
            jax.ShapeDtypeStruct((N_HEADS, n), jnp.float32),
        ],
        scratch_shapes=[pltpu.VMEM((1, LANES), jnp.float32)],
        compiler_params=pltpu.CompilerParams(
            dimension_semantics=("arbitrary",), vmem_limit_bytes=VMEM_LIMIT),
        name="inproj",
    )(x2, g, w_main, wf_hi, wf_lo, bf)


def _stack_heads(q):
    lane = lax.broadcasted_iota(jnp.int32, q.shape, 1)
    return jnp.concatenate(
        [jnp.where((lane // HEAD_DIM) == h, q, jnp.zeros_like(q)) for h in range(HEADS_PER_BLOCK)], axis=0)


def _unstack_heads(acc):
    tq = acc.shape[0] // HEADS_PER_BLOCK
    lane = lax.broadcasted_iota(jnp.int32, (tq, LANES), 1)
    out = acc[:tq]
    for h in range(1, HEADS_PER_BLOCK):
        out = jnp.where((lane // HEAD_DIM) == h, acc[h * tq:(h + 1) * tq], out)
    return out


def _lane_tile(x, width):
    return jnp.concatenate([x] * (width // LANES), axis=1)


class _Chain(NamedTuple):
    score: Callable
    finish: Callable
    live: Callable


def _pipelined_tiles(k0, chains, diag):
    for ch in chains:
        ch.score(k0, 0, diag)

    def overlap(ch, kj, slot):
        ch.score(kj - 1, 1 - slot, False)
        ch.finish(kj, slot)

    def cond(state):
        p, gos = state[0], state[1:]
        return jnp.logical_and(p < k0 // 2, functools.reduce(jnp.logical_or, [g != 0 for g in gos]))

    def pair(state):
        p, gos = state[0], state[1:]
        kj = k0 - 2 * p
        for mask in range(1, 2 ** len(chains)):
            active = [c for i, c in enumerate(chains) if mask >> i & 1]
            preds = [(gos[i] != 0) if mask >> i & 1 else (gos[i] == 0) for i in range(len(chains))]

            @pl.when(functools.reduce(jnp.logical_and, preds))
            def _():
                for slot in (0, 1):
                    for ch in active:
                        overlap(ch, kj - slot, slot)

        new = [jnp.where(jnp.logical_and(g != 0, ch.live(kj - 1)), 1, 0).astype(jnp.int32)
               for g, ch in zip(gos, chains)]
        return (p + 1, *new)

    state = lax.while_loop(cond, pair, (jnp.int32(0),) + (jnp.int32(1),) * len(chains))

    for go, ch in zip(state[1:], chains):
        @pl.when(jnp.logical_and(go != 0, k0 % 2 == 1))
        def _():
            overlap(ch, 1, 0)
            ch.finish(0, 1)

        @pl.when(jnp.logical_and(go != 0, k0 % 2 == 0))
        def _():
            ch.finish(0, 0)


def _sb_chain(q_ref, k_ref, v_ref, acc_ref, carry_ref, lb_ref, lk_ref, rs_ref):
    qcat = _stack_heads(q_ref[...])
    m = HEADS_PER_BLOCK * TQ
    urow = lax.broadcasted_iota(jnp.int32, (TK, TK), 0)
    ucol = lax.broadcasted_iota(jnp.int32, (TK, TK), 1)
    upper = (ucol < urow).astype(jnp.bfloat16)

    acc_ref[...] = jnp.zeros_like(acc_ref)
    carry_ref[...] = jnp.zeros_like(carry_ref)

    def score(kj, slot, diag):
        k = k_ref[pl.ds(kj * TK, TK), :]
        z = lax.dot_general(qcat, k, _NT, preferred_element_type=jnp.float32)
        t = jnp.log2(1.0 + jnp.exp2(-jnp.abs(z)))
        log_beta = jnp.minimum(z, 0.0) - t
        log_keep = log_beta - z
        if diag:
            row = lax.broadcasted_iota(jnp.int32, (m, TK), 0)
            col = lax.broadcasted_iota(jnp.int32, (m, TK), 1)
            below = col < (row % TQ)
            log_keep = jnp.where(below, log_keep, 0.0)
            log_beta = jnp.where(below, log_beta, -jnp.inf)
        lb_ref[slot] = log_beta
        lk_ref[slot] = log_keep.astype(jnp.bfloat16)
        rs_ref[slot] = jnp.broadcast_to(jnp.sum(log_keep, axis=-1, keepdims=True), (m, LANES))

    def finish(kj, slot):
        v = v_ref[pl.ds(kj * TK, TK), :]
        rest = jnp.dot(lk_ref[slot], upper, preferred_element_type=jnp.float32)
        carry = carry_ref[...]
        a = jnp.exp2(lb_ref[slot] + rest + _lane_tile(carry, TK))
        acc_ref[...] += jnp.dot(a.astype(v.dtype), v, preferred_element_type=jnp.float32)
        carry_ref[...] = carry + rs_ref[slot]

    def live(kj):
        return jnp.max(carry_ref[...]) > -SKIP_LOG2

    return _Chain(score, finish, live)


def _fox_chain(hb, qi, q_ref, k_ref, v_ref, c_ref, ct_ref, acc_ref, m_ref, l_ref, z_ref, rm_ref,
               kn_ref):
    qcat = _stack_heads(q_ref[...])
    lane = lax.broadcasted_iota(jnp.int32, (TQ, LANES), 1)
    cblk = c_ref[...]
    cq = jnp.concatenate(
        [jnp.broadcast_to(
            jnp.sum(jnp.where(lane == hb * HEADS_PER_BLOCK + h, cblk, 0.0), axis=-1, keepdims=True),
            (TQ, TK)) for h in range(HEADS_PER_BLOCK)], axis=0)
    m = HEADS_PER_BLOCK * TQ
    ones = jnp.ones((TK, LANES), jnp.bfloat16)

    @pl.when(qi == 0)
    def _():
        kf = k_ref[...].astype(jnp.float32)
        ksq = kf * kf
        klane = lax.broadcasted_iota(jnp.int32, ksq.shape, 1)
        for h in range(HEADS_PER_BLOCK):
            n2 = jnp.sum(jnp.where((klane // HEAD_DIM) == h, ksq, 0.0), axis=-1, keepdims=True)
            kn_ref[h * TQ:(h + 1) * TQ, :] = jnp.broadcast_to(
                jnp.sqrt(jnp.max(n2, axis=0, keepdims=True)), (TQ, LANES))

    qf = qcat.astype(jnp.float32)
    qn = jnp.broadcast_to(jnp.sqrt(jnp.sum(qf * qf, axis=-1, keepdims=True)), (m, LANES))

    acc_ref[...] = jnp.zeros_like(acc_ref)
    l_ref[...] = jnp.zeros_like(l_ref)
    m_ref[...] = jnp.full_like(m_ref, -jnp.inf)

    def score(kj, slot, diag):
        k = k_ref[pl.ds(kj * TK, TK), :]
        ck = jnp.concatenate(
            [jnp.broadcast_to(ct_ref[pl.ds(hb * HEADS_PER_BLOCK + h, 1), pl.ds(kj * TK, TK)], (TQ, TK))
             for h in range(HEADS_PER_BLOCK)], axis=0)
        z = lax.dot_general(qcat, k, _NT, preferred_element_type=jnp.float32)
        z = (z + cq) - ck
        if diag:
            row = lax.broadcasted_iota(jnp.int32, (m, TK), 0)
            col = lax.broadcasted_iota(jnp.int32, (m, TK), 1)
            z = jnp.where(col <= (row % TQ), z, -jnp.inf)
        z_ref[slot] = z
        rm_ref[slot] = jnp.broadcast_to(jnp.max(z, axis=-1, keepdims=True), (m, LANES))

    def finish(kj, slot):
        v = v_ref[pl.ds(kj * TK, TK), :]
        m_old = m_ref[...]
        m_new = jnp.maximum(m_old, rm_ref[slot])
        alpha = jnp.exp2(m_old - m_new)
        p = jnp.exp2(z_ref[slot] - _lane_tile(m_new, TK))
        pv = jnp.dot(p.astype(v.dtype), jnp.concatenate([v, ones], axis=1),
                     preferred_element_type=jnp.float32)
        acc_ref[...] = alpha * acc_ref[...] + pv[:, :LANES]
        l_ref[...] = alpha * l_ref[...] + pv[:, LANES:]
        m_ref[...] = m_new

    def live(kj):
        cb = jnp.concatenate(
            [jnp.broadcast_to(
                jnp.max(ct_ref[pl.ds(hb * HEADS_PER_BLOCK + h, 1), pl.ds(kj * TK, TK)],
                        axis=-1, keepdims=True), (TQ, LANES)) for h in range(HEADS_PER_BLOCK)], axis=0)
        bound = qn * kn_ref[...] + cq[:, :LANES] - cb - m_ref[...]
        return jnp.max(bound) > -SKIP_LOG2

    return _Chain(score, finish, live)


SB_HEAD_TILES = 2
FOX_HEAD_TILES = 4


def _attn_kernel(qs_ref, ks_ref, vs_ref, qf_ref, kf_ref, vf_ref, c_ref, ct_ref, osb_ref, ofx_ref,
                 sb_acc, sb_carry, sb_lb, sb_lk, sb_rs, fx_acc, fx_m, fx_l, fx_z, fx_rm, fx_kn):
    hb = pl.program_id(1)
    qi = pl.program_id(2)
    sb = _sb_chain(qs_ref, ks_ref, vs_ref, sb_acc, sb_carry, sb_lb, sb_lk, sb_rs)
    fx = _fox_chain(hb, qi, qf_ref, kf_ref, vf_ref, c_ref, ct_ref, fx_acc, fx_m, fx_l, fx_z, fx_rm, fx_kn)
    head = max(SB_HEAD_TILES, FOX_HEAD_TILES)

    @pl.when(qi < head)
    def _():
        _pipelined_tiles(qi, [sb, fx], True)

    @pl.when(qi >= head)
    def _():
        for i in range(head + 1):
            for ch, count in ((sb, SB_HEAD_TILES), (fx, FOX_HEAD_TILES)):
                if i < count:
                    ch.score(qi - i, i, i == 0)
                if 1 <= i <= count:
                    ch.finish(qi - i + 1, i - 1)
        for ch, count in ((sb, SB_HEAD_TILES), (fx, FOX_HEAD_TILES)):
            @pl.when(ch.live(qi - count + 1))
            def _():
                _pipelined_tiles(qi - count, [ch], False)

    osb_ref[...] = _unstack_heads(sb_acc[...]).astype(osb_ref.dtype)
    ofx_ref[...] = _unstack_heads(fx_acc[...] / fx_l[...]).astype(ofx_ref.dtype)


def _attention(proj3, c3, ct):
    b, s, _ = proj3.shape
    m = HEADS_PER_BLOCK * TQ

    def q_spec(blk):
        return pl.BlockSpec((None, TQ, LANES), lambda bi, hb, qi: (bi, qi, blk * N_HEAD_BLOCKS + hb))

    def kv_spec(blk):
        return pl.BlockSpec((None, s, LANES), lambda bi, hb, qi: (bi, 0, blk * N_HEAD_BLOCKS + hb))

    out_spec = pl.BlockSpec((None, TQ, LANES), lambda bi, hb, qi: (bi, qi, hb))
    out_shape = jax.ShapeDtypeStruct((b, s, ATTN_WIDTH), jnp.bfloat16)
    f32 = jnp.float32
    return pl.pallas_call(
        _attn_kernel,
        grid=(b, N_HEAD_BLOCKS, s // TQ),
        in_specs=[q_spec(0), kv_spec(1), kv_spec(2), q_spec(3), kv_spec(4), kv_spec(5),
                  pl.BlockSpec((None, TQ, LANES), lambda bi, hb, qi: (bi, qi, 0)),
                  pl.BlockSpec((N_HEADS, s), lambda bi, hb, qi: (0, bi))],
        out_specs=[out_spec, out_spec],
        out_shape=[out_shape, out_shape],
        scratch_shapes=[pltpu.VMEM((m, LANES), f32),
                        pltpu.VMEM((m, LANES), f32),
                        pltpu.VMEM((2, m, TK), f32),
                        pltpu.VMEM((2, m, TK), jnp.bfloat16),
                        pltpu.VMEM((2, m, LANES), f32),
                        pltpu.VMEM((m, LANES), f32),
                        pltpu.VMEM((m, LANES), f32),
                        pltpu.VMEM((m, LANES), f32),
                        pltpu.VMEM((FOX_HEAD_TILES, m, TK), f32),
                        pltpu.VMEM((FOX_HEAD_TILES, m, LANES), f32),
                        pltpu.VMEM((m, LANES), f32)],
        compiler_params=pltpu.CompilerParams(
            dimension_semantics=("arbitrary", "arbitrary", "arbitrary"),
            vmem_limit_bytes=VMEM_LIMIT),
        name="attn",
    )(proj3, proj3, proj3, proj3, proj3, proj3, c3, ct)


def _post_kernel(osb_ref, ofx_ref, gsb_ref, gfx_ref, x_ref, wosb_ref, wofx_ref, wout_ref,
                 gn_ref, wrh_ref, wrl_ref, br_ref,
                 x1_ref, h_ref, ids_ref, aux_ref):
    y_sb = jnp.dot(osb_ref[...], wosb_ref[...], preferred_element_type=jnp.float32)
    y_fx = jnp.dot(ofx_ref[...], wofx_ref[...], preferred_element_type=jnp.float32)
    mixed = (jax.nn.sigmoid(gsb_ref[...].astype(jnp.float32)) * y_sb
             + jax.nn.sigmoid(gfx_ref[...].astype(jnp.float32)) * y_fx)
    x1 = x_ref[...] + jnp.dot(mixed.astype(jnp.bfloat16), wout_ref[...],
                              preferred_element_type=jnp.float32)
    x1_ref[...] = x1
    y = _rmsnorm_f32(x1, gn_ref[...])
    h_hi, h_lo = _split_bf16(y, 2)
    h_ref[...] = h_hi

    lg = (lax.dot_general(wrh_ref[...], h_hi, _NT, preferred_element_type=jnp.float32)
          + lax.dot_general(wrh_ref[...], h_lo, _NT, preferred_element_type=jnp.float32)
          + lax.dot_general(wrl_ref[...], h_hi, _NT, preferred_element_type=jnp.float32))
    lg = lg + br_ref[...]
    tm = lg.shape[1]
    sub = lax.broadcasted_iota(jnp.int32, (8, tm), 0)
    neg = -jnp.inf

    gl = jnp.where(sub < N_GROUPS, lg[0:8], neg)
    gm = jnp.max(gl, axis=0, keepdims=True)
    g_w = 1.0 / jnp.sum(jnp.exp(gl - gm), axis=0, keepdims=True)
    g_idx = jnp.min(jnp.where(gl == gm, sub, 8), axis=0, keepdims=True)

    e_sel = lg[8:16]
    for g in range(1, N_GROUPS):
        e_sel = jnp.where(g_idx == g, lg[8 + 8 * g:16 + 8 * g], e_sel)
    m1 = jnp.max(e_sel, axis=0, keepdims=True)
    i1 = jnp.min(jnp.where(e_sel == m1, sub, 8), axis=0, keepdims=True)
    e_rest = jnp.where(sub == i1, neg, e_sel)
    m2 = jnp.max(e_rest, axis=0, keepdims=True)
    i2 = jnp.min(jnp.where(e_rest == m2, sub, 8), axis=0, keepdims=True)
    p2 = jnp.exp(m2 - m1)
    w1 = g_w / (1.0 + p2)
    w2 = g_w * p2 / (1.0 + p2)
    base = g_idx * EXPERTS_PER_GROUP
    ids_ref[...] = jnp.where(sub == 0, base + i1, jnp.where(sub == 1, base + i2, 0))
    dense_w = jnp.where(sub == i1, w1, jnp.where(sub == i2, w2, 0.0))
    group_hot = jnp.where(sub == g_idx, 1.0, 0.0)
    record = jnp.concatenate(
        [dense_w, group_hot, jnp.zeros((LANES - 16, tm), jnp.float32)], axis=0)
    aux_ref[...] = record.T


def _post_attention(o_sb, o_fx, proj, x2, wosb, wofx, wout, gn, wr_hi, wr_lo, br):
    n, d = x2.shape
    tm = TM_POST
    gate_blk = (3 * ATTN_WIDTH * 2) // d
    const = dict(pipeline_mode=pl.Buffered(1))
    return pl.pallas_call(
        _post_kernel,
        grid=(n // tm,),
        in_specs=[
            pl.BlockSpec((tm, ATTN_WIDTH), lambda i: (i, 0)),
            pl.BlockSpec((tm, ATTN_WIDTH), lambda i: (i, 0)),
            pl.BlockSpec((tm, d), lambda i: (i, gate_blk)),
            pl.BlockSpec((tm, d), lambda i: (i, gate_blk + 1)),
            pl.BlockSpec((tm, d), lambda i: (i, 0)),
            pl.BlockSpec((ATTN_WIDTH, d), lambda i: (0, 0), **const),
            pl.BlockSpec((ATTN_WIDTH, d), lambda i: (0, 0), **const),
            pl.BlockSpec((d, d), lambda i: (0, 0), **const),
            pl.BlockSpec((1, d), lambda i: (0, 0), **const),
            pl.BlockSpec((ROUTER_ROWS, d), lambda i: (0, 0), **const),
            pl.BlockSpec((ROUTER_ROWS, d), lambda i: (0, 0), **const),
            pl.BlockSpec((ROUTER_ROWS, 1), lambda i: (0, 0), **const),
        ],
        out_specs=[
            pl.BlockSpec((tm, d), lambda i: (i, 0)),
            pl.BlockSpec((tm, d), lambda i: (i, 0)),
            pl.BlockSpec((8, tm), lambda i: (0, i)),
            pl.BlockSpec((tm, LANES), lambda i: (i, 0)),
        ],
        out_shape=[
            jax.ShapeDtypeStruct((n, d), jnp.float32),
            jax.ShapeDtypeStruct((n, d), jnp.bfloat16),
            jax.ShapeDtypeStruct((8, n), jnp.int32),
            jax.ShapeDtypeStruct((n, LANES), jnp.float32),
        ],
        compiler_params=pltpu.CompilerParams(
            dimension_semantics=("arbitrary",), vmem_limit_bytes=VMEM_LIMIT),
        name="post_attn",
    )(o_sb, o_fx, proj, proj, x2, wosb, wofx, wout, gn, wr_hi, wr_lo, br)


TM_SORT = 256
T_GROUP = 256
ROW_ALIGN = 16
ZERO_TAIL = TM_SORT + T_GROUP
SORT_ROWS = TM_SORT + N_GROUPS * ROW_ALIGN
GROUP_LANE0 = EXPERTS_PER_GROUP


def _sort_capacity(n):
    cap = n + ROW_ALIGN * (n // TM_SORT) + ZERO_TAIL
    return -(-cap // T_GROUP) * T_GROUP


def _max_group_tiles(n):
    padded = n + (ROW_ALIGN - 1) * N_GROUPS * (n // TM_SORT)
    return padded // T_GROUP + N_GROUPS * (ZERO_TAIL // T_GROUP + 1)


def _dispatch_kernel(h_ref, ids_ref, aux_ref, hg_ref, wg_ref, toff_ref, tot_ref,
                     hbuf, wbuf, sems, off_ref, prev_ref, src_ref):
    i = pl.program_id(0)
    tm = TM_SORT
    rows = SORT_ROWS
    slot = i % 2

    def chunk_copies(g, slot, src, off):
        src = src if isinstance(src, int) else pl.multiple_of(src, ROW_ALIGN)
        off = pl.multiple_of(off, ROW_ALIGN)
        return (pltpu.make_async_copy(hbuf.at[slot, pl.ds(src, tm)], hg_ref.at[g, pl.ds(off, tm)],
                                      sems.at[0, g]),
                pltpu.make_async_copy(wbuf.at[slot, pl.ds(src, tm)], wg_ref.at[g, pl.ds(off, tm)],
                                      sems.at[1, g]))

    @pl.when(i == 0)
    def _():
        for g in range(N_GROUPS):
            off_ref[g] = 0
        hbuf[...] = jnp.zeros_like(hbuf)
        wbuf[...] = jnp.zeros_like(wbuf)

    gid = ids_ref[0:1, :] // EXPERTS_PER_GROUP
    sub = lax.broadcasted_iota(jnp.int32, (8, tm), 0)
    hot = sub == gid
    r_io = lax.broadcasted_iota(jnp.int32, (tm, tm), 0)
    c_io = lax.broadcasted_iota(jnp.int32, (tm, tm), 1)
    before = (r_io < c_io).astype(jnp.bfloat16)
    seen = jnp.dot(jnp.where(hot, 1.0, 0.0).astype(jnp.bfloat16), before,
                   preferred_element_type=jnp.float32)
    pos = jnp.sum(jnp.where(hot, seen, 0.0), axis=0, keepdims=True).astype(jnp.int32)

    sub_out = lax.broadcasted_iota(jnp.int32, (8, LANES), 0)
    offs = jnp.zeros((8, LANES), jnp.int32)
    start = jnp.int32(0)
    starts, dests = [], []
    for g in range(N_GROUPS):
        count = jnp.sum(jnp.where(gid == g, 1.0, 0.0)).astype(jnp.int32)
        pos = pos + jnp.where(gid == g, start, 0)
        off = off_ref[g]
        offs = jnp.where(sub_out == g, off, offs)
        starts.append(start)
        dests.append(off)
        padded = (count + (ROW_ALIGN - 1)) // ROW_ALIGN * ROW_ALIGN
        off_ref[g] = off + padded
        start = start + padded
    toff_ref[...] = offs

    s_io = lax.broadcasted_iota(jnp.int32, (rows, tm), 0)
    sel = jnp.where(pos == s_io, 1.0, 0.0).astype(jnp.bfloat16)
    hbuf[slot, 0:rows] = jnp.dot(sel, h_ref[...], preferred_element_type=jnp.float32).astype(hbuf.dtype)
    aux3 = jnp.concatenate(_split_bf16(aux_ref[...], 3), axis=1)
    w3 = jnp.dot(sel, aux3, preferred_element_type=jnp.float32)
    wbuf[slot, 0:rows] = w3[:, :LANES] + w3[:, LANES:2 * LANES] + w3[:, 2 * LANES:]

    @pl.when(i > 0)
    def _():
        for g in range(N_GROUPS):
            for c in chunk_copies(g, 1 - slot, src_ref[g], prev_ref[g]):
                c.wait()

    for g in range(N_GROUPS):
        src_ref[g] = starts[g]
        prev_ref[g] = dests[g]
        for c in chunk_copies(g, slot, starts[g], dests[g]):
            c.start()

    total = jnp.zeros((8, LANES), jnp.int32)
    for g in range(N_GROUPS):
        total = jnp.where(sub_out == g, off_ref[g], total)
    tot_ref[...] = total

    @pl.when(i == pl.num_programs(0) - 1)
    def _():
        for g in range(N_GROUPS):
            for c in chunk_copies(g, slot, src_ref[g], prev_ref[g]):
                c.wait()
        hbuf[...] = jnp.zeros_like(hbuf)
        wbuf[...] = jnp.zeros_like(wbuf)
        for g in range(N_GROUPS):
            for z in range(ZERO_TAIL // tm):
                for c in chunk_copies(g, 0, 0, off_ref[g] + z * tm):
                    c.start()
                for c in chunk_copies(g, 0, 0, off_ref[g] + z * tm):
                    c.wait()


def _dispatch(h, ids, aux):
    n, d = h.shape
    tm = TM_SORT
    cap = _sort_capacity(n)
    steps = n // tm
    return pl.pallas_call(
        _dispatch_kernel,
        grid=(steps,),
        in_specs=[
            pl.BlockSpec((tm, d), lambda i: (i, 0)),
            pl.BlockSpec((8, tm), lambda i: (0, i)),
            pl.BlockSpec((tm, LANES), lambda i: (i, 0)),
        ],
        out_specs=[
            pl.BlockSpec(memory_space=pl.ANY),
            pl.BlockSpec(memory_space=pl.ANY),
            pl.BlockSpec((8, LANES), lambda i: (i, 0)),
            pl.BlockSpec((8, LANES), lambda i: (0, 0)),
        ],
        out_shape=[
            jax.ShapeDtypeStruct((N_GROUPS, cap, d), jnp.bfloat16),
            jax.ShapeDtypeStruct((N_GROUPS, cap, LANES), jnp.float32),
            jax.ShapeDtypeStruct((steps * 8, LANES), jnp.int32),
            jax.ShapeDtypeStruct((8, LANES), jnp.int32),
        ],
        scratch_shapes=[pltpu.VMEM((2, SORT_ROWS + tm, d), jnp.bfloat16),
                        pltpu.VMEM((2, SORT_ROWS + tm, LANES), jnp.float32),
                        pltpu.SemaphoreType.DMA((2, N_GROUPS)),
                        pltpu.SMEM((N_GROUPS,), jnp.int32),
                        pltpu.SMEM((N_GROUPS,), jnp.int32),
                        pltpu.SMEM((N_GROUPS,), jnp.int32)],
        compiler_params=pltpu.CompilerParams(
            dimension_semantics=("arbitrary",), vmem_limit_bytes=VMEM_LIMIT),
        name="moe_sort",
    )(h, ids, aux)


def _group_moe_kernel(tg_ref, tj_ref, tv_ref, h_ref, w_ref, w1_ref, w3_ref, w2_ref, y_ref):
    i = pl.program_id(0)

    @pl.when(tv_ref[i] != 0)
    def _():
        h = h_ref[...]
        w = w_ref[...]
        hidden = []
        for e in range(EXPERTS_PER_GROUP):
            a = jnp.dot(h, w1_ref[e], preferred_element_type=jnp.float32)
            b = jnp.dot(h, w3_ref[e], preferred_element_type=jnp.float32)
            hidden.append(((a * jax.nn.sigmoid(a)) * b * w[:, e:e + 1]).astype(jnp.bfloat16))
        ne, de, d = w2_ref.shape
        y = jnp.dot(jnp.concatenate(hidden, axis=1), w2_ref[...].reshape(ne * de, d),
                    preferred_element_type=jnp.float32)
        y_ref[...] = y.astype(y_ref.dtype)


def _group_moe(tile_group, tile_index, tile_valid, hg, wg, w1, w3, w2):
    ng, cap, d = hg.shape
    t = T_GROUP
    de = w1.shape[-1]
    grid_spec = pltpu.PrefetchScalarGridSpec(
        num_scalar_prefetch=3,
        grid=(tile_group.shape[0],),
        in_specs=[
            pl.BlockSpec((None, t, d), lambda i, tg, tj, tv: (tg[i], tj[i], 0)),
            pl.BlockSpec((None, t, LANES), lambda i, tg, tj, tv: (tg[i], tj[i], 0)),
            pl.BlockSpec((EXPERTS_PER_GROUP, d, de), lambda i, tg, tj, tv: (tg[i], 0, 0)),
            pl.BlockSpec((EXPERTS_PER_GROUP, d, de), lambda i, tg, tj, tv: (tg[i], 0, 0)),
            pl.BlockSpec((EXPERTS_PER_GROUP, de, d), lambda i, tg, tj, tv: (tg[i], 0, 0)),
        ],
        out_specs=pl.BlockSpec((None, t, d), lambda i, tg, tj, tv: (tg[i], tj[i], 0)),
    )
    return pl.pallas_call(
        _group_moe_kernel,
        grid_spec=grid_spec,
        out_shape=jax.ShapeDtypeStruct((ng, cap, d), jnp.bfloat16),
        compiler_params=pltpu.CompilerParams(
            dimension_semantics=("arbitrary",), vmem_limit_bytes=VMEM_LIMIT),
        name="moe_experts",
    )(tile_group, tile_index, tile_valid, hg, wg, w1, w3, w2)


def _combine_kernel(toff_ref, x1_ref, aux_ref, gn_ref, yg_ref, o_ref, ybuf, sems):
    i = pl.program_id(0)
    tm = TM_SORT
    slot = i % 2

    def fetch(step, slot):
        return [pltpu.make_async_copy(
            yg_ref.at[g, pl.ds(pl.multiple_of(toff_ref[step * N_GROUPS + g], ROW_ALIGN), tm)],
            ybuf.at[slot, g], sems.at[slot, g]) for g in range(N_GROUPS)]

    @pl.when(i == 0)
    def _():
        for c in fetch(0, 0):
            c.start()

    @pl.when(i + 1 < pl.num_programs(0))
    def _():
        for c in fetch(i + 1, 1 - slot):
            c.start()

    for c in fetch(i, slot):
        c.wait()

    aux = aux_ref[...]
    lane = lax.broadcasted_iota(jnp.int32, (tm, LANES), 1)
    is_group_lane = jnp.logical_and(lane >= GROUP_LANE0, lane < GROUP_LANE0 + N_GROUPS)
    hot = jnp.where(is_group_lane, aux, 0.0)
    r_io = lax.broadcasted_iota(jnp.int32, (tm, tm), 0)
    c_io = lax.broadcasted_iota(jnp.int32, (tm, tm), 1)
    before = (c_io < r_io).astype(jnp.bfloat16)
    seen = jnp.dot(before, hot.astype(jnp.bfloat16), preferred_element_type=jnp.float32)
    rank = jnp.sum(hot * seen, axis=-1, keepdims=True).astype(jnp.int32)
    y = jnp.zeros((tm, x1_ref.shape[1]), jnp.float32)
    for g in range(N_GROUPS):
        mine = aux[:, GROUP_LANE0 + g:GROUP_LANE0 + g + 1] > 0.5
        sel = jnp.where(jnp.logical_and(rank == c_io, mine), 1.0, 0.0).astype(jnp.bfloat16)
        y = y + jnp.dot(sel, ybuf[slot, g], preferred_element_type=jnp.float32)
    o_ref[...] = _rmsnorm_f32(x1_ref[...] + y, gn_ref[...])


def _combine(toff, x1, aux, gn, yg):
    n, d = x1.shape
    tm = TM_SORT
    grid_spec = pltpu.PrefetchScalarGridSpec(
        num_scalar_prefetch=1,
        grid=(n // tm,),
        in_specs=[
            pl.BlockSpec((tm, d), lambda i, toff: (i, 0)),
            pl.BlockSpec((tm, LANES), lambda i, toff: (i, 0)),
            pl.BlockSpec((1, d), lambda i, toff: (0, 0)),
            pl.BlockSpec(memory_space=pl.ANY),
        ],
        out_specs=pl.BlockSpec((tm, d), lambda i, toff: (i, 0)),
        scratch_shapes=[pltpu.VMEM((2, N_GROUPS, tm, d), jnp.bfloat16),
                        pltpu.SemaphoreType.DMA((2, N_GROUPS))],
    )
    return pl.pallas_call(
        _combine_kernel,
        grid_spec=grid_spec,
        out_shape=jax.ShapeDtypeStruct((n, d), jnp.float32),
        compiler_params=pltpu.CompilerParams(
            dimension_semantics=("arbitrary",), vmem_limit_bytes=VMEM_LIMIT),
        name="moe_combine",
    )(toff, x1, aux, gn, yg)


def _group_tile_table(totals, n):
    tiles = (totals + (TM_SORT + T_GROUP - 1)) // T_GROUP
    ends = jnp.cumsum(tiles)
    steps = jnp.arange(_max_group_tiles(n), dtype=jnp.int32)
    valid = steps < ends[-1]
    step = jnp.minimum(steps, ends[-1] - 1)
    group = jnp.sum((step[:, None] >= ends[None, :]).astype(jnp.int32), axis=1)
    index = step - (ends - tiles)[group]
    return group.astype(jnp.int32), index.astype(jnp.int32), valid.astype(jnp.int32)


def kernel(x, norm_attn, w_in, b_forget, w_o_sb, w_o_fox, w_out, norm_ffn, w_router_group,
           b_router_group, w_router_expert, b_router_expert, w1, w3, w2, norm_final):
    b, s, d = x.shape
    n = b * s
    depth = w_in.shape[0]
    assert depth == 1, "the final norm is fused into the MoE un-sort of a single layer"
    assert s % TM_PROJ == 0 and s % TQ == 0 and n % TM_SORT == 0 and ZERO_TAIL % TM_SORT == 0
    bf16 = jnp.bfloat16
    n_main = 6 * ATTN_WIDTH
    x2 = x.reshape(n, d)
    for l in range(depth):
        w_l = w_in[l]
        w_main = jnp.concatenate([w_l[:, :n_main], w_l[:, n_main + N_HEADS:]], axis=1).astype(bf16)
        wf = jnp.pad(w_l[:, n_main:n_main + N_HEADS], ((0, 0), (0, LANES - N_HEADS)))
        wf_hi = wf.astype(bf16)
        wf_lo = (wf - wf_hi.astype(jnp.float32)).astype(bf16)
        bf = jnp.pad(b_forget[l], (0, LANES - N_HEADS)).reshape(1, LANES)
        wr = jnp.concatenate([w_router_group[l].T, jnp.zeros((8 - N_GROUPS, d), jnp.float32),
                              w_router_expert[l].T], axis=0)
        wr_hi = wr.astype(bf16)
        wr_lo = (wr - wr_hi.astype(jnp.float32)).astype(bf16)
        br = jnp.concatenate([b_router_group[l], jnp.zeros((8 - N_GROUPS,), jnp.float32),
                              b_router_expert[l]]).reshape(ROUTER_ROWS, 1)

        proj, c, ct = _inproj(x2, norm_attn[l].reshape(1, d), w_main, wf_hi, wf_lo, bf, s)
        proj3 = proj.reshape(b, s, proj.shape[1])
        o_sb, o_fx = _attention(proj3, c.reshape(b, s, LANES), ct)
        x1, h, ids, aux = _post_attention(
            o_sb.reshape(n, ATTN_WIDTH), o_fx.reshape(n, ATTN_WIDTH), proj, x2,
            w_o_sb[l].astype(bf16), w_o_fox[l].astype(bf16), w_out[l].astype(bf16),
            norm_ffn[l].reshape(1, d), wr_hi, wr_lo, br)
        hg, wg, toff, totals = _dispatch(h, ids, aux)
        tile_group, tile_index, tile_valid = _group_tile_table(totals[:N_GROUPS, 0], n)
        yg = _group_moe(tile_group, tile_index, tile_valid, hg, wg,
                        w1[l].astype(bf16), w3[l].astype(bf16), w2[l].astype(bf16))
        chunk_starts = toff.reshape(n // TM_SORT, 8, LANES)[:, :N_GROUPS, 0].reshape(-1)
        x2 = _combine(chunk_starts, x1, aux, norm_final.reshape(1, d), yg)
    return x2.reshape(b, s, d)
```

```python
import functools
import math
from typing import Callable, NamedTuple

import jax
import jax.numpy as jnp
from jax import lax
from jax.experimental import pallas as pl
from jax.experimental.pallas import tpu as pltpu

HEAD_DIM = 64
N_HEADS = 8
ATTN_WIDTH = N_HEADS * HEAD_DIM
N_GROUPS = 4
EXPERTS_PER_GROUP = 8
N_EXPERTS = N_GROUPS * EXPERTS_PER_GROUP
RMS_EPS = 1e-6
LANES = 128
HEADS_PER_BLOCK = LANES // HEAD_DIM
N_HEAD_BLOCKS = N_HEADS // HEADS_PER_BLOCK
ROUTER_ROWS = 8 + N_EXPERTS
VMEM_LIMIT = 56 * 1024 * 1024

TM_PROJ = 512
TQ = 256
TK = 256
TM_POST = 512

_NT = (((1,), (1,)), ((), ()))
LOG2E = math.log2(math.e)
Q_SCALE = LOG2E / math.sqrt(HEAD_DIM)
SKIP_LOG2 = 160.0


def _split_bf16(v, parts):
    out = []
    r = v
    for i in range(parts):
        p = r.astype(jnp.bfloat16)
        out.append(p)
        if i + 1 < parts:
            r = r - p.astype(jnp.float32)
    return out


def _rmsnorm_f32(x, g):
    ms = jnp.mean(x * x, axis=-1, keepdims=True)
    return x * lax.rsqrt(ms + RMS_EPS) * g


def _inproj_kernel(x_ref, g_ref, w_ref, wfh_ref, wfl_ref, bf_ref,
                   proj_ref, c_ref, ct_ref, carry_ref, *, tiles_per_seq, n_chunks, chunk, q_chunks):
    i = pl.program_id(0)

    @pl.when(i % tiles_per_seq == 0)
    def _():
        carry_ref[...] = jnp.zeros_like(carry_ref)

    y = _rmsnorm_f32(x_ref[...], g_ref[...])
    h_hi, h_lo = _split_bf16(y, 2)
    for c in range(n_chunks):
        sl = slice(c * chunk, (c + 1) * chunk)
        p = jnp.dot(h_hi, w_ref[:, sl], preferred_element_type=jnp.float32)
        if c in q_chunks:
            p = p * Q_SCALE
        proj_ref[:, sl] = p.astype(proj_ref.dtype)

    f = (jnp.dot(h_hi, wfh_ref[...], preferred_element_type=jnp.float32)
         + jnp.dot(h_lo, wfh_ref[...], preferred_element_type=jnp.float32)
         + jnp.dot(h_hi, wfl_ref[...], preferred_element_type=jnp.float32))
    f = f + bf_ref[...]
    logf = (jnp.minimum(f, 0.0) - jnp.log1p(jnp.exp(-jnp.abs(f)))) * LOG2E

    tm = logf.shape[0]
    row = lax.broadcasted_iota(jnp.int32, (tm, tm), 0)
    col = lax.broadcasted_iota(jnp.int32, (tm, tm), 1)
    tri = (col <= row).astype(jnp.bfloat16)
    cum = carry_ref[...]
    for p in _split_bf16(logf, 3):
        cum = cum + jnp.dot(tri, p, preferred_element_type=jnp.float32)
    c_ref[...] = cum
    ct_ref[...] = cum.T[:N_HEADS, :]
    carry_ref[...] = cum[tm - 1:tm, :]


def _inproj(x2, g, w_main, wf_hi, wf_lo, bf, seq):
    n, d = x2.shape
    cols = w_main.shape[1]
    tm = TM_PROJ
    chunk = ATTN_WIDTH
    kern = functools.partial(_inproj_kernel, tiles_per_seq=seq // tm,
                             n_chunks=cols // chunk, chunk=chunk, q_chunks=(0, 3))
    const = dict(pipeline_mode=pl.Buffered(1))
    return pl.pallas_call(
        kern,
        grid=(n // tm,),
        in_specs=[
            pl.BlockSpec((tm, d), lambda i: (i, 0)),
            pl.BlockSpec((1, d), lambda i: (0, 0), **const),
            pl.BlockSpec((d, cols), lambda i: (0, 0), **const),
            pl.BlockSpec((d, LANES), lambda i: (0, 0), **const),
            pl.BlockSpec((d, LANES), lambda i: (0, 0), **const),
            pl.BlockSpec((1, LANES), lambda i: (0, 0), **const),
        ],
        out_specs=[
            pl.BlockSpec((tm, cols), lambda i: (i, 0)),
            pl.BlockSpec((tm, LANES), lambda i: (i, 0)),
            pl.BlockSpec((N_HEADS, tm), lambda i: (0, i)),
        ],
        out_shape=[
            jax.ShapeDtypeStruct((n, cols), jnp.bfloat16),
            jax.ShapeDtypeStruct((n, LANES), jnp.float32),
            jax.ShapeDtypeStruct((N_HEADS, n), jnp.float32),
        ],
        scratch_shapes=[pltpu.VMEM((1, LANES), jnp.float32)],
        compiler_params=pltpu.CompilerParams(
            dimension_semantics=("arbitrary",), vmem_limit_bytes=VMEM_LIMIT),
        name="inproj",
    )(x2, g, w_main, wf_hi, wf_lo, bf)


def _stack_heads(q):
    lane = lax.broadcasted_iota(jnp.int32, q.shape, 1)
    return jnp.concatenate(
        [jnp.where((lane // HEAD_DIM) == h, q, jnp.zeros_like(q)) for h in range(HEADS_PER_BLOCK)], axis=0)


def _unstack_heads(acc):
    tq = acc.shape[0] // HEADS_PER_BLOCK
    lane = lax.broadcasted_iota(jnp.int32, (tq, LANES), 1)
    out = acc[:tq]
    for h in range(1, HEADS_PER_BLOCK):
        out = jnp.where((lane // HEAD_DIM) == h, acc[h * tq:(h + 1) * tq], out)
    return out


def _lane_tile(x, width):
    return jnp.concatenate([x] * (width // LANES), axis=1)


class _Chain(NamedTuple):
    score: Callable
    finish: Callable
    live: Callable


def _pipelined_tiles(k0, chains, diag):
    for ch in chains:
        ch.score(k0, 0, diag)

    def overlap(ch, kj, slot):
        ch.score(kj - 1, 1 - slot, False)
        ch.finish(kj, slot)

    def cond(state):
        p, gos = state[0], state[1:]
        return jnp.logical_and(p < k0 // 2, functools.reduce(jnp.logical_or, [g != 0 for g in gos]))

    def pair(state):
        p, gos = state[0], state[1:]
        kj = k0 - 2 * p
        for mask in range(1, 2 ** len(chains)):
            active = [c for i, c in enumerate(chains) if mask >> i & 1]
            preds = [(gos[i] != 0) if mask >> i & 1 else (gos[i] == 0) for i in range(len(chains))]

            @pl.when(functools.reduce(jnp.logical_and, preds))
            def _():
                for slot in (0, 1):
                    for ch in active:
                        overlap(ch, kj - slot, slot)

        new = [jnp.where(jnp.logical_and(g != 0, ch.live(kj - 1)), 1, 0).astype(jnp.int32)
               for g, ch in zip(gos, chains)]
        return (p + 1, *new)

    state = lax.while_loop(cond, pair, (jnp.int32(0),) + (jnp.int32(1),) * len(chains))

    for go, ch in zip(state[1:], chains):
        @pl.when(jnp.logical_and(go != 0, k0 % 2 == 1))
        def _():
            overlap(ch, 1, 0)
            ch.finish(0, 1)

        @pl.when(jnp.logical_and(go != 0, k0 % 2 == 0))
        def _():
            ch.finish(0, 0)


def _sb_chain(q_ref, k_ref, v_ref, acc_ref, carry_ref, lb_ref, lk_ref, rs_ref):
    qcat = _stack_heads(q_ref[...])
    m = HEADS_PER_BLOCK * TQ
    urow = lax.broadcasted_iota(jnp.int32, (TK, TK), 0)
    ucol = lax.broadcasted_iota(jnp.int32, (TK, TK), 1)
    upper = (ucol < urow).astype(jnp.bfloat16)

    acc_ref[...] = jnp.zeros_like(acc_ref)
    carry_ref[...] = jnp.zeros_like(carry_ref)

    def score(kj, slot, diag):
        k = k_ref[pl.ds(kj * TK, TK), :]
        z = lax.dot_general(qcat, k, _NT, preferred_element_type=jnp.float32)
        t = jnp.log2(1.0 + jnp.exp2(-jnp.abs(z)))
        log_beta = jnp.minimum(z, 0.0) - t
        log_keep = log_beta - z
        if diag:
            row = lax.broadcasted_iota(jnp.int32, (m, TK), 0)
            col = lax.broadcasted_iota(jnp.int32, (m, TK), 1)
            below = col < (row % TQ)
            log_keep = jnp.where(below, log_keep, 0.0)
            log_beta = jnp.where(below, log_beta, -jnp.inf)
        lb_ref[slot] = log_beta
        lk_ref[slot] = log_keep.astype(jnp.bfloat16)
        rs_ref[slot] = jnp.broadcast_to(jnp.sum(log_keep, axis=-1, keepdims=True), (m, LANES))

    def finish(kj, slot):
        v = v_ref[pl.ds(kj * TK, TK), :]
        rest = jnp.dot(lk_ref[slot], upper, preferred_element_type=jnp.float32)
        carry = carry_ref[...]
        a = jnp.exp2(lb_ref[slot] + rest + _lane_tile(carry, TK))
        acc_ref[...] += jnp.dot(a.astype(v.dtype), v, preferred_element_type=jnp.float32)
        carry_ref[...] = carry + rs_ref[slot]

    def live(kj):
        return jnp.max(carry_ref[...]) > -SKIP_LOG2

    return _Chain(score, finish, live)


def _fox_chain(hb, qi, q_ref, k_ref, v_ref, c_ref, ct_ref, acc_ref, m_ref, l_ref, z_ref, rm_ref,
               kn_ref):
    qcat = _stack_heads(q_ref[...])
    lane = lax.broadcasted_iota(jnp.int32, (TQ, LANES), 1)
    cblk = c_ref[...]
    cq = jnp.concatenate(
        [jnp.broadcast_to(
            jnp.sum(jnp.where(lane == hb * HEADS_PER_BLOCK + h, cblk, 0.0), axis=-1, keepdims=True),
            (TQ, TK)) for h in range(HEADS_PER_BLOCK)], axis=0)
    m = HEADS_PER_BLOCK * TQ
    ones = jnp.ones((TK, LANES), jnp.bfloat16)

    @pl.when(qi == 0)
    def _():
        kf = k_ref[...].astype(jnp.float32)
        ksq = kf * kf
        klane = lax.broadcasted_iota(jnp.int32, ksq.shape, 1)
        for h in range(HEADS_PER_BLOCK):
            n2 = jnp.sum(jnp.where((klane // HEAD_DIM) == h, ksq, 0.0), axis=-1, keepdims=True)
            kn_ref[h * TQ:(h + 1) * TQ, :] = jnp.broadcast_to(
                jnp.sqrt(jnp.max(n2, axis=0, keepdims=True)), (TQ, LANES))

    qf = qcat.astype(jnp.float32)
    qn = jnp.broadcast_to(jnp.sqrt(jnp.sum(qf * qf, axis=-1, keepdims=True)), (m, LANES))

    acc_ref[...] = jnp.zeros_like(acc_ref)
    l_ref[...] = jnp.zeros_like(l_ref)
    m_ref[...] = jnp.full_like(m_ref, -jnp.inf)

    def score(kj, slot, diag):
        k = k_ref[pl.ds(kj * TK, TK), :]
        ck = jnp.concatenate(
            [jnp.broadcast_to(ct_ref[pl.ds(hb * HEADS_PER_BLOCK + h, 1), pl.ds(kj * TK, TK)], (TQ, TK))
             for h in range(HEADS_PER_BLOCK)], axis=0)
        z = lax.dot_general(qcat, k, _NT, preferred_element_type=jnp.float32)
        z = (z + cq) - ck
        if diag:
            row = lax.broadcasted_iota(jnp.int32, (m, TK), 0)
            col = lax.broadcasted_iota(jnp.int32, (m, TK), 1)
            z = jnp.where(col <= (row % TQ), z, -jnp.inf)
        z_ref[slot] = z
        rm_ref[slot] = jnp.broadcast_to(jnp.max(z, axis=-1, keepdims=True), (m, LANES))

    def finish(kj, slot):
        v = v_ref[pl.ds(kj * TK, TK), :]
        m_old = m_ref[...]
        m_new = jnp.maximum(m_old, rm_ref[slot])
        alpha = jnp.exp2(m_old - m_new)
        p = jnp.exp2(z_ref[slot] - _lane_tile(m_new, TK))
        pv = jnp.dot(p.astype(v.dtype), jnp.concatenate([v, ones], axis=1),
                     preferred_element_type=jnp.float32)
        acc_ref[...] = alpha * acc_ref[...] + pv[:, :LANES]
        l_ref[...] = alpha * l_ref[...] + pv[:, LANES:]
        m_ref[...] = m_new

    def live(kj):
        cb = jnp.concatenate(
            [jnp.broadcast_to(
                jnp.max(ct_ref[pl.ds(hb * HEADS_PER_BLOCK + h, 1), pl.ds(kj * TK, TK)],
                        axis=-1, keepdims=True), (TQ, LANES)) for h in range(HEADS_PER_BLOCK)], axis=0)
        bound = qn * kn_ref[...] + cq[:, :LANES] - cb - m_ref[...]
        return jnp.max(bound) > -SKIP_LOG2

    return _Chain(score, finish, live)


SB_HEAD_TILES = 2
FOX_HEAD_TILES = 4


def _attn_kernel(qs_ref, ks_ref, vs_ref, qf_ref, kf_ref, vf_ref, c_ref, ct_ref, osb_ref, ofx_ref,
                 sb_acc, sb_carry, sb_lb, sb_lk, sb_rs, fx_acc, fx_m, fx_l, fx_z, fx_rm, fx_kn):
    hb = pl.program_id(1)
    qi = pl.program_id(2)
    sb = _sb_chain(qs_ref, ks_ref, vs_ref, sb_acc, sb_carry, sb_lb, sb_lk, sb_rs)
    fx = _fox_chain(hb, qi, qf_ref, kf_ref, vf_ref, c_ref, ct_ref, fx_acc, fx_m, fx_l, fx_z, fx_rm, fx_kn)
    head = max(SB_HEAD_TILES, FOX_HEAD_TILES)

    @pl.when(qi < head)
    def _():
        _pipelined_tiles(qi, [sb, fx], True)

    @pl.when(qi >= head)
    def _():
        for i in range(head + 1):
            for ch, count in ((sb, SB_HEAD_TILES), (fx, FOX_HEAD_TILES)):
                if i < count:
                    ch.score(qi - i, i, i == 0)
                if 1 <= i <= count:
                    ch.finish(qi - i + 1, i - 1)
        for ch, count in ((sb, SB_HEAD_TILES), (fx, FOX_HEAD_TILES)):
            @pl.when(ch.live(qi - count + 1))
            def _():
                _pipelined_tiles(qi - count, [ch], False)

    osb_ref[...] = _unstack_heads(sb_acc[...]).astype(osb_ref.dtype)
    ofx_ref[...] = _unstack_heads(fx_acc[...] / fx_l[...]).astype(ofx_ref.dtype)


def _attention(proj3, c3, ct):
    b, s, _ = proj3.shape
    m = HEADS_PER_BLOCK * TQ

    def q_spec(blk):
        return pl.BlockSpec((None, TQ, LANES), lambda bi, hb, qi: (bi, qi, blk * N_HEAD_BLOCKS + hb))

    def kv_spec(blk):
        return pl.BlockSpec((None, s, LANES), lambda bi, hb, qi: (bi, 0, blk * N_HEAD_BLOCKS + hb))

    out_spec = pl.BlockSpec((None, TQ, LANES), lambda bi, hb, qi: (bi, qi, hb))
    out_shape = jax.ShapeDtypeStruct((b, s, ATTN_WIDTH), jnp.bfloat16)
    f32 = jnp.float32
    return pl.pallas_call(
        _attn_kernel,
        grid=(b, N_HEAD_BLOCKS, s // TQ),
        in_specs=[q_spec(0), kv_spec(1), kv_spec(2), q_spec(3), kv_spec(4), kv_spec(5),
                  pl.BlockSpec((None, TQ, LANES), lambda bi, hb, qi: (bi, qi, 0)),
                  pl.BlockSpec((N_HEADS, s), lambda bi, hb, qi: (0, bi))],
        out_specs=[out_spec, out_spec],
        out_shape=[out_shape, out_shape],
        scratch_shapes=[pltpu.VMEM((m, LANES), f32),
                        pltpu.VMEM((m, LANES), f32),
                        pltpu.VMEM((2, m, TK), f32),
                        pltpu.VMEM((2, m, TK), jnp.bfloat16),
                        pltpu.VMEM((2, m, LANES), f32),
                        pltpu.VMEM((m, LANES), f32),
                        pltpu.VMEM((m, LANES), f32),
                        pltpu.VMEM((m, LANES), f32),
                        pltpu.VMEM((FOX_HEAD_TILES, m, TK), f32),
                        pltpu.VMEM((FOX_HEAD_TILES, m, LANES), f32),
                        pltpu.VMEM((m, LANES), f32)],
        compiler_params=pltpu.CompilerParams(
            dimension_semantics=("arbitrary", "arbitrary", "arbitrary"),
            vmem_limit_bytes=VMEM_LIMIT),
        name="attn",
    )(proj3, proj3, proj3, proj3, proj3, proj3, c3, ct)


def _post_kernel(osb_ref, ofx_ref, gsb_ref, gfx_ref, x_ref, wosb_ref, wofx_ref, wout_ref,
                 gn_ref, wrh_ref, wrl_ref, br_ref,
                 x1_ref, h_ref, ids_ref, aux_ref):
    y_sb = jnp.dot(osb_ref[...], wosb_ref[...], preferred_element_type=jnp.float32)
    y_fx = jnp.dot(ofx_ref[...], wofx_ref[...], preferred_element_type=jnp.float32)
    mixed = (jax.nn.sigmoid(gsb_ref[...].astype(jnp.float32)) * y_sb
             + jax.nn.sigmoid(gfx_ref[...].astype(jnp.float32)) * y_fx)
    x1 = x_ref[...] + jnp.dot(mixed.astype(jnp.bfloat16), wout_ref[...],
                              preferred_element_type=jnp.float32)
    x1_ref[...] = x1
    y = _rmsnorm_f32(x1, gn_ref[...])
    h_hi, h_lo = _split_bf16(y, 2)
    h_ref[...] = h_hi

    lg = (lax.dot_general(wrh_ref[...], h_hi, _NT, preferred_element_type=jnp.float32)
          + lax.dot_general(wrh_ref[...], h_lo, _NT, preferred_element_type=jnp.float32)
          + lax.dot_general(wrl_ref[...], h_hi, _NT, preferred_element_type=jnp.float32))
    lg = lg + br_ref[...]
    tm = lg.shape[1]
    sub = lax.broadcasted_iota(jnp.int32, (8, tm), 0)
    neg = -jnp.inf

    gl = jnp.where(sub < N_GROUPS, lg[0:8], neg)
    gm = jnp.max(gl, axis=0, keepdims=True)
    g_w = 1.0 / jnp.sum(jnp.exp(gl - gm), axis=0, keepdims=True)
    g_idx = jnp.min(jnp.where(gl == gm, sub, 8), axis=0, keepdims=True)

    e_sel = lg[8:16]
    for g in range(1, N_GROUPS):
        e_sel = jnp.where(g_idx == g, lg[8 + 8 * g:16 + 8 * g], e_sel)
    m1 = jnp.max(e_sel, axis=0, keepdims=True)
    i1 = jnp.min(jnp.where(e_sel == m1, sub, 8), axis=0, keepdims=True)
    e_rest = jnp.where(sub == i1, neg, e_sel)
    m2 = jnp.max(e_rest, axis=0, keepdims=True)
    i2 = jnp.min(jnp.where(e_rest == m2, sub, 8), axis=0, keepdims=True)
    p2 = jnp.exp(m2 - m1)
    w1 = g_w / (1.0 + p2)
    w2 = g_w * p2 / (1.0 + p2)
    base = g_idx * EXPERTS_PER_GROUP
    ids_ref[...] = jnp.where(sub == 0, base + i1, jnp.where(sub == 1, base + i2, 0))
    dense_w = jnp.where(sub == i1, w1, jnp.where(sub == i2, w2, 0.0))
    group_hot = jnp.where(sub == g_idx, 1.0, 0.0)
    record = jnp.concatenate(
        [dense_w, group_hot, jnp.zeros((LANES - 16, tm), jnp.float32)], axis=0)
    aux_ref[...] = record.T


def _post_attention(o_sb, o_fx, proj, x2, wosb, wofx, wout, gn, wr_hi, wr_lo, br):
    n, d = x2.shape
    tm = TM_POST
    gate_blk = (3 * ATTN_WIDTH * 2) // d
    const = dict(pipeline_mode=pl.Buffered(1))
    return pl.pallas_call(
        _post_kernel,
        grid=(n // tm,),
        in_specs=[
            pl.BlockSpec((tm, ATTN_WIDTH), lambda i: (i, 0)),
            pl.BlockSpec((tm, ATTN_WIDTH), lambda i: (i, 0)),
            pl.BlockSpec((tm, d), lambda i: (i, gate_blk)),
            pl.BlockSpec((tm, d), lambda i: (i, gate_blk + 1)),
            pl.BlockSpec((tm, d), lambda i: (i, 0)),
            pl.BlockSpec((ATTN_WIDTH, d), lambda i: (0, 0), **const),
            pl.BlockSpec((ATTN_WIDTH, d), lambda i: (0, 0), **const),
            pl.BlockSpec((d, d), lambda i: (0, 0), **const),
            pl.BlockSpec((1, d), lambda i: (0, 0), **const),
            pl.BlockSpec((ROUTER_ROWS, d), lambda i: (0, 0), **const),
            pl.BlockSpec((ROUTER_ROWS, d), lambda i: (0, 0), **const),
            pl.BlockSpec((ROUTER_ROWS, 1), lambda i: (0, 0), **const),
        ],
        out_specs=[
            pl.BlockSpec((tm, d), lambda i: (i, 0)),
            pl.BlockSpec((tm, d), lambda i: (i, 0)),
            pl.BlockSpec((8, tm), lambda i: (0, i)),
            pl.BlockSpec((tm, LANES), lambda i: (i, 0)),
        ],
        out_shape=[
            jax.ShapeDtypeStruct((n, d), jnp.float32),
            jax.ShapeDtypeStruct((n, d), jnp.bfloat16),
            jax.ShapeDtypeStruct((8, n), jnp.int32),
            jax.ShapeDtypeStruct((n, LANES), jnp.float32),
        ],
        compiler_params=pltpu.CompilerParams(
            dimension_semantics=("arbitrary",), vmem_limit_bytes=VMEM_LIMIT),
        name="post_attn",
    )(o_sb, o_fx, proj, proj, x2, wosb, wofx, wout, gn, wr_hi, wr_lo, br)


TM_SORT = 256
T_GROUP = 256
ROW_ALIGN = 16
SORT_ROWS = TM_SORT + N_GROUPS * ROW_ALIGN
SMALL_CHUNK = TM_SORT // 2
GROUP_LANE0 = EXPERTS_PER_GROUP


def _max_group_tiles(n):
    padded = n + (ROW_ALIGN - 1) * N_GROUPS * (n // TM_SORT)
    return padded // T_GROUP + N_GROUPS * (TM_SORT // T_GROUP + 2)


def _sort_plan(token_group, n):
    i32 = jnp.int32
    tiles = n // TM_SORT
    hot = token_group.reshape(tiles, TM_SORT, 1) == jnp.arange(N_GROUPS, dtype=i32)
    padded = (jnp.sum(hot, axis=1, dtype=i32) + (ROW_ALIGN - 1)) // ROW_ALIGN * ROW_ALIGN
    block = jnp.cumsum(padded, axis=1) - padded
    total = jnp.sum(padded, axis=0)
    group_tiles = (total + (TM_SORT + T_GROUP - 1)) // T_GROUP
    region = group_tiles * T_GROUP
    base = jnp.cumsum(region) - region
    dest = base[None, :] + jnp.cumsum(padded, axis=0) - padded
    ends = jnp.cumsum(group_tiles)
    steps = jnp.arange(_max_group_tiles(n), dtype=i32)
    tile_group = jnp.minimum(jnp.sum(steps[:, None] >= ends[None, :], axis=1), N_GROUPS - 1)
    return dict(block=block.reshape(-1).astype(i32), dest=dest.reshape(-1).astype(i32),
                small=(padded <= SMALL_CHUNK).reshape(-1).astype(i32),
                zero_from=(base + total).astype(i32), zero_to=(base + region).astype(i32),
                used_tiles=ends[-1:].astype(i32), tile_group=tile_group.astype(i32))


def _sort_kernel(block_ref, dest_ref, small_ref, zfrom_ref, zto_ref, used_ref,
                 h_ref, ids_ref, aux_ref, hg_ref, wg_ref, hbuf, wbuf, sems):
    i = pl.program_id(0)
    tm = TM_SORT
    rows = SORT_ROWS
    slot = i % 2

    def copies(g, slot, src, off, nrows):
        return (pltpu.make_async_copy(hbuf.at[slot, pl.ds(src, nrows)], hg_ref.at[pl.ds(off, nrows)],
                                      sems.at[0, g]),
                pltpu.make_async_copy(wbuf.at[slot, pl.ds(src, nrows)], wg_ref.at[pl.ds(off, nrows)],
                                      sems.at[1, g]))

    def tile_chunks(step, slot, act):
        for g in range(N_GROUPS):
            idx = step * N_GROUPS + g
            src = pl.multiple_of(block_ref[idx], ROW_ALIGN)
            off = pl.multiple_of(dest_ref[idx], ROW_ALIGN)
            for nrows, is_small in ((SMALL_CHUNK, True), (tm, False)):
                @pl.when((small_ref[idx] != 0) == is_small)
                def _():
                    for c in copies(g, slot, src, off, nrows):
                        act(c)

    def zero_rows(off, g):
        for c in copies(g, 0, 0, pl.multiple_of(off, ROW_ALIGN), tm):
            c.start()
        for c in copies(g, 0, 0, pl.multiple_of(off, ROW_ALIGN), tm):
            c.wait()

    @pl.when(i == 0)
    def _():
        hbuf[...] = jnp.zeros_like(hbuf)
        wbuf[...] = jnp.zeros_like(wbuf)

    gid = ids_ref[0:1, :] // EXPERTS_PER_GROUP
    sub = lax.broadcasted_iota(jnp.int32, (8, tm), 0)
    hot = sub == gid
    r_io = lax.broadcasted_iota(jnp.int32, (tm, tm), 0)
    c_io = lax.broadcasted_iota(jnp.int32, (tm, tm), 1)
    before = (r_io < c_io).astype(jnp.bfloat16)
    seen = jnp.dot(jnp.where(hot, 1.0, 0.0).astype(jnp.bfloat16), before,
                   preferred_element_type=jnp.float32)
    pos = jnp.sum(jnp.where(hot, seen, 0.0), axis=0, keepdims=True).astype(jnp.int32)
    for g in range(N_GROUPS):
        pos = pos + jnp.where(gid == g, block_ref[i * N_GROUPS + g], 0)

    s_io = lax.broadcasted_iota(jnp.int32, (rows, tm), 0)
    sel = jnp.where(pos == s_io, 1.0, 0.0).astype(jnp.bfloat16)
    hbuf[slot, 0:rows] = jnp.dot(sel, h_ref[...], preferred_element_type=jnp.float32).astype(hbuf.dtype)
    aux3 = jnp.concatenate(_split_bf16(aux_ref[...], 3), axis=1)
    w3 = jnp.dot(sel, aux3, preferred_element_type=jnp.float32)
    wbuf[slot, 0:rows] = w3[:, :LANES] + w3[:, LANES:2 * LANES] + w3[:, 2 * LANES:]

    @pl.when(i > 0)
    def _():
        tile_chunks(i - 1, 1 - slot, lambda c: c.wait())

    tile_chunks(i, slot, lambda c: c.start())

    @pl.when(i == pl.num_programs(0) - 1)
    def _():
        tile_chunks(i, slot, lambda c: c.wait())
        hbuf[...] = jnp.zeros_like(hbuf)
        wbuf[...] = jnp.zeros_like(wbuf)
        for g in range(N_GROUPS):
            zero_rows(zfrom_ref[g], g)
            zero_rows(zto_ref[g] - tm, g)

        def tail(t, _):
            zero_rows(t * T_GROUP, 0)
            return 0

        lax.fori_loop(used_ref[0], hg_ref.shape[0] // T_GROUP, tail, 0)


def _moe_sort(plan, h, ids, aux):
    n, d = h.shape
    tm = TM_SORT
    rows = _max_group_tiles(n) * T_GROUP
    grid_spec = pltpu.PrefetchScalarGridSpec(
        num_scalar_prefetch=6,
        grid=(n // tm,),
        in_specs=[
            pl.BlockSpec((tm, d), lambda i, *_: (i, 0)),
            pl.BlockSpec((8, tm), lambda i, *_: (0, i)),
            pl.BlockSpec((tm, LANES), lambda i, *_: (i, 0)),
        ],
        out_specs=[pl.BlockSpec(memory_space=pl.ANY), pl.BlockSpec(memory_space=pl.ANY)],
        scratch_shapes=[pltpu.VMEM((2, SORT_ROWS + tm, d), jnp.bfloat16),
                        pltpu.VMEM((2, SORT_ROWS + tm, LANES), jnp.float32),
                        pltpu.SemaphoreType.DMA((2, N_GROUPS))],
    )
    return pl.pallas_call(
        _sort_kernel,
        grid_spec=grid_spec,
        out_shape=[jax.ShapeDtypeStruct((rows, d), jnp.bfloat16),
                   jax.ShapeDtypeStruct((rows, LANES), jnp.float32)],
        compiler_params=pltpu.CompilerParams(
            dimension_semantics=("arbitrary",), vmem_limit_bytes=VMEM_LIMIT),
        name="moe_sort",
    )(plan["block"], plan["dest"], plan["small"], plan["zero_from"], plan["zero_to"],
      plan["used_tiles"], h, ids, aux)


def _experts_kernel(tg_ref, used_ref, h_ref, w_ref, w1_ref, w3_ref, w2_ref, y_ref):
    i = pl.program_id(0)

    @pl.when(i < used_ref[0])
    def _():
        h = h_ref[...]
        w = w_ref[...]
        hidden = []
        for e in range(EXPERTS_PER_GROUP):
            a = jnp.dot(h, w1_ref[e], preferred_element_type=jnp.float32)
            b = jnp.dot(h, w3_ref[e], preferred_element_type=jnp.float32)
            hidden.append(((a * jax.nn.sigmoid(a)) * b * w[:, e:e + 1]).astype(jnp.bfloat16))
        ne, de, d = w2_ref.shape
        y = jnp.dot(jnp.concatenate(hidden, axis=1), w2_ref[...].reshape(ne * de, d),
                    preferred_element_type=jnp.float32)
        y_ref[...] = y.astype(y_ref.dtype)

    @pl.when(i >= used_ref[0])
    def _():
        y_ref[...] = jnp.zeros_like(y_ref)


def _moe_experts(plan, hg, wg, w1, w3, w2):
    rows, d = hg.shape
    t = T_GROUP
    de = w1.shape[-1]

    def row_tile(i, tg, used):
        return (jnp.minimum(i, used[0] - 1), 0)

    def group_weights(i, tg, used):
        return (tg[i], 0, 0)

    grid_spec = pltpu.PrefetchScalarGridSpec(
        num_scalar_prefetch=2,
        grid=(rows // t,),
        in_specs=[
            pl.BlockSpec((t, d), row_tile),
            pl.BlockSpec((t, LANES), row_tile),
            pl.BlockSpec((EXPERTS_PER_GROUP, d, de), group_weights),
            pl.BlockSpec((EXPERTS_PER_GROUP, d, de), group_weights),
            pl.BlockSpec((EXPERTS_PER_GROUP, de, d), group_weights),
        ],
        out_specs=pl.BlockSpec((t, d), lambda i, tg, used: (i, 0)),
    )
    return pl.pallas_call(
        _experts_kernel,
        grid_spec=grid_spec,
        out_shape=jax.ShapeDtypeStruct((rows, d), jnp.bfloat16),
        compiler_params=pltpu.CompilerParams(
            dimension_semantics=("arbitrary",), vmem_limit_bytes=VMEM_LIMIT),
        name="moe_experts",
    )(plan["tile_group"], plan["used_tiles"], hg, wg, w1, w3, w2)


def _unsort_kernel(dest_ref, small_ref, x1_ref, aux_ref, gn_ref, yg_ref, o_ref, ybuf, yacc, sems):
    i = pl.program_id(0)
    tm = TM_SORT
    slot = i % 2

    def chunks(step, slot, act):
        for g in range(N_GROUPS):
            idx = step * N_GROUPS + g
            off = pl.multiple_of(dest_ref[idx], ROW_ALIGN)
            for nrows, is_small in ((SMALL_CHUNK, True), (tm, False)):
                @pl.when((small_ref[idx] != 0) == is_small)
                def _():
                    act(pltpu.make_async_copy(yg_ref.at[pl.ds(off, nrows)],
                                              ybuf.at[slot, g, pl.ds(0, nrows)], sems.at[slot, g]))

    @pl.when(i == 0)
    def _():
        chunks(0, 0, lambda c: c.start())

    @pl.when(i + 1 < pl.num_programs(0))
    def _():
        chunks(i + 1, 1 - slot, lambda c: c.start())

    chunks(i, slot, lambda c: c.wait())

    aux = aux_ref[...]
    lane = lax.broadcasted_iota(jnp.int32, (tm, LANES), 1)
    is_group_lane = jnp.logical_and(lane >= GROUP_LANE0, lane < GROUP_LANE0 + N_GROUPS)
    hot = jnp.where(is_group_lane, aux, 0.0)
    r_io = lax.broadcasted_iota(jnp.int32, (tm, tm), 0)
    c_io = lax.broadcasted_iota(jnp.int32, (tm, tm), 1)
    before = (c_io < r_io).astype(jnp.bfloat16)
    seen = jnp.dot(before, hot.astype(jnp.bfloat16), preferred_element_type=jnp.float32)
    rank = jnp.sum(hot * seen, axis=-1, keepdims=True).astype(jnp.int32)
    yacc[...] = x1_ref[...]
    for g in range(N_GROUPS):
        mine = aux[:, GROUP_LANE0 + g:GROUP_LANE0 + g + 1] > 0.5
        sel = jnp.where(jnp.logical_and(rank == c_io, mine), 1.0, 0.0).astype(jnp.bfloat16)
        for nrows, is_small in ((SMALL_CHUNK, True), (tm, False)):
            @pl.when((small_ref[i * N_GROUPS + g] != 0) == is_small)
            def _():
                yacc[...] += jnp.dot(sel[:, :nrows], ybuf[slot, g, 0:nrows],
                                     preferred_element_type=jnp.float32)
    o_ref[...] = _rmsnorm_f32(yacc[...], gn_ref[...])


def _moe_unsort(plan, x1, aux, gn, yg):
    n, d = x1.shape
    tm = TM_SORT
    grid_spec = pltpu.PrefetchScalarGridSpec(
        num_scalar_prefetch=2,
        grid=(n // tm,),
        in_specs=[
            pl.BlockSpec((tm, d), lambda i, *_: (i, 0)),
            pl.BlockSpec((tm, LANES), lambda i, *_: (i, 0)),
            pl.BlockSpec((1, d), lambda i, *_: (0, 0)),
            pl.BlockSpec(memory_space=pl.ANY),
        ],
        out_specs=pl.BlockSpec((tm, d), lambda i, *_: (i, 0)),
        scratch_shapes=[pltpu.VMEM((2, N_GROUPS, tm, d), jnp.bfloat16),
                        pltpu.VMEM((tm, d), jnp.float32),
                        pltpu.SemaphoreType.DMA((2, N_GROUPS))],
    )
    return pl.pallas_call(
        _unsort_kernel,
        grid_spec=grid_spec,
        out_shape=jax.ShapeDtypeStruct((n, d), jnp.float32),
        compiler_params=pltpu.CompilerParams(
            dimension_semantics=("arbitrary",), vmem_limit_bytes=VMEM_LIMIT),
        name="moe_unsort",
    )(plan["dest"], plan["small"], x1, aux, gn, yg)


def kernel(x, norm_attn, w_in, b_forget, w_o_sb, w_o_fox, w_out, norm_ffn, w_router_group,
           b_router_group, w_router_expert, b_router_expert, w1, w3, w2, norm_final):
    b, s, d = x.shape
    n = b * s
    depth = w_in.shape[0]
    assert depth == 1, "the final norm is fused into the MoE un-sort of a single layer"
    assert s % TM_PROJ == 0 and s % TQ == 0 and n % TM_SORT == 0 and T_GROUP == TM_SORT
    bf16 = jnp.bfloat16
    n_main = 6 * ATTN_WIDTH
    x2 = x.reshape(n, d)
    for l in range(depth):
        w_l = w_in[l]
        w_main = jnp.concatenate([w_l[:, :n_main], w_l[:, n_main + N_HEADS:]], axis=1).astype(bf16)
        wf = jnp.pad(w_l[:, n_main:n_main + N_HEADS], ((0, 0), (0, LANES - N_HEADS)))
        wf_hi = wf.astype(bf16)
        wf_lo = (wf - wf_hi.astype(jnp.float32)).astype(bf16)
        bf = jnp.pad(b_forget[l], (0, LANES - N_HEADS)).reshape(1, LANES)
        wr = jnp.concatenate([w_router_group[l].T, jnp.zeros((8 - N_GROUPS, d), jnp.float32),
                              w_router_expert[l].T], axis=0)
        wr_hi = wr.astype(bf16)
        wr_lo = (wr - wr_hi.astype(jnp.float32)).astype(bf16)
        br = jnp.concatenate([b_router_group[l], jnp.zeros((8 - N_GROUPS,), jnp.float32),
                              b_router_expert[l]]).reshape(ROUTER_ROWS, 1)

        proj, c, ct = _inproj(x2, norm_attn[l].reshape(1, d), w_main, wf_hi, wf_lo, bf, s)
        proj3 = proj.reshape(b, s, proj.shape[1])
        o_sb, o_fx = _attention(proj3, c.reshape(b, s, LANES), ct)
        x1, h, ids, aux = _post_attention(
            o_sb.reshape(n, ATTN_WIDTH), o_fx.reshape(n, ATTN_WIDTH), proj, x2,
            w_o_sb[l].astype(bf16), w_o_fox[l].astype(bf16), w_out[l].astype(bf16),
            norm_ffn[l].reshape(1, d), wr_hi, wr_lo, br)
        plan = _sort_plan(ids[0] // EXPERTS_PER_GROUP, n)
        hg, wg = _moe_sort(plan, h, ids, aux)
        yg = _moe_experts(plan, hg, wg, w1[l].astype(bf16), w3[l].astype(bf16), w2[l].astype(bf16))
        x2 = _moe_unsort(plan, x1, aux, norm_final.reshape(1, d), yg)
    return x2.reshape(b, s, d)
```

```python
import functools
import math
from typing import Callable, NamedTuple

import jax
import jax.numpy as jnp
from jax import lax
from jax.experimental import pallas as pl
from jax.experimental.pallas import tpu as pltpu

HEAD_DIM = 64
N_HEADS = 8
ATTN_WIDTH = N_HEADS * HEAD_DIM
N_GROUPS = 4
EXPERTS_PER_GROUP = 8
N_EXPERTS = N_GROUPS * EXPERTS_PER_GROUP
RMS_EPS = 1e-6
LANES = 128
HEADS_PER_BLOCK = LANES // HEAD_DIM
N_HEAD_BLOCKS = N_HEADS // HEADS_PER_BLOCK
ROUTER_ROWS = 8 + N_EXPERTS
VMEM_LIMIT = 56 * 1024 * 1024

TM_PROJ = 512
TQ = 256
TK = 256
TM_POST = 512

_NT = (((1,), (1,)), ((), ()))
LOG2E = math.log2(math.e)
Q_SCALE = LOG2E / math.sqrt(HEAD_DIM)
SKIP_LOG2 = 160.0


def _split_bf16(v, parts):
    out = []
    r = v
    for i in range(parts):
        p = r.astype(jnp.bfloat16)
        out.append(p)
        if i + 1 < parts:
            r = r - p.astype(jnp.float32)
    return out


def _rmsnorm_f32(x, g):
    ms = jnp.mean(x * x, axis=-1, keepdims=True)
    return x * lax.rsqrt(ms + RMS_EPS) * g


def _inproj_kernel(x_ref, g_ref, w_ref, wfh_ref, wfl_ref, bf_ref,
                   proj_ref, c_ref, ct_ref, carry_ref, *, tiles_per_seq, n_chunks, chunk, q_chunks):
    i = pl.program_id(0)

    @pl.when(i % tiles_per_seq == 0)
    def _():
        carry_ref[...] = jnp.zeros_like(carry_ref)

    y = _rmsnorm_f32(x_ref[...], g_ref[...])
    h_hi, h_lo = _split_bf16(y, 2)
    for c in range(n_chunks):
        sl = slice(c * chunk, (c + 1) * chunk)
        p = jnp.dot(h_hi, w_ref[:, sl], preferred_element_type=jnp.float32)
        if c in q_chunks:
            p = p * Q_SCALE
        proj_ref[:, sl] = p.astype(proj_ref.dtype)

    f = (jnp.dot(h_hi, wfh_ref[...], preferred_element_type=jnp.float32)
         + jnp.dot(h_lo, wfh_ref[...], preferred_element_type=jnp.float32)
         + jnp.dot(h_hi, wfl_ref[...], preferred_element_type=jnp.float32))
    f = f + bf_ref[...]
    logf = (jnp.minimum(f, 0.0) - jnp.log1p(jnp.exp(-jnp.abs(f)))) * LOG2E

    tm = logf.shape[0]
    row = lax.broadcasted_iota(jnp.int32, (tm, tm), 0)
    col = lax.broadcasted_iota(jnp.int32, (tm, tm), 1)
    tri = (col <= row).astype(jnp.bfloat16)
    cum = carry_ref[...]
    for p in _split_bf16(logf, 3):
        cum = cum + jnp.dot(tri, p, preferred_element_type=jnp.float32)
    c_ref[...] = cum
    ct_ref[...] = cum.T[:N_HEADS, :]
    carry_ref[...] = cum[tm - 1:tm, :]


def _inproj(x2, g, w_main, wf_hi, wf_lo, bf, seq):
    n, d = x2.shape
    cols = w_main.shape[1]
    tm = TM_PROJ
    chunk = ATTN_WIDTH
    kern = functools.partial(_inproj_kernel, tiles_per_seq=seq // tm,
                             n_chunks=cols // chunk, chunk=chunk, q_chunks=(0, 3))
    const = dict(pipeline_mode=pl.Buffered(1))
    return pl.pallas_call(
        kern,
        grid=(n // tm,),
        in_specs=[
            pl.BlockSpec((tm, d), lambda i: (i, 0)),
            pl.BlockSpec((1, d), lambda i: (0, 0), **const),
            pl.BlockSpec((d, cols), lambda i: (0, 0), **const),
            pl.BlockSpec((d, LANES), lambda i: (0, 0), **const),
            pl.BlockSpec((d, LANES), lambda i: (0, 0), **const),
            pl.BlockSpec((1, LANES), lambda i: (0, 0), **const),
        ],
        out_specs=[
            pl.BlockSpec((tm, cols), lambda i: (i, 0)),
            pl.BlockSpec((tm, LANES), lambda i: (i, 0)),
            pl.BlockSpec((N_HEADS, tm), lambda i: (0, i)),
        ],
        out_shape=[
            jax.ShapeDtypeStruct((n, cols), jnp.bfloat16),
            jax.ShapeDtypeStruct((n, LANES), jnp.float32),
            jax.ShapeDtypeStruct((N_HEADS, n), jnp.float32),
        ],
        scratch_shapes=[pltpu.VMEM((1, LANES), jnp.float32)],
        compiler_params=pltpu.CompilerParams(
            dimension_semantics=("arbitrary",), vmem_limit_bytes=VMEM_LIMIT),
        name="inproj",
    )(x2, g, w_main, wf_hi, wf_lo, bf)


def _stack_heads(q):
    lane = lax.broadcasted_iota(jnp.int32, q.shape, 1)
    return jnp.concatenate(
        [jnp.where((lane // HEAD_DIM) == h, q, jnp.zeros_like(q)) for h in range(HEADS_PER_BLOCK)], axis=0)


def _unstack_heads(acc):
    tq = acc.shape[0] // HEADS_PER_BLOCK
    lane = lax.broadcasted_iota(jnp.int32, (tq, LANES), 1)
    out = acc[:tq]
    for h in range(1, HEADS_PER_BLOCK):
        out = jnp.where((lane // HEAD_DIM) == h, acc[h * tq:(h + 1) * tq], out)
    return out


def _lane_tile(x, width):
    return jnp.concatenate([x] * (width // LANES), axis=1)


class _Chain(NamedTuple):
    score: Callable
    finish: Callable
    live: Callable


def _pipelined_tiles(k0, chains, diag):
    for ch in chains:
        ch.score(k0, 0, diag)

    def overlap(ch, kj, slot):
        ch.score(kj - 1, 1 - slot, False)
        ch.finish(kj, slot)

    def cond(state):
        p, gos = state[0], state[1:]
        return jnp.logical_and(p < k0 // 2, functools.reduce(jnp.logical_or, [g != 0 for g in gos]))

    def pair(state):
        p, gos = state[0], state[1:]
        kj = k0 - 2 * p
        for mask in range(1, 2 ** len(chains)):
            active = [c for i, c in enumerate(chains) if mask >> i & 1]
            preds = [(gos[i] != 0) if mask >> i & 1 else (gos[i] == 0) for i in range(len(chains))]

            @pl.when(functools.reduce(jnp.logical_and, preds))
            def _():
                for slot in (0, 1):
                    for ch in active:
                        overlap(ch, kj - slot, slot)

        new = [jnp.where(jnp.logical_and(g != 0, ch.live(kj - 1)), 1, 0).astype(jnp.int32)
               for g, ch in zip(gos, chains)]
        return (p + 1, *new)

    state = lax.while_loop(cond, pair, (jnp.int32(0),) + (jnp.int32(1),) * len(chains))

    for go, ch in zip(state[1:], chains):
        @pl.when(jnp.logical_and(go != 0, k0 % 2 == 1))
        def _():
            overlap(ch, 1, 0)
            ch.finish(0, 1)

        @pl.when(jnp.logical_and(go != 0, k0 % 2 == 0))
        def _():
            ch.finish(0, 0)


def _sb_chain(q_ref, k_ref, v_ref, acc_ref, carry_ref, lb_ref, lk_ref, rs_ref):
    qcat = _stack_heads(q_ref[...])
    m = HEADS_PER_BLOCK * TQ
    urow = lax.broadcasted_iota(jnp.int32, (TK, TK), 0)
    ucol = lax.broadcasted_iota(jnp.int32, (TK, TK), 1)
    upper = (ucol < urow).astype(jnp.bfloat16)

    acc_ref[...] = jnp.zeros_like(acc_ref)
    carry_ref[...] = jnp.zeros_like(carry_ref)

    def score(kj, slot, diag):
        k = k_ref[pl.ds(kj * TK, TK), :]
        z = lax.dot_general(qcat, k, _NT, preferred_element_type=jnp.float32)
        t = jnp.log2(1.0 + jnp.exp2(-jnp.abs(z)))
        log_beta = jnp.minimum(z, 0.0) - t
        log_keep = log_beta - z
        if diag:
            row = lax.broadcasted_iota(jnp.int32, (m, TK), 0)
            col = lax.broadcasted_iota(jnp.int32, (m, TK), 1)
            below = col < (row % TQ)
            log_keep = jnp.where(below, log_keep, 0.0)
            log_beta = jnp.where(below, log_beta, -jnp.inf)
        lb_ref[slot] = log_beta
        lk_ref[slot] = log_keep.astype(jnp.bfloat16)
        rs_ref[slot] = jnp.broadcast_to(jnp.sum(log_keep, axis=-1, keepdims=True), (m, LANES))

    def finish(kj, slot):
        v = v_ref[pl.ds(kj * TK, TK), :]
        rest = jnp.dot(lk_ref[slot], upper, preferred_element_type=jnp.float32)
        carry = carry_ref[...]
        a = jnp.exp2(lb_ref[slot] + rest + _lane_tile(carry, TK))
        acc_ref[...] += jnp.dot(a.astype(v.dtype), v, preferred_element_type=jnp.float32)
        carry_ref[...] = carry + rs_ref[slot]

    def live(kj):
        return jnp.max(carry_ref[...]) > -SKIP_LOG2

    return _Chain(score, finish, live)


def _fox_chain(hb, qi, q_ref, k_ref, v_ref, c_ref, ct_ref, acc_ref, m_ref, l_ref, z_ref, rm_ref,
               kn_ref):
    qcat = _stack_heads(q_ref[...])
    lane = lax.broadcasted_iota(jnp.int32, (TQ, LANES), 1)
    cblk = c_ref[...]
    cq = jnp.concatenate(
        [jnp.broadcast_to(
            jnp.sum(jnp.where(lane == hb * HEADS_PER_BLOCK + h, cblk, 0.0), axis=-1, keepdims=True),
            (TQ, TK)) for h in range(HEADS_PER_BLOCK)], axis=0)
    m = HEADS_PER_BLOCK * TQ
    ones = jnp.ones((TK, LANES), jnp.bfloat16)

    @pl.when(qi == 0)
    def _():
        kf = k_ref[...].astype(jnp.float32)
        ksq = kf * kf
        klane = lax.broadcasted_iota(jnp.int32, ksq.shape, 1)
        for h in range(HEADS_PER_BLOCK):
            n2 = jnp.sum(jnp.where((klane // HEAD_DIM) == h, ksq, 0.0), axis=-1, keepdims=True)
            kn_ref[h * TQ:(h + 1) * TQ, :] = jnp.broadcast_to(
                jnp.sqrt(jnp.max(n2, axis=0, keepdims=True)), (TQ, LANES))

    qf = qcat.astype(jnp.float32)
    qn = jnp.broadcast_to(jnp.sqrt(jnp.sum(qf * qf, axis=-1, keepdims=True)), (m, LANES))

    acc_ref[...] = jnp.zeros_like(acc_ref)
    l_ref[...] = jnp.zeros_like(l_ref)
    m_ref[...] = jnp.full_like(m_ref, -jnp.inf)

    def score(kj, slot, diag):
        k = k_ref[pl.ds(kj * TK, TK), :]
        ck = jnp.concatenate(
            [jnp.broadcast_to(ct_ref[pl.ds(hb * HEADS_PER_BLOCK + h, 1), pl.ds(kj * TK, TK)], (TQ, TK))
             for h in range(HEADS_PER_BLOCK)], axis=0)
        z = lax.dot_general(qcat, k, _NT, preferred_element_type=jnp.float32)
        z = (z + cq) - ck
        if diag:
            row = lax.broadcasted_iota(jnp.int32, (m, TK), 0)
            col = lax.broadcasted_iota(jnp.int32, (m, TK), 1)
            z = jnp.where(col <= (row % TQ), z, -jnp.inf)
        z_ref[slot] = z
        rm_ref[slot] = jnp.broadcast_to(jnp.max(z, axis=-1, keepdims=True), (m, LANES))

    def finish(kj, slot):
        v = v_ref[pl.ds(kj * TK, TK), :]
        m_old = m_ref[...]
        m_new = jnp.maximum(m_old, rm_ref[slot])
        alpha = jnp.exp2(m_old - m_new)
        p = jnp.exp2(z_ref[slot] - _lane_tile(m_new, TK))
        pv = jnp.dot(p.astype(v.dtype), jnp.concatenate([v, ones], axis=1),
                     preferred_element_type=jnp.float32)
        acc_ref[...] = alpha * acc_ref[...] + pv[:, :LANES]
        l_ref[...] = alpha * l_ref[...] + pv[:, LANES:]
        m_ref[...] = m_new

    def live(kj):
        cb = jnp.concatenate(
            [jnp.broadcast_to(
                jnp.max(ct_ref[pl.ds(hb * HEADS_PER_BLOCK + h, 1), pl.ds(kj * TK, TK)],
                        axis=-1, keepdims=True), (TQ, LANES)) for h in range(HEADS_PER_BLOCK)], axis=0)
        bound = qn * kn_ref[...] + cq[:, :LANES] - cb - m_ref[...]
        return jnp.max(bound) > -SKIP_LOG2

    return _Chain(score, finish, live)


SB_HEAD_TILES = 2
FOX_HEAD_TILES = 4


def _attn_kernel(qs_ref, ks_ref, vs_ref, qf_ref, kf_ref, vf_ref, c_ref, ct_ref, osb_ref, ofx_ref,
                 sb_acc, sb_carry, sb_lb, sb_lk, sb_rs, fx_acc, fx_m, fx_l, fx_z, fx_rm, fx_kn):
    hb = pl.program_id(1)
    qi = pl.program_id(2)
    sb = _sb_chain(qs_ref, ks_ref, vs_ref, sb_acc, sb_carry, sb_lb, sb_lk, sb_rs)
    fx = _fox_chain(hb, qi, qf_ref, kf_ref, vf_ref, c_ref, ct_ref, fx_acc, fx_m, fx_l, fx_z, fx_rm, fx_kn)
    head = max(SB_HEAD_TILES, FOX_HEAD_TILES)

    @pl.when(qi < head)
    def _():
        _pipelined_tiles(qi, [sb, fx], True)

    @pl.when(qi >= head)
    def _():
        for i in range(head + 1):
            for ch, count in ((sb, SB_HEAD_TILES), (fx, FOX_HEAD_TILES)):
                if i < count:
                    ch.score(qi - i, i, i == 0)
                if 1 <= i <= count:
                    ch.finish(qi - i + 1, i - 1)
        for ch, count in ((sb, SB_HEAD_TILES), (fx, FOX_HEAD_TILES)):
            @pl.when(ch.live(qi - count + 1))
            def _():
                _pipelined_tiles(qi - count, [ch], False)

    osb_ref[...] = _unstack_heads(sb_acc[...]).astype(osb_ref.dtype)
    ofx_ref[...] = _unstack_heads(fx_acc[...] / fx_l[...]).astype(ofx_ref.dtype)


def _attention(proj3, c3, ct):
    b, s, _ = proj3.shape
    m = HEADS_PER_BLOCK * TQ

    def q_spec(blk):
        return pl.BlockSpec((None, TQ, LANES), lambda bi, hb, qi: (bi, qi, blk * N_HEAD_BLOCKS + hb))

    def kv_spec(blk):
        return pl.BlockSpec((None, s, LANES), lambda bi, hb, qi: (bi, 0, blk * N_HEAD_BLOCKS + hb))

    out_spec = pl.BlockSpec((None, TQ, LANES), lambda bi, hb, qi: (bi, qi, hb))
    out_shape = jax.ShapeDtypeStruct((b, s, ATTN_WIDTH), jnp.bfloat16)
    f32 = jnp.float32
    return pl.pallas_call(
        _attn_kernel,
        grid=(b, N_HEAD_BLOCKS, s // TQ),
        in_specs=[q_spec(0), kv_spec(1), kv_spec(2), q_spec(3), kv_spec(4), kv_spec(5),
                  pl.BlockSpec((None, TQ, LANES), lambda bi, hb, qi: (bi, qi, 0)),
                  pl.BlockSpec((N_HEADS, s), lambda bi, hb, qi: (0, bi))],
        out_specs=[out_spec, out_spec],
        out_shape=[out_shape, out_shape],
        scratch_shapes=[pltpu.VMEM((m, LANES), f32),
                        pltpu.VMEM((m, LANES), f32),
                        pltpu.VMEM((2, m, TK), f32),
                        pltpu.VMEM((2, m, TK), jnp.bfloat16),
                        pltpu.VMEM((2, m, LANES), f32),
                        pltpu.VMEM((m, LANES), f32),
                        pltpu.VMEM((m, LANES), f32),
                        pltpu.VMEM((m, LANES), f32),
                        pltpu.VMEM((FOX_HEAD_TILES, m, TK), f32),
                        pltpu.VMEM((FOX_HEAD_TILES, m, LANES), f32),
                        pltpu.VMEM((m, LANES), f32)],
        compiler_params=pltpu.CompilerParams(
            dimension_semantics=("arbitrary", "arbitrary", "arbitrary"),
            vmem_limit_bytes=VMEM_LIMIT),
        name="attn",
    )(proj3, proj3, proj3, proj3, proj3, proj3, c3, ct)


def _post_kernel(osb_ref, ofx_ref, gsb_ref, gfx_ref, x_ref, wosb_ref, wofx_ref, wout_ref,
                 gn_ref, wrh_ref, wrl_ref, br_ref,
                 x1_ref, h_ref, ids_ref, aux_ref):
    y_sb = jnp.dot(osb_ref[...], wosb_ref[...], preferred_element_type=jnp.float32)
    y_fx = jnp.dot(ofx_ref[...], wofx_ref[...], preferred_element_type=jnp.float32)
    mixed = (jax.nn.sigmoid(gsb_ref[...].astype(jnp.float32)) * y_sb
             + jax.nn.sigmoid(gfx_ref[...].astype(jnp.float32)) * y_fx)
    x1 = x_ref[...] + jnp.dot(mixed.astype(jnp.bfloat16), wout_ref[...],
                              preferred_element_type=jnp.float32)
    x1_ref[...] = x1
    y = _rmsnorm_f32(x1, gn_ref[...])
    h_hi, h_lo = _split_bf16(y, 2)
    h_ref[...] = h_hi

    lg = (lax.dot_general(wrh_ref[...], h_hi, _NT, preferred_element_type=jnp.float32)
          + lax.dot_general(wrh_ref[...], h_lo, _NT, preferred_element_type=jnp.float32)
          + lax.dot_general(wrl_ref[...], h_hi, _NT, preferred_element_type=jnp.float32))
    lg = lg + br_ref[...]
    tm = lg.shape[1]
    sub = lax.broadcasted_iota(jnp.int32, (8, tm), 0)
    neg = -jnp.inf

    gl = jnp.where(sub < N_GROUPS, lg[0:8], neg)
    gm = jnp.max(gl, axis=0, keepdims=True)
    g_w = 1.0 / jnp.sum(jnp.exp(gl - gm), axis=0, keepdims=True)
    g_idx = jnp.min(jnp.where(gl == gm, sub, 8), axis=0, keepdims=True)

    e_sel = lg[8:16]
    for g in range(1, N_GROUPS):
        e_sel = jnp.where(g_idx == g, lg[8 + 8 * g:16 + 8 * g], e_sel)
    m1 = jnp.max(e_sel, axis=0, keepdims=True)
    i1 = jnp.min(jnp.where(e_sel == m1, sub, 8), axis=0, keepdims=True)
    e_rest = jnp.where(sub == i1, neg, e_sel)
    m2 = jnp.max(e_rest, axis=0, keepdims=True)
    i2 = jnp.min(jnp.where(e_rest == m2, sub, 8), axis=0, keepdims=True)
    p2 = jnp.exp(m2 - m1)
    w1 = g_w / (1.0 + p2)
    w2 = g_w * p2 / (1.0 + p2)
    base = g_idx * EXPERTS_PER_GROUP
    ids_ref[...] = jnp.where(sub == 0, base + i1, jnp.where(sub == 1, base + i2, 0))
    dense_w = jnp.where(sub == i1, w1, jnp.where(sub == i2, w2, 0.0))
    group_hot = jnp.where(sub == g_idx, 1.0, 0.0)
    record = jnp.concatenate(
        [dense_w, group_hot, jnp.zeros((LANES - 16, tm), jnp.float32)], axis=0)
    aux_ref[...] = record.T


def _post_attention(o_sb, o_fx, proj, x2, wosb, wofx, wout, gn, wr_hi, wr_lo, br):
    n, d = x2.shape
    tm = TM_POST
    gate_blk = (3 * ATTN_WIDTH * 2) // d
    const = dict(pipeline_mode=pl.Buffered(1))
    return pl.pallas_call(
        _post_kernel,
        grid=(n // tm,),
        in_specs=[
            pl.BlockSpec((tm, ATTN_WIDTH), lambda i: (i, 0)),
            pl.BlockSpec((tm, ATTN_WIDTH), lambda i: (i, 0)),
            pl.BlockSpec((tm, d), lambda i: (i, gate_blk)),
            pl.BlockSpec((tm, d), lambda i: (i, gate_blk + 1)),
            pl.BlockSpec((tm, d), lambda i: (i, 0)),
            pl.BlockSpec((ATTN_WIDTH, d), lambda i: (0, 0), **const),
            pl.BlockSpec((ATTN_WIDTH, d), lambda i: (0, 0), **const),
            pl.BlockSpec((d, d), lambda i: (0, 0), **const),
            pl.BlockSpec((1, d), lambda i: (0, 0), **const),
            pl.BlockSpec((ROUTER_ROWS, d), lambda i: (0, 0), **const),
            pl.BlockSpec((ROUTER_ROWS, d), lambda i: (0, 0), **const),
            pl.BlockSpec((ROUTER_ROWS, 1), lambda i: (0, 0), **const),
        ],
        out_specs=[
            pl.BlockSpec((tm, d), lambda i: (i, 0)),
            pl.BlockSpec((tm, d), lambda i: (i, 0)),
            pl.BlockSpec((8, tm), lambda i: (0, i)),
            pl.BlockSpec((tm, LANES), lambda i: (i, 0)),
        ],
        out_shape=[
            jax.ShapeDtypeStruct((n, d), jnp.float32),
            jax.ShapeDtypeStruct((n, d), jnp.bfloat16),
            jax.ShapeDtypeStruct((8, n), jnp.int32),
            jax.ShapeDtypeStruct((n, LANES), jnp.float32),
        ],
        compiler_params=pltpu.CompilerParams(
            dimension_semantics=("arbitrary",), vmem_limit_bytes=VMEM_LIMIT),
        name="post_attn",
    )(o_sb, o_fx, proj, proj, x2, wosb, wofx, wout, gn, wr_hi, wr_lo, br)


TM_SORT = 256
T_GROUP = 256
ROW_ALIGN = 16
SORT_ROWS = TM_SORT + N_GROUPS * ROW_ALIGN
SMALL_CHUNK = TM_SORT // 2
GROUP_LANE0 = EXPERTS_PER_GROUP


def _max_group_tiles(n):
    padded = n + (ROW_ALIGN - 1) * N_GROUPS * (n // TM_SORT)
    return padded // T_GROUP + N_GROUPS * (TM_SORT // T_GROUP + 2)


def _sort_plan(token_group, n):
    i32 = jnp.int32
    tiles = n // TM_SORT
    hot = token_group.reshape(tiles, TM_SORT, 1) == jnp.arange(N_GROUPS, dtype=i32)
    padded = (jnp.sum(hot, axis=1, dtype=i32) + (ROW_ALIGN - 1)) // ROW_ALIGN * ROW_ALIGN
    block = jnp.cumsum(padded, axis=1) - padded
    total = jnp.sum(padded, axis=0)
    group_tiles = (total + (TM_SORT + T_GROUP - 1)) // T_GROUP
    region = group_tiles * T_GROUP
    base = jnp.cumsum(region) - region
    dest = base[None, :] + jnp.cumsum(padded, axis=0) - padded
    ends = jnp.cumsum(group_tiles)
    steps = jnp.arange(_max_group_tiles(n), dtype=i32)
    tile_group = jnp.minimum(jnp.sum(steps[:, None] >= ends[None, :], axis=1), N_GROUPS - 1)
    return dict(block=block.reshape(-1).astype(i32), dest=dest.reshape(-1).astype(i32),
                small=(padded <= SMALL_CHUNK).reshape(-1).astype(i32),
                zero_from=(base + total).astype(i32), zero_to=(base + region).astype(i32),
                used_tiles=ends[-1:].astype(i32), tile_group=tile_group.astype(i32))


def _sort_kernel(block_ref, dest_ref, small_ref, zfrom_ref, zto_ref, used_ref,
                 h_ref, ids_ref, aux_ref, hg_ref, wg_ref, hbuf, wbuf, sems):
    i = pl.program_id(0)
    tm = TM_SORT
    rows = SORT_ROWS
    slot = i % 2

    def copies(g, slot, src, off, nrows):
        return (pltpu.make_async_copy(hbuf.at[slot, pl.ds(src, nrows)], hg_ref.at[pl.ds(off, nrows)],
                                      sems.at[0, g]),
                pltpu.make_async_copy(wbuf.at[slot, pl.ds(src, nrows)], wg_ref.at[pl.ds(off, nrows)],
                                      sems.at[1, g]))

    def tile_chunks(step, slot, act):
        for g in range(N_GROUPS):
            idx = step * N_GROUPS + g
            src = pl.multiple_of(block_ref[idx], ROW_ALIGN)
            off = pl.multiple_of(dest_ref[idx], ROW_ALIGN)
            for nrows, is_small in ((SMALL_CHUNK, True), (tm, False)):
                @pl.when((small_ref[idx] != 0) == is_small)
                def _():
                    for c in copies(g, slot, src, off, nrows):
                        act(c)

    def zero_rows(off, g):
        for c in copies(g, 0, 0, pl.multiple_of(off, ROW_ALIGN), tm):
            c.start()
        for c in copies(g, 0, 0, pl.multiple_of(off, ROW_ALIGN), tm):
            c.wait()

    @pl.when(i == 0)
    def _():
        hbuf[...] = jnp.zeros_like(hbuf)
        wbuf[...] = jnp.zeros_like(wbuf)

    gid = ids_ref[0:1, :] // EXPERTS_PER_GROUP
    sub = lax.broadcasted_iota(jnp.int32, (8, tm), 0)
    hot = sub == gid
    r_io = lax.broadcasted_iota(jnp.int32, (tm, tm), 0)
    c_io = lax.broadcasted_iota(jnp.int32, (tm, tm), 1)
    before = (r_io < c_io).astype(jnp.bfloat16)
    seen = jnp.dot(jnp.where(hot, 1.0, 0.0).astype(jnp.bfloat16), before,
                   preferred_element_type=jnp.float32)
    pos = jnp.sum(jnp.where(hot, seen, 0.0), axis=0, keepdims=True).astype(jnp.int32)
    for g in range(N_GROUPS):
        pos = pos + jnp.where(gid == g, block_ref[i * N_GROUPS + g], 0)

    s_io = lax.broadcasted_iota(jnp.int32, (rows, tm), 0)
    sel = jnp.where(pos == s_io, 1.0, 0.0).astype(jnp.bfloat16)
    hbuf[slot, 0:rows] = jnp.dot(sel, h_ref[...], preferred_element_type=jnp.float32).astype(hbuf.dtype)
    aux3 = jnp.concatenate(_split_bf16(aux_ref[...], 3), axis=1)
    w3 = jnp.dot(sel, aux3, preferred_element_type=jnp.float32)
    wbuf[slot, 0:rows] = w3[:, :LANES] + w3[:, LANES:2 * LANES] + w3[:, 2 * LANES:]

    @pl.when(i > 0)
    def _():
        tile_chunks(i - 1, 1 - slot, lambda c: c.wait())

    tile_chunks(i, slot, lambda c: c.start())

    @pl.when(i == pl.num_programs(0) - 1)
    def _():
        tile_chunks(i, slot, lambda c: c.wait())
        hbuf[...] = jnp.zeros_like(hbuf)
        wbuf[...] = jnp.zeros_like(wbuf)
        for g in range(N_GROUPS):
            zero_rows(zfrom_ref[g], g)
            zero_rows(zto_ref[g] - tm, g)

        def tail(t, _):
            zero_rows(t * T_GROUP, 0)
            return 0

        lax.fori_loop(used_ref[0], hg_ref.shape[0] // T_GROUP, tail, 0)


def _moe_sort(plan, h, ids, aux):
    n, d = h.shape
    tm = TM_SORT
    rows = _max_group_tiles(n) * T_GROUP
    grid_spec = pltpu.PrefetchScalarGridSpec(
        num_scalar_prefetch=6,
        grid=(n // tm,),
        in_specs=[
            pl.BlockSpec((tm, d), lambda i, *_: (i, 0)),
            pl.BlockSpec((8, tm), lambda i, *_: (0, i)),
            pl.BlockSpec((tm, LANES), lambda i, *_: (i, 0)),
        ],
        out_specs=[pl.BlockSpec(memory_space=pl.ANY), pl.BlockSpec(memory_space=pl.ANY)],
        scratch_shapes=[pltpu.VMEM((2, SORT_ROWS + tm, d), jnp.bfloat16),
                        pltpu.VMEM((2, SORT_ROWS + tm, LANES), jnp.float32),
                        pltpu.SemaphoreType.DMA((2, N_GROUPS))],
    )
    return pl.pallas_call(
        _sort_kernel,
        grid_spec=grid_spec,
        out_shape=[jax.ShapeDtypeStruct((rows, d), jnp.bfloat16),
                   jax.ShapeDtypeStruct((rows, LANES), jnp.float32)],
        compiler_params=pltpu.CompilerParams(
            dimension_semantics=("arbitrary",), vmem_limit_bytes=VMEM_LIMIT),
        name="moe_sort",
    )(plan["block"], plan["dest"], plan["small"], plan["zero_from"], plan["zero_to"],
      plan["used_tiles"], h, ids, aux)


def _experts_kernel(tg_ref, used_ref, h_ref, w_ref, w1_ref, w3_ref, w2_ref, y_ref):
    i = pl.program_id(0)

    @pl.when(i < used_ref[0])
    def _():
        h = h_ref[...]
        w = w_ref[...]
        hidden = []
        for e in range(EXPERTS_PER_GROUP):
            a = jnp.dot(h, w1_ref[e], preferred_element_type=jnp.float32)
            b = jnp.dot(h, w3_ref[e], preferred_element_type=jnp.float32)
            hidden.append(((a * jax.nn.sigmoid(a)) * b * w[:, e:e + 1]).astype(jnp.bfloat16))
        ne, de, d = w2_ref.shape
        y = jnp.dot(jnp.concatenate(hidden, axis=1), w2_ref[...].reshape(ne * de, d),
                    preferred_element_type=jnp.float32)
        y_ref[...] = y.astype(y_ref.dtype)

    @pl.when(i >= used_ref[0])
    def _():
        y_ref[...] = jnp.zeros_like(y_ref)


def _moe_experts(plan, hg, wg, w1, w3, w2):
    rows, d = hg.shape
    t = T_GROUP
    de = w1.shape[-1]

    def row_tile(i, tg, used):
        return (jnp.minimum(i, used[0] - 1), 0)

    def group_weights(i, tg, used):
        return (tg[i], 0, 0)

    grid_spec = pltpu.PrefetchScalarGridSpec(
        num_scalar_prefetch=2,
        grid=(rows // t,),
        in_specs=[
            pl.BlockSpec((t, d), row_tile),
            pl.BlockSpec((t, LANES), row_tile),
            pl.BlockSpec((EXPERTS_PER_GROUP, d, de), group_weights),
            pl.BlockSpec((EXPERTS_PER_GROUP, d, de), group_weights),
            pl.BlockSpec((EXPERTS_PER_GROUP, de, d), group_weights),
        ],
        out_specs=pl.BlockSpec((t, d), lambda i, tg, used: (i, 0)),
    )
    return pl.pallas_call(
        _experts_kernel,
        grid_spec=grid_spec,
        out_shape=jax.ShapeDtypeStruct((rows, d), jnp.bfloat16),
        compiler_params=pltpu.CompilerParams(
            dimension_semantics=("arbitrary",), vmem_limit_bytes=VMEM_LIMIT),
        name="moe_experts",
    )(plan["tile_group"], plan["used_tiles"], hg, wg, w1, w3, w2)


def _unsort_kernel(dest_ref, small_ref, x1_ref, aux_ref, gn_ref, yg_ref, o_ref, ybuf, sems):
    i = pl.program_id(0)
    tm = TM_SORT
    slot = i % 2

    def chunks(step, slot, act):
        for g in range(N_GROUPS):
            idx = step * N_GROUPS + g
            off = pl.multiple_of(dest_ref[idx], ROW_ALIGN)
            for nrows, is_small in ((SMALL_CHUNK, True), (tm, False)):
                @pl.when((small_ref[idx] != 0) == is_small)
                def _():
                    act(pltpu.make_async_copy(yg_ref.at[pl.ds(off, nrows)],
                                              ybuf.at[slot, g, pl.ds(0, nrows)], sems.at[slot, g]))

    @pl.when(i == 0)
    def _():
        ybuf[...] = jnp.zeros_like(ybuf)
        chunks(0, 0, lambda c: c.start())

    @pl.when(i + 1 < pl.num_programs(0))
    def _():
        chunks(i + 1, 1 - slot, lambda c: c.start())

    chunks(i, slot, lambda c: c.wait())

    aux = aux_ref[...]
    lane = lax.broadcasted_iota(jnp.int32, (tm, LANES), 1)
    is_group_lane = jnp.logical_and(lane >= GROUP_LANE0, lane < GROUP_LANE0 + N_GROUPS)
    hot = jnp.where(is_group_lane, aux, 0.0)
    r_io = lax.broadcasted_iota(jnp.int32, (tm, tm), 0)
    c_io = lax.broadcasted_iota(jnp.int32, (tm, tm), 1)
    before = (c_io < r_io).astype(jnp.bfloat16)
    seen = jnp.dot(before, hot.astype(jnp.bfloat16), preferred_element_type=jnp.float32)
    rank = jnp.sum(hot * seen, axis=-1, keepdims=True).astype(jnp.int32)
    y = x1_ref[...]
    for g in range(N_GROUPS):
        mine = aux[:, GROUP_LANE0 + g:GROUP_LANE0 + g + 1] > 0.5
        sel = jnp.where(jnp.logical_and(rank == c_io, mine), 1.0, 0.0).astype(jnp.bfloat16)
        y = y + jnp.dot(sel, ybuf[slot, g], preferred_element_type=jnp.float32)
    o_ref[...] = _rmsnorm_f32(y, gn_ref[...])


def _moe_unsort(plan, x1, aux, gn, yg):
    n, d = x1.shape
    tm = TM_SORT
    grid_spec = pltpu.PrefetchScalarGridSpec(
        num_scalar_prefetch=2,
        grid=(n // tm,),
        in_specs=[
            pl.BlockSpec((tm, d), lambda i, *_: (i, 0)),
            pl.BlockSpec((tm, LANES), lambda i, *_: (i, 0)),
            pl.BlockSpec((1, d), lambda i, *_: (0, 0)),
            pl.BlockSpec(memory_space=pl.ANY),
        ],
        out_specs=pl.BlockSpec((tm, d), lambda i, *_: (i, 0)),
        scratch_shapes=[pltpu.VMEM((2, N_GROUPS, tm, d), jnp.bfloat16),
                        pltpu.SemaphoreType.DMA((2, N_GROUPS))],
    )
    return pl.pallas_call(
        _unsort_kernel,
        grid_spec=grid_spec,
        out_shape=jax.ShapeDtypeStruct((n, d), jnp.float32),
        compiler_params=pltpu.CompilerParams(
            dimension_semantics=("arbitrary",), vmem_limit_bytes=VMEM_LIMIT),
        name="moe_unsort",
    )(plan["dest"], plan["small"], x1, aux, gn, yg)


def kernel(x, norm_attn, w_in, b_forget, w_o_sb, w_o_fox, w_out, norm_ffn, w_router_group,
           b_router_group, w_router_expert, b_router_expert, w1, w3, w2, norm_final):
    b, s, d = x.shape
    n = b * s
    depth = w_in.shape[0]
    assert depth == 1, "the final norm is fused into the MoE un-sort of a single layer"
    assert s % TM_PROJ == 0 and s % TQ == 0 and n % TM_SORT == 0 and T_GROUP == TM_SORT
    bf16 = jnp.bfloat16
    n_main = 6 * ATTN_WIDTH
    x2 = x.reshape(n, d)
    for l in range(depth):
        w_l = w_in[l]
        w_main = jnp.concatenate([w_l[:, :n_main], w_l[:, n_main + N_HEADS:]], axis=1).astype(bf16)
        wf = jnp.pad(w_l[:, n_main:n_main + N_HEADS], ((0, 0), (0, LANES - N_HEADS)))
        wf_hi = wf.astype(bf16)
        wf_lo = (wf - wf_hi.astype(jnp.float32)).astype(bf16)
        bf = jnp.pad(b_forget[l], (0, LANES - N_HEADS)).reshape(1, LANES)
        wr = jnp.concatenate([w_router_group[l].T, jnp.zeros((8 - N_GROUPS, d), jnp.float32),
                              w_router_expert[l].T], axis=0)
        wr_hi = wr.astype(bf16)
        wr_lo = (wr - wr_hi.astype(jnp.float32)).astype(bf16)
        br = jnp.concatenate([b_router_group[l], jnp.zeros((8 - N_GROUPS,), jnp.float32),
                              b_router_expert[l]]).reshape(ROUTER_ROWS, 1)

        proj, c, ct = _inproj(x2, norm_attn[l].reshape(1, d), w_main, wf_hi, wf_lo, bf, s)
        proj3 = proj.reshape(b, s, proj.shape[1])
        o_sb, o_fx = _attention(proj3, c.reshape(b, s, LANES), ct)
        x1, h, ids, aux = _post_attention(
            o_sb.reshape(n, ATTN_WIDTH), o_fx.reshape(n, ATTN_WIDTH), proj, x2,
            w_o_sb[l].astype(bf16), w_o_fox[l].astype(bf16), w_out[l].astype(bf16),
            norm_ffn[l].reshape(1, d), wr_hi, wr_lo, br)
        plan = _sort_plan(ids[0] // EXPERTS_PER_GROUP, n)
        hg, wg = _moe_sort(plan, h, ids, aux)
        yg = _moe_experts(plan, hg, wg, w1[l].astype(bf16), w3[l].astype(bf16), w2[l].astype(bf16))
        x2 = _moe_unsort(plan, x1, aux, norm_final.reshape(1, d), yg)
    return x2.reshape(b, s, d)
```

```python
import functools
import math
from typing import Callable, NamedTuple

import jax
import jax.numpy as jnp
from jax import lax
from jax.experimental import pallas as pl
from jax.experimental.pallas import tpu as pltpu

HEAD_DIM = 64
N_HEADS = 8
ATTN_WIDTH = N_HEADS * HEAD_DIM
N_GROUPS = 4
EXPERTS_PER_GROUP = 8
N_EXPERTS = N_GROUPS * EXPERTS_PER_GROUP
RMS_EPS = 1e-6
LANES = 128
HEADS_PER_BLOCK = LANES // HEAD_DIM
N_HEAD_BLOCKS = N_HEADS // HEADS_PER_BLOCK
ROUTER_ROWS = 8 + N_EXPERTS
VMEM_LIMIT = 56 * 1024 * 1024

TM_PROJ = 512
TQ = 256
TK = 256
TM_POST = 512

_NT = (((1,), (1,)), ((), ()))
LOG2E = math.log2(math.e)
Q_SCALE = LOG2E / math.sqrt(HEAD_DIM)
SKIP_LOG2 = 160.0


def _split_bf16(v, parts):
    out = []
    r = v
    for i in range(parts):
        p = r.astype(jnp.bfloat16)
        out.append(p)
        if i + 1 < parts:
            r = r - p.astype(jnp.float32)
    return out


def _rmsnorm_f32(x, g):
    ms = jnp.mean(x * x, axis=-1, keepdims=True)
    return x * lax.rsqrt(ms + RMS_EPS) * g


def _inproj_kernel(x_ref, g_ref, w_ref, wf_ref, bf_ref,
                   proj_ref, c_ref, ct_ref, carry_ref, *, tiles_per_seq, n_chunks, chunk, q_chunks):
    i = pl.program_id(0)

    @pl.when(i % tiles_per_seq == 0)
    def _():
        carry_ref[...] = jnp.zeros_like(carry_ref)

    y = _rmsnorm_f32(x_ref[...], g_ref[...])
    h_hi, h_lo = _split_bf16(y, 2)
    for c in range(n_chunks):
        sl = slice(c * chunk, (c + 1) * chunk)
        p = jnp.dot(h_hi, w_ref[:, sl], preferred_element_type=jnp.float32)
        if c in q_chunks:
            p = p * Q_SCALE
        proj_ref[:, sl] = p.astype(proj_ref.dtype)

    ff = jnp.dot(h_hi, wf_ref[...], preferred_element_type=jnp.float32)
    f = (ff[:, :LANES] + ff[:, LANES:]
         + jnp.dot(h_lo, wf_ref[:, :LANES], preferred_element_type=jnp.float32))
    f = f + bf_ref[...]
    logf = (jnp.minimum(f, 0.0) - jnp.log1p(jnp.exp(-jnp.abs(f)))) * LOG2E

    tm = logf.shape[0]
    row = lax.broadcasted_iota(jnp.int32, (tm, tm), 0)
    col = lax.broadcasted_iota(jnp.int32, (tm, tm), 1)
    tri = (col <= row).astype(jnp.bfloat16)
    parts = jnp.dot(tri, jnp.concatenate(_split_bf16(logf, 3), axis=1),
                    preferred_element_type=jnp.float32)
    cum = carry_ref[...] + parts[:, :LANES] + parts[:, LANES:2 * LANES] + parts[:, 2 * LANES:]
    c_ref[...] = cum
    ct_ref[...] = cum.T[:N_HEADS, :]
    carry_ref[...] = cum[tm - 1:tm, :]


def _inproj(x2, g, w_main, wf_parts, bf, seq):
    n, d = x2.shape
    cols = w_main.shape[1]
    tm = TM_PROJ
    chunk = ATTN_WIDTH
    kern = functools.partial(_inproj_kernel, tiles_per_seq=seq // tm,
                             n_chunks=cols // chunk, chunk=chunk, q_chunks=(0, 3))
    const = dict(pipeline_mode=pl.Buffered(1))
    return pl.pallas_call(
        kern,
        grid=(n // tm,),
        in_specs=[
            pl.BlockSpec((tm, d), lambda i: (i, 0)),
            pl.BlockSpec((1, d), lambda i: (0, 0), **const),
            pl.BlockSpec((d, cols), lambda i: (0, 0), **const),
            pl.BlockSpec((d, 2 * LANES), lambda i: (0, 0), **const),
            pl.BlockSpec((1, LANES), lambda i: (0, 0), **const),
        ],
        out_specs=[
            pl.BlockSpec((tm, cols), lambda i: (i, 0)),
            pl.BlockSpec((tm, LANES), lambda i: (i, 0)),
            pl.BlockSpec((N_HEADS, tm), lambda i: (0, i)),
        ],
        out_shape=[
            jax.ShapeDtypeStruct((n, cols), jnp.bfloat16),
            jax.ShapeDtypeStruct((n, LANES), jnp.float32),
            jax.ShapeDtypeStruct((N_HEADS, n), jnp.float32),
        ],
        scratch_shapes=[pltpu.VMEM((1, LANES), jnp.float32)],
        compiler_params=pltpu.CompilerParams(
            dimension_semantics=("arbitrary",), vmem_limit_bytes=VMEM_LIMIT),
        name="inproj",
    )(x2, g, w_main, wf_parts, bf)


def _stack_heads(q):
    lane = lax.broadcasted_iota(jnp.int32, q.shape, 1)
    return jnp.concatenate(
        [jnp.where((lane // HEAD_DIM) == h, q, jnp.zeros_like(q)) for h in range(HEADS_PER_BLOCK)], axis=0)


def _unstack_heads(acc):
    tq = acc.shape[0] // HEADS_PER_BLOCK
    lane = lax.broadcasted_iota(jnp.int32, (tq, LANES), 1)
    out = acc[:tq]
    for h in range(1, HEADS_PER_BLOCK):
        out = jnp.where((lane // HEAD_DIM) == h, acc[h * tq:(h + 1) * tq], out)
    return out


def _lane_tile(x, width):
    return jnp.concatenate([x] * (width // LANES), axis=1)


class _Chain(NamedTuple):
    init: Callable
    score: Callable
    finish: Callable
    live: Callable


def _pipelined_tiles(k0, chains, diag):
    for ch in chains:
        ch.score(k0, 0, diag)

    def overlap(ch, kj, slot):
        ch.score(kj - 1, 1 - slot, False)
        ch.finish(kj, slot)

    def cond(state):
        p, gos = state[0], state[1:]
        return jnp.logical_and(p < k0 // 2, functools.reduce(jnp.logical_or, [g != 0 for g in gos]))

    def pair(state):
        p, gos = state[0], state[1:]
        kj = k0 - 2 * p
        for mask in range(1, 2 ** len(chains)):
            active = [c for i, c in enumerate(chains) if mask >> i & 1]
            preds = [(gos[i] != 0) if mask >> i & 1 else (gos[i] == 0) for i in range(len(chains))]

            @pl.when(functools.reduce(jnp.logical_and, preds))
            def _():
                for slot in (0, 1):
                    for ch in active:
                        overlap(ch, kj - slot, slot)

        new = [jnp.where(jnp.logical_and(g != 0, ch.live(kj - 1)), 1, 0).astype(jnp.int32)
               for g, ch in zip(gos, chains)]
        return (p + 1, *new)

    state = lax.while_loop(cond, pair, (jnp.int32(0),) + (jnp.int32(1),) * len(chains))

    for go, ch in zip(state[1:], chains):
        @pl.when(jnp.logical_and(go != 0, k0 % 2 == 1))
        def _():
            overlap(ch, 1, 0)
            ch.finish(0, 1)

        @pl.when(jnp.logical_and(go != 0, k0 % 2 == 0))
        def _():
            ch.finish(0, 0)


def _sb_chain(q_ref, k_ref, v_ref, acc_ref, carry_ref, lb_ref, lk_ref, rs_ref):
    qcat = _stack_heads(q_ref[...])
    m = HEADS_PER_BLOCK * TQ
    urow = lax.broadcasted_iota(jnp.int32, (TK, TK), 0)
    ucol = lax.broadcasted_iota(jnp.int32, (TK, TK), 1)
    upper = (ucol < urow).astype(jnp.bfloat16)

    def init():
        acc_ref[...] = jnp.zeros_like(acc_ref)
        carry_ref[...] = jnp.zeros_like(carry_ref)

    def score(kj, slot, diag):
        k = k_ref[pl.ds(kj * TK, TK), :]
        z = lax.dot_general(qcat, k, _NT, preferred_element_type=jnp.float32)
        t = jnp.log2(1.0 + jnp.exp2(-jnp.abs(z)))
        log_beta = jnp.minimum(z, 0.0) - t
        log_keep = log_beta - z
        if diag:
            row = lax.broadcasted_iota(jnp.int32, (m, TK), 0)
            col = lax.broadcasted_iota(jnp.int32, (m, TK), 1)
            below = col < (row % TQ)
            log_keep = jnp.where(below, log_keep, 0.0)
            log_beta = jnp.where(below, log_beta, -jnp.inf)
        lb_ref[slot] = log_beta
        lk_ref[slot] = log_keep.astype(jnp.bfloat16)
        rs_ref[slot] = jnp.broadcast_to(jnp.sum(log_keep, axis=-1, keepdims=True), (m, LANES))

    def finish(kj, slot, first=False):
        v = v_ref[pl.ds(kj * TK, TK), :]
        rest = jnp.dot(lk_ref[slot], upper, preferred_element_type=jnp.float32)
        if first:
            a = jnp.exp2(lb_ref[slot] + rest)
            acc_ref[...] = jnp.dot(a.astype(v.dtype), v, preferred_element_type=jnp.float32)
            carry_ref[...] = rs_ref[slot]
            return
        carry = carry_ref[...]
        a = jnp.exp2(lb_ref[slot] + rest + _lane_tile(carry, TK))
        acc_ref[...] += jnp.dot(a.astype(v.dtype), v, preferred_element_type=jnp.float32)
        carry_ref[...] = carry + rs_ref[slot]

    def live(kj):
        return jnp.max(carry_ref[...]) > -SKIP_LOG2

    return _Chain(init, score, finish, live)


def _fox_chain(hb, qi, q_ref, k_ref, v_ref, c_ref, ct_ref, acc_ref, m_ref, l_ref, z_ref, rm_ref,
               kn_ref):
    qcat = _stack_heads(q_ref[...])
    lane = lax.broadcasted_iota(jnp.int32, (TQ, LANES), 1)
    cblk = c_ref[...]
    cq = jnp.concatenate(
        [jnp.broadcast_to(
            jnp.sum(jnp.where(lane == hb * HEADS_PER_BLOCK + h, cblk, 0.0), axis=-1, keepdims=True),
            (TQ, TK)) for h in range(HEADS_PER_BLOCK)], axis=0)
    m = HEADS_PER_BLOCK * TQ
    ones = jnp.ones((TK, LANES), jnp.bfloat16)

    @pl.when(qi == 0)
    def _():
        kf = k_ref[...].astype(jnp.float32)
        ksq = kf * kf
        klane = lax.broadcasted_iota(jnp.int32, ksq.shape, 1)
        for h in range(HEADS_PER_BLOCK):
            n2 = jnp.sum(jnp.where((klane // HEAD_DIM) == h, ksq, 0.0), axis=-1, keepdims=True)
            kn_ref[h * TQ:(h + 1) * TQ, :] = jnp.broadcast_to(
                jnp.sqrt(jnp.max(n2, axis=0, keepdims=True)), (TQ, LANES))

    qf = qcat.astype(jnp.float32)
    qn = jnp.broadcast_to(jnp.sqrt(jnp.sum(qf * qf, axis=-1, keepdims=True)), (m, LANES))

    def init():
        acc_ref[...] = jnp.zeros_like(acc_ref)
        l_ref[...] = jnp.zeros_like(l_ref)
        m_ref[...] = jnp.full_like(m_ref, -jnp.inf)

    def score(kj, slot, diag):
        k = k_ref[pl.ds(kj * TK, TK), :]
        ck = jnp.concatenate(
            [jnp.broadcast_to(ct_ref[pl.ds(hb * HEADS_PER_BLOCK + h, 1), pl.ds(kj * TK, TK)], (TQ, TK))
             for h in range(HEADS_PER_BLOCK)], axis=0)
        z = lax.dot_general(qcat, k, _NT, preferred_element_type=jnp.float32)
        z = (z + cq) - ck
        if diag:
            row = lax.broadcasted_iota(jnp.int32, (m, TK), 0)
            col = lax.broadcasted_iota(jnp.int32, (m, TK), 1)
            z = jnp.where(col <= (row % TQ), z, -jnp.inf)
        z_ref[slot] = z
        rm_ref[slot] = jnp.broadcast_to(jnp.max(z, axis=-1, keepdims=True), (m, LANES))

    def finish(kj, slot, first=False):
        v = v_ref[pl.ds(kj * TK, TK), :]
        if first:
            m_new = rm_ref[slot]
            p = jnp.exp2(z_ref[slot] - _lane_tile(m_new, TK))
            pv = jnp.dot(p.astype(v.dtype), jnp.concatenate([v, ones], axis=1),
                         preferred_element_type=jnp.float32)
            acc_ref[...] = pv[:, :LANES]
            l_ref[...] = pv[:, LANES:]
            m_ref[...] = m_new
            return
        m_old = m_ref[...]
        m_new = jnp.maximum(m_old, rm_ref[slot])
        alpha = jnp.exp2(m_old - m_new)
        p = jnp.exp2(z_ref[slot] - _lane_tile(m_new, TK))
        pv = jnp.dot(p.astype(v.dtype), jnp.concatenate([v, ones], axis=1),
                     preferred_element_type=jnp.float32)
        acc_ref[...] = alpha * acc_ref[...] + pv[:, :LANES]
        l_ref[...] = alpha * l_ref[...] + pv[:, LANES:]
        m_ref[...] = m_new

    def live(kj):
        cb = jnp.concatenate(
            [jnp.broadcast_to(
                jnp.max(ct_ref[pl.ds(hb * HEADS_PER_BLOCK + h, 1), pl.ds(kj * TK, TK)],
                        axis=-1, keepdims=True), (TQ, LANES)) for h in range(HEADS_PER_BLOCK)], axis=0)
        bound = qn * kn_ref[...] + cq[:, :LANES] - cb - m_ref[...]
        return jnp.max(bound) > -SKIP_LOG2

    return _Chain(init, score, finish, live)


SB_HEAD_TILES = 2
FOX_HEAD_TILES = 4


def _attn_kernel(qs_ref, ks_ref, vs_ref, qf_ref, kf_ref, vf_ref, c_ref, ct_ref, osb_ref, ofx_ref,
                 sb_acc, sb_carry, sb_lb, sb_lk, sb_rs, fx_acc, fx_m, fx_l, fx_z, fx_rm, fx_kn):
    hb = pl.program_id(1)
    qi = pl.program_id(2)
    sb = _sb_chain(qs_ref, ks_ref, vs_ref, sb_acc, sb_carry, sb_lb, sb_lk, sb_rs)
    fx = _fox_chain(hb, qi, qf_ref, kf_ref, vf_ref, c_ref, ct_ref, fx_acc, fx_m, fx_l, fx_z, fx_rm, fx_kn)
    head = max(SB_HEAD_TILES, FOX_HEAD_TILES)

    @pl.when(qi < head)
    def _():
        sb.init()
        fx.init()
        _pipelined_tiles(qi, [sb, fx], True)

    @pl.when(qi >= head)
    def _():
        for i in range(head + 1):
            for ch, count in ((sb, SB_HEAD_TILES), (fx, FOX_HEAD_TILES)):
                if i < count:
                    ch.score(qi - i, i, i == 0)
                if 1 <= i <= count:
                    ch.finish(qi - i + 1, i - 1, i == 1)
        for ch, count in ((sb, SB_HEAD_TILES), (fx, FOX_HEAD_TILES)):
            @pl.when(ch.live(qi - count + 1))
            def _():
                _pipelined_tiles(qi - count, [ch], False)

    osb_ref[...] = _unstack_heads(sb_acc[...]).astype(osb_ref.dtype)
    ofx_ref[...] = _unstack_heads(fx_acc[...] / fx_l[...]).astype(ofx_ref.dtype)


def _attention(proj3, c3, ct):
    b, s, _ = proj3.shape
    m = HEADS_PER_BLOCK * TQ

    def q_spec(blk):
        return pl.BlockSpec((None, TQ, LANES), lambda bi, hb, qi: (bi, qi, blk * N_HEAD_BLOCKS + hb))

    def kv_spec(blk):
        return pl.BlockSpec((None, s, LANES), lambda bi, hb, qi: (bi, 0, blk * N_HEAD_BLOCKS + hb))

    out_spec = pl.BlockSpec((None, TQ, LANES), lambda bi, hb, qi: (bi, qi, hb))
    out_shape = jax.ShapeDtypeStruct((b, s, ATTN_WIDTH), jnp.bfloat16)
    f32 = jnp.float32
    return pl.pallas_call(
        _attn_kernel,
        grid=(b, N_HEAD_BLOCKS, s // TQ),
        in_specs=[q_spec(0), kv_spec(1), kv_spec(2), q_spec(3), kv_spec(4), kv_spec(5),
                  pl.BlockSpec((None, TQ, LANES), lambda bi, hb, qi: (bi, qi, 0)),
                  pl.BlockSpec((N_HEADS, s), lambda bi, hb, qi: (0, bi))],
        out_specs=[out_spec, out_spec],
        out_shape=[out_shape, out_shape],
        scratch_shapes=[pltpu.VMEM((m, LANES), f32),
                        pltpu.VMEM((m, LANES), f32),
                        pltpu.VMEM((2, m, TK), f32),
                        pltpu.VMEM((2, m, TK), jnp.bfloat16),
                        pltpu.VMEM((2, m, LANES), f32),
                        pltpu.VMEM((m, LANES), f32),
                        pltpu.VMEM((m, LANES), f32),
                        pltpu.VMEM((m, LANES), f32),
                        pltpu.VMEM((FOX_HEAD_TILES, m, TK), f32),
                        pltpu.VMEM((FOX_HEAD_TILES, m, LANES), f32),
                        pltpu.VMEM((m, LANES), f32)],
        compiler_params=pltpu.CompilerParams(
            dimension_semantics=("arbitrary", "arbitrary", "arbitrary"),
            vmem_limit_bytes=VMEM_LIMIT),
        name="attn",
    )(proj3, proj3, proj3, proj3, proj3, proj3, c3, ct)


def _post_kernel(osb_ref, ofx_ref, gsb_ref, gfx_ref, x_ref, wosb_ref, wofx_ref, wout_ref,
                 gn_ref, wrh_ref, wrl_ref, br_ref,
                 x1_ref, h_ref, ids_ref, aux_ref):
    y_sb = jnp.dot(osb_ref[...], wosb_ref[...], preferred_element_type=jnp.float32)
    y_fx = jnp.dot(ofx_ref[...], wofx_ref[...], preferred_element_type=jnp.float32)
    mixed = (jax.nn.sigmoid(gsb_ref[...].astype(jnp.float32)) * y_sb
             + jax.nn.sigmoid(gfx_ref[...].astype(jnp.float32)) * y_fx)
    x1 = x_ref[...] + jnp.dot(mixed.astype(jnp.bfloat16), wout_ref[...],
                              preferred_element_type=jnp.float32)
    x1_ref[...] = x1
    y = _rmsnorm_f32(x1, gn_ref[...])
    h_hi, h_lo = _split_bf16(y, 2)
    h_ref[...] = h_hi

    lg = (lax.dot_general(wrh_ref[...], h_hi, _NT, preferred_element_type=jnp.float32)
          + lax.dot_general(wrh_ref[...], h_lo, _NT, preferred_element_type=jnp.float32)
          + lax.dot_general(wrl_ref[...], h_hi, _NT, preferred_element_type=jnp.float32))
    lg = lg + br_ref[...]
    tm = lg.shape[1]
    sub = lax.broadcasted_iota(jnp.int32, (8, tm), 0)
    neg = -jnp.inf

    gl = jnp.where(sub < N_GROUPS, lg[0:8], neg)
    gm = jnp.max(gl, axis=0, keepdims=True)
    g_w = 1.0 / jnp.sum(jnp.exp(gl - gm), axis=0, keepdims=True)
    g_idx = jnp.min(jnp.where(gl == gm, sub, 8), axis=0, keepdims=True)

    e_sel = lg[8:16]
    for g in range(1, N_GROUPS):
        e_sel = jnp.where(g_idx == g, lg[8 + 8 * g:16 + 8 * g], e_sel)
    m1 = jnp.max(e_sel, axis=0, keepdims=True)
    i1 = jnp.min(jnp.where(e_sel == m1, sub, 8), axis=0, keepdims=True)
    e_rest = jnp.where(sub == i1, neg, e_sel)
    m2 = jnp.max(e_rest, axis=0, keepdims=True)
    i2 = jnp.min(jnp.where(e_rest == m2, sub, 8), axis=0, keepdims=True)
    p2 = jnp.exp(m2 - m1)
    w1 = g_w / (1.0 + p2)
    w2 = g_w * p2 / (1.0 + p2)
    base = g_idx * EXPERTS_PER_GROUP
    ids_ref[...] = jnp.where(sub == 0, base + i1, jnp.where(sub == 1, base + i2, 0))
    dense_w = jnp.where(sub == i1, w1, jnp.where(sub == i2, w2, 0.0))
    group_hot = jnp.where(sub == g_idx, 1.0, 0.0)
    record = jnp.concatenate(
        [dense_w, group_hot, jnp.zeros((LANES - 16, tm), jnp.float32)], axis=0)
    aux_ref[...] = record.T


def _post_attention(o_sb, o_fx, proj, x2, wosb, wofx, wout, gn, wr_hi, wr_lo, br):
    n, d = x2.shape
    tm = TM_POST
    gate_blk = (3 * ATTN_WIDTH * 2) // d
    const = dict(pipeline_mode=pl.Buffered(1))
    return pl.pallas_call(
        _post_kernel,
        grid=(n // tm,),
        in_specs=[
            pl.BlockSpec((tm, ATTN_WIDTH), lambda i: (i, 0)),
            pl.BlockSpec((tm, ATTN_WIDTH), lambda i: (i, 0)),
            pl.BlockSpec((tm, d), lambda i: (i, gate_blk)),
            pl.BlockSpec((tm, d), lambda i: (i, gate_blk + 1)),
            pl.BlockSpec((tm, d), lambda i: (i, 0)),
            pl.BlockSpec((ATTN_WIDTH, d), lambda i: (0, 0), **const),
            pl.BlockSpec((ATTN_WIDTH, d), lambda i: (0, 0), **const),
            pl.BlockSpec((d, d), lambda i: (0, 0), **const),
            pl.BlockSpec((1, d), lambda i: (0, 0), **const),
            pl.BlockSpec((ROUTER_ROWS, d), lambda i: (0, 0), **const),
            pl.BlockSpec((ROUTER_ROWS, d), lambda i: (0, 0), **const),
            pl.BlockSpec((ROUTER_ROWS, 1), lambda i: (0, 0), **const),
        ],
        out_specs=[
            pl.BlockSpec((tm, d), lambda i: (i, 0)),
            pl.BlockSpec((tm, d), lambda i: (i, 0)),
            pl.BlockSpec((8, tm), lambda i: (0, i)),
            pl.BlockSpec((tm, LANES), lambda i: (i, 0)),
        ],
        out_shape=[
            jax.ShapeDtypeStruct((n, d), jnp.float32),
            jax.ShapeDtypeStruct((n, d), jnp.bfloat16),
            jax.ShapeDtypeStruct((8, n), jnp.int32),
            jax.ShapeDtypeStruct((n, LANES), jnp.float32),
        ],
        compiler_params=pltpu.CompilerParams(
            dimension_semantics=("arbitrary",), vmem_limit_bytes=VMEM_LIMIT),
        name="post_attn",
    )(o_sb, o_fx, proj, proj, x2, wosb, wofx, wout, gn, wr_hi, wr_lo, br)


TM_SORT = 256
T_GROUP = 256
ROW_ALIGN = 16
SORT_ROWS = TM_SORT + N_GROUPS * ROW_ALIGN
SMALL_CHUNK = TM_SORT // 2
GROUP_LANE0 = EXPERTS_PER_GROUP


def _max_group_tiles(n):
    padded = n + (ROW_ALIGN - 1) * N_GROUPS * (n // TM_SORT)
    return padded // T_GROUP + N_GROUPS * (TM_SORT // T_GROUP + 2)


def _sort_plan(token_group, n):
    i32 = jnp.int32
    tiles = n // TM_SORT
    hot = token_group.reshape(tiles, TM_SORT, 1) == jnp.arange(N_GROUPS, dtype=i32)
    padded = (jnp.sum(hot, axis=1, dtype=i32) + (ROW_ALIGN - 1)) // ROW_ALIGN * ROW_ALIGN
    block = jnp.cumsum(padded, axis=1) - padded
    total = jnp.sum(padded, axis=0)
    group_tiles = (total + (TM_SORT + T_GROUP - 1)) // T_GROUP
    region = group_tiles * T_GROUP
    base = jnp.cumsum(region) - region
    dest = base[None, :] + jnp.cumsum(padded, axis=0) - padded
    ends = jnp.cumsum(group_tiles)
    steps = jnp.arange(_max_group_tiles(n), dtype=i32)
    tile_group = jnp.minimum(jnp.sum(steps[:, None] >= ends[None, :], axis=1), N_GROUPS - 1)
    return dict(block=block.reshape(-1).astype(i32), dest=dest.reshape(-1).astype(i32),
                small=(padded <= SMALL_CHUNK).reshape(-1).astype(i32),
                zero_from=(base + total).astype(i32), zero_to=(base + region).astype(i32),
                used_tiles=ends[-1:].astype(i32), tile_group=tile_group.astype(i32))


def _sort_kernel(block_ref, dest_ref, small_ref, zfrom_ref, zto_ref, used_ref,
                 h_ref, ids_ref, aux_ref, hg_ref, wg_ref, hbuf, wbuf, sems):
    i = pl.program_id(0)
    tm = TM_SORT
    rows = SORT_ROWS
    slot = i % 2

    def copies(g, slot, src, off, nrows):
        return (pltpu.make_async_copy(hbuf.at[slot, pl.ds(src, nrows)], hg_ref.at[pl.ds(off, nrows)],
                                      sems.at[0, g]),
                pltpu.make_async_copy(wbuf.at[slot, pl.ds(src, nrows)], wg_ref.at[pl.ds(off, nrows)],
                                      sems.at[1, g]))

    def tile_chunks(step, slot, act):
        for g in range(N_GROUPS):
            idx = step * N_GROUPS + g
            src = pl.multiple_of(block_ref[idx], ROW_ALIGN)
            off = pl.multiple_of(dest_ref[idx], ROW_ALIGN)
            for nrows, is_small in ((SMALL_CHUNK, True), (tm, False)):
                @pl.when((small_ref[idx] != 0) == is_small)
                def _():
                    for c in copies(g, slot, src, off, nrows):
                        act(c)

    def zero_rows(off, g):
        for c in copies(g, 0, 0, pl.multiple_of(off, ROW_ALIGN), tm):
            c.start()
        for c in copies(g, 0, 0, pl.multiple_of(off, ROW_ALIGN), tm):
            c.wait()

    @pl.when(i == 0)
    def _():
        hbuf[...] = jnp.zeros_like(hbuf)
        wbuf[...] = jnp.zeros_like(wbuf)

    gid = ids_ref[0:1, :] // EXPERTS_PER_GROUP
    sub = lax.broadcasted_iota(jnp.int32, (8, tm), 0)
    hot = sub == gid
    r_io = lax.broadcasted_iota(jnp.int32, (tm, tm), 0)
    c_io = lax.broadcasted_iota(jnp.int32, (tm, tm), 1)
    before = (r_io < c_io).astype(jnp.bfloat16)
    seen = jnp.dot(jnp.where(hot, 1.0, 0.0).astype(jnp.bfloat16), before,
                   preferred_element_type=jnp.float32)
    pos = jnp.sum(jnp.where(hot, seen, 0.0), axis=0, keepdims=True).astype(jnp.int32)
    for g in range(N_GROUPS):
        pos = pos + jnp.where(gid == g, block_ref[i * N_GROUPS + g], 0)

    s_io = lax.broadcasted_iota(jnp.int32, (rows, tm), 0)
    sel = jnp.where(pos == s_io, 1.0, 0.0).astype(jnp.bfloat16)
    hbuf[slot, 0:rows] = jnp.dot(sel, h_ref[...], preferred_element_type=jnp.float32).astype(hbuf.dtype)
    aux3 = jnp.concatenate(_split_bf16(aux_ref[...], 3), axis=1)
    w3 = jnp.dot(sel, aux3, preferred_element_type=jnp.float32)
    wbuf[slot, 0:rows] = w3[:, :LANES] + w3[:, LANES:2 * LANES] + w3[:, 2 * LANES:]

    @pl.when(i > 0)
    def _():
        tile_chunks(i - 1, 1 - slot, lambda c: c.wait())

    tile_chunks(i, slot, lambda c: c.start())

    @pl.when(i == pl.num_programs(0) - 1)
    def _():
        tile_chunks(i, slot, lambda c: c.wait())
        hbuf[...] = jnp.zeros_like(hbuf)
        wbuf[...] = jnp.zeros_like(wbuf)
        for g in range(N_GROUPS):
            zero_rows(zfrom_ref[g], g)
            zero_rows(zto_ref[g] - tm, g)

        def tail(t, _):
            zero_rows(t * T_GROUP, 0)
            return 0

        lax.fori_loop(used_ref[0], hg_ref.shape[0] // T_GROUP, tail, 0)


def _moe_sort(plan, h, ids, aux):
    n, d = h.shape
    tm = TM_SORT
    rows = _max_group_tiles(n) * T_GROUP
    grid_spec = pltpu.PrefetchScalarGridSpec(
        num_scalar_prefetch=6,
        grid=(n // tm,),
        in_specs=[
            pl.BlockSpec((tm, d), lambda i, *_: (i, 0)),
            pl.BlockSpec((8, tm), lambda i, *_: (0, i)),
            pl.BlockSpec((tm, LANES), lambda i, *_: (i, 0)),
        ],
        out_specs=[pl.BlockSpec(memory_space=pl.ANY), pl.BlockSpec(memory_space=pl.ANY)],
        scratch_shapes=[pltpu.VMEM((2, SORT_ROWS + tm, d), jnp.bfloat16),
                        pltpu.VMEM((2, SORT_ROWS + tm, LANES), jnp.float32),
                        pltpu.SemaphoreType.DMA((2, N_GROUPS))],
    )
    return pl.pallas_call(
        _sort_kernel,
        grid_spec=grid_spec,
        out_shape=[jax.ShapeDtypeStruct((rows, d), jnp.bfloat16),
                   jax.ShapeDtypeStruct((rows, LANES), jnp.float32)],
        compiler_params=pltpu.CompilerParams(
            dimension_semantics=("arbitrary",), vmem_limit_bytes=VMEM_LIMIT),
        name="moe_sort",
    )(plan["block"], plan["dest"], plan["small"], plan["zero_from"], plan["zero_to"],
      plan["used_tiles"], h, ids, aux)


def _experts_kernel(tg_ref, used_ref, h_ref, w_ref, w1_ref, w3_ref, w2_ref, y_ref):
    i = pl.program_id(0)

    @pl.when(i < used_ref[0])
    def _():
        h = h_ref[...]
        w = w_ref[...]
        hidden = []
        for e in range(EXPERTS_PER_GROUP):
            a = jnp.dot(h, w1_ref[e], preferred_element_type=jnp.float32)
            b = jnp.dot(h, w3_ref[e], preferred_element_type=jnp.float32)
            hidden.append(((a * jax.nn.sigmoid(a)) * b * w[:, e:e + 1]).astype(jnp.bfloat16))
        ne, de, d = w2_ref.shape
        y = jnp.dot(jnp.concatenate(hidden, axis=1), w2_ref[...].reshape(ne * de, d),
                    preferred_element_type=jnp.float32)
        y_ref[...] = y.astype(y_ref.dtype)

    @pl.when(i >= used_ref[0])
    def _():
        y_ref[...] = jnp.zeros_like(y_ref)


def _moe_experts(plan, hg, wg, w1, w3, w2):
    rows, d = hg.shape
    t = T_GROUP
    de = w1.shape[-1]

    def row_tile(i, tg, used):
        return (jnp.minimum(i, used[0] - 1), 0)

    def group_weights(i, tg, used):
        return (tg[i], 0, 0)

    grid_spec = pltpu.PrefetchScalarGridSpec(
        num_scalar_prefetch=2,
        grid=(rows // t,),
        in_specs=[
            pl.BlockSpec((t, d), row_tile),
            pl.BlockSpec((t, LANES), row_tile),
            pl.BlockSpec((EXPERTS_PER_GROUP, d, de), group_weights),
            pl.BlockSpec((EXPERTS_PER_GROUP, d, de), group_weights),
            pl.BlockSpec((EXPERTS_PER_GROUP, de, d), group_weights),
        ],
        out_specs=pl.BlockSpec((t, d), lambda i, tg, used: (i, 0)),
    )
    return pl.pallas_call(
        _experts_kernel,
        grid_spec=grid_spec,
        out_shape=jax.ShapeDtypeStruct((rows, d), jnp.bfloat16),
        compiler_params=pltpu.CompilerParams(
            dimension_semantics=("arbitrary",), vmem_limit_bytes=VMEM_LIMIT),
        name="moe_experts",
    )(plan["tile_group"], plan["used_tiles"], hg, wg, w1, w3, w2)


def _unsort_kernel(dest_ref, small_ref, x1_ref, aux_ref, gn_ref, yg_ref, o_ref, ybuf, sems):
    i = pl.program_id(0)
    tm = TM_SORT
    slot = i % 2

    def chunks(step, slot, act):
        for g in range(N_GROUPS):
            idx = step * N_GROUPS + g
            off = pl.multiple_of(dest_ref[idx], ROW_ALIGN)
            for nrows, is_small in ((SMALL_CHUNK, True), (tm, False)):
                @pl.when((small_ref[idx] != 0) == is_small)
                def _():
                    act(pltpu.make_async_copy(yg_ref.at[pl.ds(off, nrows)],
                                              ybuf.at[slot, g, pl.ds(0, nrows)], sems.at[slot, g]))

    @pl.when(i == 0)
    def _():
        ybuf[...] = jnp.zeros_like(ybuf)
        chunks(0, 0, lambda c: c.start())

    @pl.when(i + 1 < pl.num_programs(0))
    def _():
        chunks(i + 1, 1 - slot, lambda c: c.start())

    chunks(i, slot, lambda c: c.wait())

    aux = aux_ref[...]
    lane = lax.broadcasted_iota(jnp.int32, (tm, LANES), 1)
    is_group_lane = jnp.logical_and(lane >= GROUP_LANE0, lane < GROUP_LANE0 + N_GROUPS)
    hot = jnp.where(is_group_lane, aux, 0.0)
    r_io = lax.broadcasted_iota(jnp.int32, (tm, tm), 0)
    c_io = lax.broadcasted_iota(jnp.int32, (tm, tm), 1)
    before = (c_io < r_io).astype(jnp.bfloat16)
    seen = jnp.dot(before, hot.astype(jnp.bfloat16), preferred_element_type=jnp.float32)
    rank = jnp.sum(hot * seen, axis=-1, keepdims=True).astype(jnp.int32)
    y = x1_ref[...]
    for g in range(N_GROUPS):
        mine = aux[:, GROUP_LANE0 + g:GROUP_LANE0 + g + 1] > 0.5
        sel = jnp.where(jnp.logical_and(rank == c_io, mine), 1.0, 0.0).astype(jnp.bfloat16)
        y = y + jnp.dot(sel, ybuf[slot, g], preferred_element_type=jnp.float32)
    o_ref[...] = _rmsnorm_f32(y, gn_ref[...])


def _moe_unsort(plan, x1, aux, gn, yg):
    n, d = x1.shape
    tm = TM_SORT
    grid_spec = pltpu.PrefetchScalarGridSpec(
        num_scalar_prefetch=2,
        grid=(n // tm,),
        in_specs=[
            pl.BlockSpec((tm, d), lambda i, *_: (i, 0)),
            pl.BlockSpec((tm, LANES), lambda i, *_: (i, 0)),
            pl.BlockSpec((1, d), lambda i, *_: (0, 0)),
            pl.BlockSpec(memory_space=pl.ANY),
        ],
        out_specs=pl.BlockSpec((tm, d), lambda i, *_: (i, 0)),
        scratch_shapes=[pltpu.VMEM((2, N_GROUPS, tm, d), jnp.bfloat16),
                        pltpu.SemaphoreType.DMA((2, N_GROUPS))],
    )
    return pl.pallas_call(
        _unsort_kernel,
        grid_spec=grid_spec,
        out_shape=jax.ShapeDtypeStruct((n, d), jnp.float32),
        compiler_params=pltpu.CompilerParams(
            dimension_semantics=("arbitrary",), vmem_limit_bytes=VMEM_LIMIT),
        name="moe_unsort",
    )(plan["dest"], plan["small"], x1, aux, gn, yg)


def kernel(x, norm_attn, w_in, b_forget, w_o_sb, w_o_fox, w_out, norm_ffn, w_router_group,
           b_router_group, w_router_expert, b_router_expert, w1, w3, w2, norm_final):
    b, s, d = x.shape
    n = b * s
    depth = w_in.shape[0]
    assert depth == 1, "the final norm is fused into the MoE un-sort of a single layer"
    assert s % TM_PROJ == 0 and s % TQ == 0 and n % TM_SORT == 0 and T_GROUP == TM_SORT
    bf16 = jnp.bfloat16
    n_main = 6 * ATTN_WIDTH
    x2 = x.reshape(n, d)
    for l in range(depth):
        w_l = w_in[l]
        w_main = jnp.concatenate([w_l[:, :n_main], w_l[:, n_main + N_HEADS:]], axis=1).astype(bf16)
        wf = jnp.pad(w_l[:, n_main:n_main + N_HEADS], ((0, 0), (0, LANES - N_HEADS)))
        wf_hi = wf.astype(bf16)
        wf_parts = jnp.concatenate([wf_hi, (wf - wf_hi.astype(jnp.float32)).astype(bf16)], axis=1)
        bf = jnp.pad(b_forget[l], (0, LANES - N_HEADS)).reshape(1, LANES)
        wr = jnp.concatenate([w_router_group[l].T, jnp.zeros((8 - N_GROUPS, d), jnp.float32),
                              w_router_expert[l].T], axis=0)
        wr_hi = wr.astype(bf16)
        wr_lo = (wr - wr_hi.astype(jnp.float32)).astype(bf16)
        br = jnp.concatenate([b_router_group[l], jnp.zeros((8 - N_GROUPS,), jnp.float32),
                              b_router_expert[l]]).reshape(ROUTER_ROWS, 1)

        proj, c, ct = _inproj(x2, norm_attn[l].reshape(1, d), w_main, wf_parts, bf, s)
        proj3 = proj.reshape(b, s, proj.shape[1])
        o_sb, o_fx = _attention(proj3, c.reshape(b, s, LANES), ct)
        x1, h, ids, aux = _post_attention(
            o_sb.reshape(n, ATTN_WIDTH), o_fx.reshape(n, ATTN_WIDTH), proj, x2,
            w_o_sb[l].astype(bf16), w_o_fox[l].astype(bf16), w_out[l].astype(bf16),
            norm_ffn[l].reshape(1, d), wr_hi, wr_lo, br)
        plan = _sort_plan(ids[0] // EXPERTS_PER_GROUP, n)
        hg, wg = _moe_sort(plan, h, ids, aux)
        yg = _moe_experts(plan, hg, wg, w1[l].astype(bf16), w3[l].astype(bf16), w2[l].astype(bf16))
        x2 = _moe_unsort(plan, x1, aux, norm_final.reshape(1, d), yg)
    return x2.reshape(b, s, d)
```

```python
import functools
import math
from typing import Callable, NamedTuple

import jax
import jax.numpy as jnp
from jax import lax
from jax.experimental import pallas as pl
from jax.experimental.pallas import tpu as pltpu

HEAD_DIM = 64
N_HEADS = 8
ATTN_WIDTH = N_HEADS * HEAD_DIM
N_GROUPS = 4
EXPERTS_PER_GROUP = 8
N_EXPERTS = N_GROUPS * EXPERTS_PER_GROUP
RMS_EPS = 1e-6
LANES = 128
HEADS_PER_BLOCK = LANES // HEAD_DIM
N_HEAD_BLOCKS = N_HEADS // HEADS_PER_BLOCK
ROUTER_ROWS = 8 + N_EXPERTS
VMEM_LIMIT = 56 * 1024 * 1024

TM_PROJ = 512
TQ = 256
TK = 256
TM_POST = 512

_NT = (((1,), (1,)), ((), ()))
LOG2E = math.log2(math.e)
Q_SCALE = LOG2E / math.sqrt(HEAD_DIM)
SKIP_LOG2 = 160.0


def _split_bf16(v, parts):
    out = []
    r = v
    for i in range(parts):
        p = r.astype(jnp.bfloat16)
        out.append(p)
        if i + 1 < parts:
            r = r - p.astype(jnp.float32)
    return out


def _rmsnorm_f32(x, g):
    ms = jnp.mean(x * x, axis=-1, keepdims=True)
    return x * lax.rsqrt(ms + RMS_EPS) * g


def _inproj_kernel(x_ref, g_ref, w_ref, wf_ref, bf_ref,
                   proj_ref, c_ref, ct_ref, carry_ref, *, tiles_per_seq, n_chunks, chunk, q_chunks):
    i = pl.program_id(0)

    @pl.when(i % tiles_per_seq == 0)
    def _():
        carry_ref[...] = jnp.zeros_like(carry_ref)

    y = _rmsnorm_f32(x_ref[...], g_ref[...])
    h_hi, h_lo = _split_bf16(y, 2)
    for c in range(n_chunks):
        sl = slice(c * chunk, (c + 1) * chunk)
        p = jnp.dot(h_hi, w_ref[:, sl], preferred_element_type=jnp.float32)
        if c in q_chunks:
            p = p * Q_SCALE
        proj_ref[:, sl] = p.astype(proj_ref.dtype)

    ff = jnp.dot(h_hi, wf_ref[...], preferred_element_type=jnp.float32)
    f = (ff[:, :LANES] + ff[:, LANES:]
         + jnp.dot(h_lo, wf_ref[:, :LANES], preferred_element_type=jnp.float32))
    f = f + bf_ref[...]
    logf = (jnp.minimum(f, 0.0) - jnp.log1p(jnp.exp(-jnp.abs(f)))) * LOG2E

    tm = logf.shape[0]
    row = lax.broadcasted_iota(jnp.int32, (tm, tm), 0)
    col = lax.broadcasted_iota(jnp.int32, (tm, tm), 1)
    tri = (col <= row).astype(jnp.bfloat16)
    parts = jnp.dot(tri, jnp.concatenate(_split_bf16(logf, 3), axis=1),
                    preferred_element_type=jnp.float32)
    cum = carry_ref[...] + parts[:, :LANES] + parts[:, LANES:2 * LANES] + parts[:, 2 * LANES:]
    c_ref[...] = cum
    ct_ref[...] = cum.T[:N_HEADS, :]
    carry_ref[...] = cum[tm - 1:tm, :]


def _inproj(x2, g, w_main, wf_parts, bf, seq):
    n, d = x2.shape
    cols = w_main.shape[1]
    tm = TM_PROJ
    chunk = ATTN_WIDTH
    kern = functools.partial(_inproj_kernel, tiles_per_seq=seq // tm,
                             n_chunks=cols // chunk, chunk=chunk, q_chunks=(0, 3))
    const = dict(pipeline_mode=pl.Buffered(1))
    return pl.pallas_call(
        kern,
        grid=(n // tm,),
        in_specs=[
            pl.BlockSpec((tm, d), lambda i: (i, 0)),
            pl.BlockSpec((1, d), lambda i: (0, 0), **const),
            pl.BlockSpec((d, cols), lambda i: (0, 0), **const),
            pl.BlockSpec((d, 2 * LANES), lambda i: (0, 0), **const),
            pl.BlockSpec((1, LANES), lambda i: (0, 0), **const),
        ],
        out_specs=[
            pl.BlockSpec((tm, cols), lambda i: (i, 0)),
            pl.BlockSpec((tm, LANES), lambda i: (i, 0)),
            pl.BlockSpec((N_HEADS, tm), lambda i: (0, i)),
        ],
        out_shape=[
            jax.ShapeDtypeStruct((n, cols), jnp.bfloat16),
            jax.ShapeDtypeStruct((n, LANES), jnp.float32),
            jax.ShapeDtypeStruct((N_HEADS, n), jnp.float32),
        ],
        scratch_shapes=[pltpu.VMEM((1, LANES), jnp.float32)],
        compiler_params=pltpu.CompilerParams(
            dimension_semantics=("arbitrary",), vmem_limit_bytes=VMEM_LIMIT),
        name="inproj",
    )(x2, g, w_main, wf_parts, bf)


def _stack_heads(q):
    lane = lax.broadcasted_iota(jnp.int32, q.shape, 1)
    return jnp.concatenate(
        [jnp.where((lane // HEAD_DIM) == h, q, jnp.zeros_like(q)) for h in range(HEADS_PER_BLOCK)], axis=0)


def _unstack_heads(acc):
    tq = acc.shape[0] // HEADS_PER_BLOCK
    lane = lax.broadcasted_iota(jnp.int32, (tq, LANES), 1)
    out = acc[:tq]
    for h in range(1, HEADS_PER_BLOCK):
        out = jnp.where((lane // HEAD_DIM) == h, acc[h * tq:(h + 1) * tq], out)
    return out


def _lane_tile(x, width):
    return jnp.concatenate([x] * (width // LANES), axis=1)


class _Chain(NamedTuple):
    init: Callable
    score: Callable
    finish: Callable
    live: Callable


def _pipelined_tiles(k0, chains, diag):
    for ch in chains:
        ch.score(k0, 0, diag)

    def overlap(ch, kj, slot):
        ch.score(kj - 1, 1 - slot, False)
        ch.finish(kj, slot)

    def cond(state):
        p, gos = state[0], state[1:]
        return jnp.logical_and(p < k0 // 2, functools.reduce(jnp.logical_or, [g != 0 for g in gos]))

    def pair(state):
        p, gos = state[0], state[1:]
        kj = k0 - 2 * p
        for mask in range(1, 2 ** len(chains)):
            active = [c for i, c in enumerate(chains) if mask >> i & 1]
            preds = [(gos[i] != 0) if mask >> i & 1 else (gos[i] == 0) for i in range(len(chains))]

            @pl.when(functools.reduce(jnp.logical_and, preds))
            def _():
                for slot in (0, 1):
                    for ch in active:
                        overlap(ch, kj - slot, slot)

        new = [jnp.where(jnp.logical_and(g != 0, ch.live(kj - 1)), 1, 0).astype(jnp.int32)
               for g, ch in zip(gos, chains)]
        return (p + 1, *new)

    state = lax.while_loop(cond, pair, (jnp.int32(0),) + (jnp.int32(1),) * len(chains))

    for go, ch in zip(state[1:], chains):
        @pl.when(jnp.logical_and(go != 0, k0 % 2 == 1))
        def _():
            overlap(ch, 1, 0)
            ch.finish(0, 1)

        @pl.when(jnp.logical_and(go != 0, k0 % 2 == 0))
        def _():
            ch.finish(0, 0)


def _sb_chain(q_ref, k_ref, v_ref, acc_ref, carry_ref, lb_ref, lk_ref, rs_ref):
    qcat = _stack_heads(q_ref[...])
    m = HEADS_PER_BLOCK * TQ
    urow = lax.broadcasted_iota(jnp.int32, (TK, TK), 0)
    ucol = lax.broadcasted_iota(jnp.int32, (TK, TK), 1)
    upper = (ucol < urow).astype(jnp.bfloat16)

    def init():
        acc_ref[...] = jnp.zeros_like(acc_ref)
        carry_ref[...] = jnp.zeros_like(carry_ref)

    def score(kj, slot, diag):
        k = k_ref[pl.ds(kj * TK, TK), :]
        z = lax.dot_general(qcat, k, _NT, preferred_element_type=jnp.float32)
        t = jnp.log2(1.0 + jnp.exp2(-jnp.abs(z)))
        log_beta = jnp.minimum(z, 0.0) - t
        log_keep = log_beta - z
        if diag:
            row = lax.broadcasted_iota(jnp.int32, (m, TK), 0)
            col = lax.broadcasted_iota(jnp.int32, (m, TK), 1)
            below = col < (row % TQ)
            log_keep = jnp.where(below, log_keep, 0.0)
            log_beta = jnp.where(below, log_beta, -jnp.inf)
        lb_ref[slot] = log_beta
        lk_ref[slot] = log_keep.astype(jnp.bfloat16)
        rs_ref[slot] = jnp.broadcast_to(jnp.sum(log_keep, axis=-1, keepdims=True), (m, LANES))

    def finish(kj, slot, first=False):
        v = v_ref[pl.ds(kj * TK, TK), :]
        rest = jnp.dot(lk_ref[slot], upper, preferred_element_type=jnp.float32)
        if first:
            a = jnp.exp2(lb_ref[slot] + rest)
            acc_ref[...] = jnp.dot(a.astype(v.dtype), v, preferred_element_type=jnp.float32)
            carry_ref[...] = rs_ref[slot]
            return
        carry = carry_ref[...]
        a = jnp.exp2(lb_ref[slot] + rest + _lane_tile(carry, TK))
        acc_ref[...] += jnp.dot(a.astype(v.dtype), v, preferred_element_type=jnp.float32)
        carry_ref[...] = carry + rs_ref[slot]

    def live(kj):
        return jnp.max(carry_ref[...]) > -SKIP_LOG2

    return _Chain(init, score, finish, live)


def _fox_chain(hb, qi, q_ref, k_ref, v_ref, c_ref, ct_ref, acc_ref, m_ref, l_ref, z_ref, rm_ref,
               kn_ref):
    qcat = _stack_heads(q_ref[...])
    lane = lax.broadcasted_iota(jnp.int32, (TQ, LANES), 1)
    cblk = c_ref[...]
    cq = jnp.concatenate(
        [jnp.broadcast_to(
            jnp.sum(jnp.where(lane == hb * HEADS_PER_BLOCK + h, cblk, 0.0), axis=-1, keepdims=True),
            (TQ, TK)) for h in range(HEADS_PER_BLOCK)], axis=0)
    m = HEADS_PER_BLOCK * TQ
    ones = jnp.ones((TK, LANES), jnp.bfloat16)

    @pl.when(qi == 0)
    def _():
        kf = k_ref[...].astype(jnp.float32)
        ksq = kf * kf
        klane = lax.broadcasted_iota(jnp.int32, ksq.shape, 1)
        for h in range(HEADS_PER_BLOCK):
            n2 = jnp.sum(jnp.where((klane // HEAD_DIM) == h, ksq, 0.0), axis=-1, keepdims=True)
            kn_ref[h * TQ:(h + 1) * TQ, :] = jnp.broadcast_to(
                jnp.sqrt(jnp.max(n2, axis=0, keepdims=True)), (TQ, LANES))

    qf = qcat.astype(jnp.float32)
    qn = jnp.broadcast_to(jnp.sqrt(jnp.sum(qf * qf, axis=-1, keepdims=True)), (m, LANES))

    def init():
        acc_ref[...] = jnp.zeros_like(acc_ref)
        l_ref[...] = jnp.zeros_like(l_ref)
        m_ref[...] = jnp.full_like(m_ref, -jnp.inf)

    def score(kj, slot, diag):
        k = k_ref[pl.ds(kj * TK, TK), :]
        ck = jnp.concatenate(
            [jnp.broadcast_to(ct_ref[pl.ds(hb * HEADS_PER_BLOCK + h, 1), pl.ds(kj * TK, TK)], (TQ, TK))
             for h in range(HEADS_PER_BLOCK)], axis=0)
        z = lax.dot_general(qcat, k, _NT, preferred_element_type=jnp.float32)
        z = (z + cq) - ck
        if diag:
            row = lax.broadcasted_iota(jnp.int32, (m, TK), 0)
            col = lax.broadcasted_iota(jnp.int32, (m, TK), 1)
            z = jnp.where(col <= (row % TQ), z, -jnp.inf)
        z_ref[slot] = z
        rm_ref[slot] = jnp.broadcast_to(jnp.max(z, axis=-1, keepdims=True), (m, LANES))

    def finish(kj, slot, first=False):
        v = v_ref[pl.ds(kj * TK, TK), :]
        if first:
            m_new = rm_ref[slot]
            p = jnp.exp2(z_ref[slot] - _lane_tile(m_new, TK))
            pv = jnp.dot(p.astype(v.dtype), jnp.concatenate([v, ones], axis=1),
                         preferred_element_type=jnp.float32)
            acc_ref[...] = pv[:, :LANES]
            l_ref[...] = pv[:, LANES:]
            m_ref[...] = m_new
            return
        m_old = m_ref[...]
        m_new = jnp.maximum(m_old, rm_ref[slot])
        alpha = jnp.exp2(m_old - m_new)
        p = jnp.exp2(z_ref[slot] - _lane_tile(m_new, TK))
        pv = jnp.dot(p.astype(v.dtype), jnp.concatenate([v, ones], axis=1),
                     preferred_element_type=jnp.float32)
        acc_ref[...] = alpha * acc_ref[...] + pv[:, :LANES]
        l_ref[...] = alpha * l_ref[...] + pv[:, LANES:]
        m_ref[...] = m_new

    def live(kj):
        cb = jnp.concatenate(
            [jnp.broadcast_to(
                jnp.max(ct_ref[pl.ds(hb * HEADS_PER_BLOCK + h, 1), pl.ds(kj * TK, TK)],
                        axis=-1, keepdims=True), (TQ, LANES)) for h in range(HEADS_PER_BLOCK)], axis=0)
        bound = qn * kn_ref[...] + cq[:, :LANES] - cb - m_ref[...]
        return jnp.max(bound) > -SKIP_LOG2

    return _Chain(init, score, finish, live)


SB_HEAD_TILES = 2
FOX_HEAD_TILES = 4


def _attn_kernel(qs_ref, ks_ref, vs_ref, qf_ref, kf_ref, vf_ref, c_ref, ct_ref, osb_ref, ofx_ref,
                 sb_acc, sb_carry, sb_lb, sb_lk, sb_rs, fx_acc, fx_m, fx_l, fx_z, fx_rm, fx_kn):
    hb = pl.program_id(1)
    qi = pl.program_id(2)
    sb = _sb_chain(qs_ref, ks_ref, vs_ref, sb_acc, sb_carry, sb_lb, sb_lk, sb_rs)
    fx = _fox_chain(hb, qi, qf_ref, kf_ref, vf_ref, c_ref, ct_ref, fx_acc, fx_m, fx_l, fx_z, fx_rm, fx_kn)
    head = max(SB_HEAD_TILES, FOX_HEAD_TILES)

    @pl.when(qi < head)
    def _():
        sb.init()
        fx.init()
        _pipelined_tiles(qi, [sb, fx], True)

    @pl.when(qi >= head)
    def _():
        for i in range(head + 1):
            for ch, count in ((sb, SB_HEAD_TILES), (fx, FOX_HEAD_TILES)):
                if i < count:
                    ch.score(qi - i, i, i == 0)
                if 1 <= i <= count:
                    ch.finish(qi - i + 1, i - 1, i == 1)
        heads = ((sb, SB_HEAD_TILES), (fx, FOX_HEAD_TILES))
        alive = [ch.live(qi - count + 1) for ch, count in heads]
        for (ch, count), go in zip(heads, alive):
            @pl.when(go)
            def _():
                _pipelined_tiles(qi - count, [ch], False)

    osb_ref[...] = _unstack_heads(sb_acc[...]).astype(osb_ref.dtype)
    ofx_ref[...] = _unstack_heads(fx_acc[...] / fx_l[...]).astype(ofx_ref.dtype)


def _attention(proj3, c3, ct):
    b, s, _ = proj3.shape
    m = HEADS_PER_BLOCK * TQ

    def q_spec(blk):
        return pl.BlockSpec((None, TQ, LANES), lambda bi, hb, qi: (bi, qi, blk * N_HEAD_BLOCKS + hb))

    def kv_spec(blk):
        return pl.BlockSpec((None, s, LANES), lambda bi, hb, qi: (bi, 0, blk * N_HEAD_BLOCKS + hb))

    out_spec = pl.BlockSpec((None, TQ, LANES), lambda bi, hb, qi: (bi, qi, hb))
    out_shape = jax.ShapeDtypeStruct((b, s, ATTN_WIDTH), jnp.bfloat16)
    f32 = jnp.float32
    return pl.pallas_call(
        _attn_kernel,
        grid=(b, N_HEAD_BLOCKS, s // TQ),
        in_specs=[q_spec(0), kv_spec(1), kv_spec(2), q_spec(3), kv_spec(4), kv_spec(5),
                  pl.BlockSpec((None, TQ, LANES), lambda bi, hb, qi: (bi, qi, 0)),
                  pl.BlockSpec((N_HEADS, s), lambda bi, hb, qi: (0, bi))],
        out_specs=[out_spec, out_spec],
        out_shape=[out_shape, out_shape],
        scratch_shapes=[pltpu.VMEM((m, LANES), f32),
                        pltpu.VMEM((m, LANES), f32),
                        pltpu.VMEM((2, m, TK), f32),
                        pltpu.VMEM((2, m, TK), jnp.bfloat16),
                        pltpu.VMEM((2, m, LANES), f32),
                        pltpu.VMEM((m, LANES), f32),
                        pltpu.VMEM((m, LANES), f32),
                        pltpu.VMEM((m, LANES), f32),
                        pltpu.VMEM((FOX_HEAD_TILES, m, TK), f32),
                        pltpu.VMEM((FOX_HEAD_TILES, m, LANES), f32),
                        pltpu.VMEM((m, LANES), f32)],
        compiler_params=pltpu.CompilerParams(
            dimension_semantics=("arbitrary", "arbitrary", "arbitrary"),
            vmem_limit_bytes=VMEM_LIMIT),
        name="attn",
    )(proj3, proj3, proj3, proj3, proj3, proj3, c3, ct)


def _post_kernel(osb_ref, ofx_ref, gsb_ref, gfx_ref, x_ref, wosb_ref, wofx_ref, wout_ref,
                 gn_ref, wrh_ref, wrl_ref, br_ref,
                 x1_ref, h_ref, ids_ref, aux_ref):
    y_sb = jnp.dot(osb_ref[...], wosb_ref[...], preferred_element_type=jnp.float32)
    y_fx = jnp.dot(ofx_ref[...], wofx_ref[...], preferred_element_type=jnp.float32)
    mixed = (jax.nn.sigmoid(gsb_ref[...].astype(jnp.float32)) * y_sb
             + jax.nn.sigmoid(gfx_ref[...].astype(jnp.float32)) * y_fx)
    x1 = x_ref[...] + jnp.dot(mixed.astype(jnp.bfloat16), wout_ref[...],
                              preferred_element_type=jnp.float32)
    x1_ref[...] = x1
    y = _rmsnorm_f32(x1, gn_ref[...])
    h_hi, h_lo = _split_bf16(y, 2)
    h_ref[...] = h_hi

    lg = (lax.dot_general(wrh_ref[...], h_hi, _NT, preferred_element_type=jnp.float32)
          + lax.dot_general(wrh_ref[...], h_lo, _NT, preferred_element_type=jnp.float32)
          + lax.dot_general(wrl_ref[...], h_hi, _NT, preferred_element_type=jnp.float32))
    lg = lg + br_ref[...]
    tm = lg.shape[1]
    sub = lax.broadcasted_iota(jnp.int32, (8, tm), 0)
    neg = -jnp.inf

    gl = jnp.where(sub < N_GROUPS, lg[0:8], neg)
    gm = jnp.max(gl, axis=0, keepdims=True)
    g_w = 1.0 / jnp.sum(jnp.exp(gl - gm), axis=0, keepdims=True)
    g_idx = jnp.min(jnp.where(gl == gm, sub, 8), axis=0, keepdims=True)

    e_sel = lg[8:16]
    for g in range(1, N_GROUPS):
        e_sel = jnp.where(g_idx == g, lg[8 + 8 * g:16 + 8 * g], e_sel)
    m1 = jnp.max(e_sel, axis=0, keepdims=True)
    i1 = jnp.min(jnp.where(e_sel == m1, sub, 8), axis=0, keepdims=True)
    e_rest = jnp.where(sub == i1, neg, e_sel)
    m2 = jnp.max(e_rest, axis=0, keepdims=True)
    i2 = jnp.min(jnp.where(e_rest == m2, sub, 8), axis=0, keepdims=True)
    p2 = jnp.exp(m2 - m1)
    w1 = g_w / (1.0 + p2)
    w2 = g_w * p2 / (1.0 + p2)
    base = g_idx * EXPERTS_PER_GROUP
    ids_ref[...] = jnp.where(sub == 0, base + i1, jnp.where(sub == 1, base + i2, 0))
    dense_w = jnp.where(sub == i1, w1, jnp.where(sub == i2, w2, 0.0))
    group_hot = jnp.where(sub == g_idx, 1.0, 0.0)
    record = jnp.concatenate(
        [dense_w, group_hot, jnp.zeros((LANES - 16, tm), jnp.float32)], axis=0)
    aux_ref[...] = record.T


def _post_attention(o_sb, o_fx, proj, x2, wosb, wofx, wout, gn, wr_hi, wr_lo, br):
    n, d = x2.shape
    tm = TM_POST
    gate_blk = (3 * ATTN_WIDTH * 2) // d
    const = dict(pipeline_mode=pl.Buffered(1))
    return pl.pallas_call(
        _post_kernel,
        grid=(n // tm,),
        in_specs=[
            pl.BlockSpec((tm, ATTN_WIDTH), lambda i: (i, 0)),
            pl.BlockSpec((tm, ATTN_WIDTH), lambda i: (i, 0)),
            pl.BlockSpec((tm, d), lambda i: (i, gate_blk)),
            pl.BlockSpec((tm, d), lambda i: (i, gate_blk + 1)),
            pl.BlockSpec((tm, d), lambda i: (i, 0)),
            pl.BlockSpec((ATTN_WIDTH, d), lambda i: (0, 0), **const),
            pl.BlockSpec((ATTN_WIDTH, d), lambda i: (0, 0), **const),
            pl.BlockSpec((d, d), lambda i: (0, 0), **const),
            pl.BlockSpec((1, d), lambda i: (0, 0), **const),
            pl.BlockSpec((ROUTER_ROWS, d), lambda i: (0, 0), **const),
            pl.BlockSpec((ROUTER_ROWS, d), lambda i: (0, 0), **const),
            pl.BlockSpec((ROUTER_ROWS, 1), lambda i: (0, 0), **const),
        ],
        out_specs=[
            pl.BlockSpec((tm, d), lambda i: (i, 0)),
            pl.BlockSpec((tm, d), lambda i: (i, 0)),
            pl.BlockSpec((8, tm), lambda i: (0, i)),
            pl.BlockSpec((tm, LANES), lambda i: (i, 0)),
        ],
        out_shape=[
            jax.ShapeDtypeStruct((n, d), jnp.float32),
            jax.ShapeDtypeStruct((n, d), jnp.bfloat16),
            jax.ShapeDtypeStruct((8, n), jnp.int32),
            jax.ShapeDtypeStruct((n, LANES), jnp.float32),
        ],
        compiler_params=pltpu.CompilerParams(
            dimension_semantics=("arbitrary",), vmem_limit_bytes=VMEM_LIMIT),
        name="post_attn",
    )(o_sb, o_fx, proj, proj, x2, wosb, wofx, wout, gn, wr_hi, wr_lo, br)


TM_SORT = 256
T_GROUP = 256
ROW_ALIGN = 16
SORT_ROWS = TM_SORT + N_GROUPS * ROW_ALIGN
SMALL_CHUNK = TM_SORT // 2
GROUP_LANE0 = EXPERTS_PER_GROUP


def _max_group_tiles(n):
    padded = n + (ROW_ALIGN - 1) * N_GROUPS * (n // TM_SORT)
    return padded // T_GROUP + N_GROUPS * (TM_SORT // T_GROUP + 2)


def _sort_plan(token_group, n):
    i32 = jnp.int32
    tiles = n // TM_SORT
    hot = token_group.reshape(tiles, TM_SORT, 1) == jnp.arange(N_GROUPS, dtype=i32)
    padded = (jnp.sum(hot, axis=1, dtype=i32) + (ROW_ALIGN - 1)) // ROW_ALIGN * ROW_ALIGN
    block = jnp.cumsum(padded, axis=1) - padded
    total = jnp.sum(padded, axis=0)
    group_tiles = (total + (TM_SORT + T_GROUP - 1)) // T_GROUP
    region = group_tiles * T_GROUP
    base = jnp.cumsum(region) - region
    dest = base[None, :] + jnp.cumsum(padded, axis=0) - padded
    ends = jnp.cumsum(group_tiles)
    steps = jnp.arange(_max_group_tiles(n), dtype=i32)
    tile_group = jnp.minimum(jnp.sum(steps[:, None] >= ends[None, :], axis=1), N_GROUPS - 1)
    return dict(block=block.reshape(-1).astype(i32), dest=dest.reshape(-1).astype(i32),
                small=(padded <= SMALL_CHUNK).reshape(-1).astype(i32),
                zero_from=(base + total).astype(i32), zero_to=(base + region).astype(i32),
                used_tiles=ends[-1:].astype(i32), tile_group=tile_group.astype(i32))


def _sort_kernel(block_ref, dest_ref, small_ref, zfrom_ref, zto_ref, used_ref,
                 h_ref, ids_ref, aux_ref, hg_ref, wg_ref, hbuf, wbuf, sems):
    i = pl.program_id(0)
    tm = TM_SORT
    rows = SORT_ROWS
    slot = i % 2

    def copies(g, slot, src, off, nrows):
        return (pltpu.make_async_copy(hbuf.at[slot, pl.ds(src, nrows)], hg_ref.at[pl.ds(off, nrows)],
                                      sems.at[0, g]),
                pltpu.make_async_copy(wbuf.at[slot, pl.ds(src, nrows)], wg_ref.at[pl.ds(off, nrows)],
                                      sems.at[1, g]))

    def tile_chunks(step, slot, act):
        for g in range(N_GROUPS):
            idx = step * N_GROUPS + g
            src = pl.multiple_of(block_ref[idx], ROW_ALIGN)
            off = pl.multiple_of(dest_ref[idx], ROW_ALIGN)
            for nrows, is_small in ((SMALL_CHUNK, True), (tm, False)):
                @pl.when((small_ref[idx] != 0) == is_small)
                def _():
                    for c in copies(g, slot, src, off, nrows):
                        act(c)

    def zero_rows(off, g):
        for c in copies(g, 0, 0, pl.multiple_of(off, ROW_ALIGN), tm):
            c.start()
        for c in copies(g, 0, 0, pl.multiple_of(off, ROW_ALIGN), tm):
            c.wait()

    @pl.when(i == 0)
    def _():
        hbuf[...] = jnp.zeros_like(hbuf)
        wbuf[...] = jnp.zeros_like(wbuf)

    gid = ids_ref[0:1, :] // EXPERTS_PER_GROUP
    sub = lax.broadcasted_iota(jnp.int32, (8, tm), 0)
    hot = sub == gid
    r_io = lax.broadcasted_iota(jnp.int32, (tm, tm), 0)
    c_io = lax.broadcasted_iota(jnp.int32, (tm, tm), 1)
    before = (r_io < c_io).astype(jnp.bfloat16)
    seen = jnp.dot(jnp.where(hot, 1.0, 0.0).astype(jnp.bfloat16), before,
                   preferred_element_type=jnp.float32)
    pos = jnp.sum(jnp.where(hot, seen, 0.0), axis=0, keepdims=True).astype(jnp.int32)
    for g in range(N_GROUPS):
        pos = pos + jnp.where(gid == g, block_ref[i * N_GROUPS + g], 0)

    s_io = lax.broadcasted_iota(jnp.int32, (rows, tm), 0)
    sel = jnp.where(pos == s_io, 1.0, 0.0).astype(jnp.bfloat16)
    hbuf[slot, 0:rows] = jnp.dot(sel, h_ref[...], preferred_element_type=jnp.float32).astype(hbuf.dtype)
    aux3 = jnp.concatenate(_split_bf16(aux_ref[...], 3), axis=1)
    w3 = jnp.dot(sel, aux3, preferred_element_type=jnp.float32)
    wbuf[slot, 0:rows] = w3[:, :LANES] + w3[:, LANES:2 * LANES] + w3[:, 2 * LANES:]

    @pl.when(i > 0)
    def _():
        tile_chunks(i - 1, 1 - slot, lambda c: c.wait())

    tile_chunks(i, slot, lambda c: c.start())

    @pl.when(i == pl.num_programs(0) - 1)
    def _():
        tile_chunks(i, slot, lambda c: c.wait())
        hbuf[...] = jnp.zeros_like(hbuf)
        wbuf[...] = jnp.zeros_like(wbuf)
        for g in range(N_GROUPS):
            zero_rows(zfrom_ref[g], g)
            zero_rows(zto_ref[g] - tm, g)

        def tail(t, _):
            zero_rows(t * T_GROUP, 0)
            return 0

        lax.fori_loop(used_ref[0], hg_ref.shape[0] // T_GROUP, tail, 0)


def _moe_sort(plan, h, ids, aux):
    n, d = h.shape
    tm = TM_SORT
    rows = _max_group_tiles(n) * T_GROUP
    grid_spec = pltpu.PrefetchScalarGridSpec(
        num_scalar_prefetch=6,
        grid=(n // tm,),
        in_specs=[
            pl.BlockSpec((tm, d), lambda i, *_: (i, 0)),
            pl.BlockSpec((8, tm), lambda i, *_: (0, i)),
            pl.BlockSpec((tm, LANES), lambda i, *_: (i, 0)),
        ],
        out_specs=[pl.BlockSpec(memory_space=pl.ANY), pl.BlockSpec(memory_space=pl.ANY)],
        scratch_shapes=[pltpu.VMEM((2, SORT_ROWS + tm, d), jnp.bfloat16),
                        pltpu.VMEM((2, SORT_ROWS + tm, LANES), jnp.float32),
                        pltpu.SemaphoreType.DMA((2, N_GROUPS))],
    )
    return pl.pallas_call(
        _sort_kernel,
        grid_spec=grid_spec,
        out_shape=[jax.ShapeDtypeStruct((rows, d), jnp.bfloat16),
                   jax.ShapeDtypeStruct((rows, LANES), jnp.float32)],
        compiler_params=pltpu.CompilerParams(
            dimension_semantics=("arbitrary",), vmem_limit_bytes=VMEM_LIMIT),
        name="moe_sort",
    )(plan["block"], plan["dest"], plan["small"], plan["zero_from"], plan["zero_to"],
      plan["used_tiles"], h, ids, aux)


def _experts_kernel(tg_ref, used_ref, h_ref, w_ref, w1_ref, w3_ref, w2_ref, y_ref):
    i = pl.program_id(0)

    @pl.when(i < used_ref[0])
    def _():
        h = h_ref[...]
        w = w_ref[...]
        hidden = []
        for e in range(EXPERTS_PER_GROUP):
            a = jnp.dot(h, w1_ref[e], preferred_element_type=jnp.float32)
            b = jnp.dot(h, w3_ref[e], preferred_element_type=jnp.float32)
            hidden.append(((a * jax.nn.sigmoid(a)) * b * w[:, e:e + 1]).astype(jnp.bfloat16))
        ne, de, d = w2_ref.shape
        y = jnp.dot(jnp.concatenate(hidden, axis=1), w2_ref[...].reshape(ne * de, d),
                    preferred_element_type=jnp.float32)
        y_ref[...] = y.astype(y_ref.dtype)

    @pl.when(i >= used_ref[0])
    def _():
        y_ref[...] = jnp.zeros_like(y_ref)


def _moe_experts(plan, hg, wg, w1, w3, w2):
    rows, d = hg.shape
    t = T_GROUP
    de = w1.shape[-1]

    def row_tile(i, tg, used):
        return (jnp.minimum(i, used[0] - 1), 0)

    def group_weights(i, tg, used):
        return (tg[i], 0, 0)

    grid_spec = pltpu.PrefetchScalarGridSpec(
        num_scalar_prefetch=2,
        grid=(rows // t,),
        in_specs=[
            pl.BlockSpec((t, d), row_tile),
            pl.BlockSpec((t, LANES), row_tile),
            pl.BlockSpec((EXPERTS_PER_GROUP, d, de), group_weights),
            pl.BlockSpec((EXPERTS_PER_GROUP, d, de), group_weights),
            pl.BlockSpec((EXPERTS_PER_GROUP, de, d), group_weights),
        ],
        out_specs=pl.BlockSpec((t, d), lambda i, tg, used: (i, 0)),
    )
    return pl.pallas_call(
        _experts_kernel,
        grid_spec=grid_spec,
        out_shape=jax.ShapeDtypeStruct((rows, d), jnp.bfloat16),
        compiler_params=pltpu.CompilerParams(
            dimension_semantics=("arbitrary",), vmem_limit_bytes=VMEM_LIMIT),
        name="moe_experts",
    )(plan["tile_group"], plan["used_tiles"], hg, wg, w1, w3, w2)


def _unsort_kernel(dest_ref, small_ref, x1_ref, aux_ref, gn_ref, yg_ref, o_ref, ybuf, sems):
    i = pl.program_id(0)
    tm = TM_SORT
    slot = i % 2

    def chunks(step, slot, act):
        for g in range(N_GROUPS):
            idx = step * N_GROUPS + g
            off = pl.multiple_of(dest_ref[idx], ROW_ALIGN)
            for nrows, is_small in ((SMALL_CHUNK, True), (tm, False)):
                @pl.when((small_ref[idx] != 0) == is_small)
                def _():
                    act(pltpu.make_async_copy(yg_ref.at[pl.ds(off, nrows)],
                                              ybuf.at[slot, g, pl.ds(0, nrows)], sems.at[slot, g]))

    @pl.when(i == 0)
    def _():
        ybuf[...] = jnp.zeros_like(ybuf)
        chunks(0, 0, lambda c: c.start())

    @pl.when(i + 1 < pl.num_programs(0))
    def _():
        chunks(i + 1, 1 - slot, lambda c: c.start())

    chunks(i, slot, lambda c: c.wait())

    aux = aux_ref[...]
    lane = lax.broadcasted_iota(jnp.int32, (tm, LANES), 1)
    is_group_lane = jnp.logical_and(lane >= GROUP_LANE0, lane < GROUP_LANE0 + N_GROUPS)
    hot = jnp.where(is_group_lane, aux, 0.0)
    r_io = lax.broadcasted_iota(jnp.int32, (tm, tm), 0)
    c_io = lax.broadcasted_iota(jnp.int32, (tm, tm), 1)
    before = (c_io < r_io).astype(jnp.bfloat16)
    seen = jnp.dot(before, hot.astype(jnp.bfloat16), preferred_element_type=jnp.float32)
    rank = jnp.sum(hot * seen, axis=-1, keepdims=True).astype(jnp.int32)
    y = x1_ref[...]
    for g in range(N_GROUPS):
        mine = aux[:, GROUP_LANE0 + g:GROUP_LANE0 + g + 1] > 0.5
        sel = jnp.where(jnp.logical_and(rank == c_io, mine), 1.0, 0.0).astype(jnp.bfloat16)
        y = y + jnp.dot(sel, ybuf[slot, g], preferred_element_type=jnp.float32)
    o_ref[...] = _rmsnorm_f32(y, gn_ref[...])


def _moe_unsort(plan, x1, aux, gn, yg):
    n, d = x1.shape
    tm = TM_SORT
    grid_spec = pltpu.PrefetchScalarGridSpec(
        num_scalar_prefetch=2,
        grid=(n // tm,),
        in_specs=[
            pl.BlockSpec((tm, d), lambda i, *_: (i, 0)),
            pl.BlockSpec((tm, LANES), lambda i, *_: (i, 0)),
            pl.BlockSpec((1, d), lambda i, *_: (0, 0)),
            pl.BlockSpec(memory_space=pl.ANY),
        ],
        out_specs=pl.BlockSpec((tm, d), lambda i, *_: (i, 0)),
        scratch_shapes=[pltpu.VMEM((2, N_GROUPS, tm, d), jnp.bfloat16),
                        pltpu.SemaphoreType.DMA((2, N_GROUPS))],
    )
    return pl.pallas_call(
        _unsort_kernel,
        grid_spec=grid_spec,
        out_shape=jax.ShapeDtypeStruct((n, d), jnp.float32),
        compiler_params=pltpu.CompilerParams(
            dimension_semantics=("arbitrary",), vmem_limit_bytes=VMEM_LIMIT),
        name="moe_unsort",
    )(plan["dest"], plan["small"], x1, aux, gn, yg)


def kernel(x, norm_attn, w_in, b_forget, w_o_sb, w_o_fox, w_out, norm_ffn, w_router_group,
           b_router_group, w_router_expert, b_router_expert, w1, w3, w2, norm_final):
    b, s, d = x.shape
    n = b * s
    depth = w_in.shape[0]
    assert depth == 1, "the final norm is fused into the MoE un-sort of a single layer"
    assert s % TM_PROJ == 0 and s % TQ == 0 and n % TM_SORT == 0 and T_GROUP == TM_SORT
    bf16 = jnp.bfloat16
    n_main = 6 * ATTN_WIDTH
    x2 = x.reshape(n, d)
    for l in range(depth):
        w_l = w_in[l]
        w_main = jnp.concatenate([w_l[:, :n_main], w_l[:, n_main + N_HEADS:]], axis=1).astype(bf16)
        wf = jnp.pad(w_l[:, n_main:n_main + N_HEADS], ((0, 0), (0, LANES - N_HEADS)))
        wf_hi = wf.astype(bf16)
        wf_parts = jnp.concatenate([wf_hi, (wf - wf_hi.astype(jnp.float32)).astype(bf16)], axis=1)
        bf = jnp.pad(b_forget[l], (0, LANES - N_HEADS)).reshape(1, LANES)
        wr = jnp.concatenate([w_router_group[l].T, jnp.zeros((8 - N_GROUPS, d), jnp.float32),
                              w_router_expert[l].T], axis=0)
        wr_hi = wr.astype(bf16)
        wr_lo = (wr - wr_hi.astype(jnp.float32)).astype(bf16)
        br = jnp.concatenate([b_router_group[l], jnp.zeros((8 - N_GROUPS,), jnp.float32),
                              b_router_expert[l]]).reshape(ROUTER_ROWS, 1)

        proj, c, ct = _inproj(x2, norm_attn[l].reshape(1, d), w_main, wf_parts, bf, s)
        proj3 = proj.reshape(b, s, proj.shape[1])
        o_sb, o_fx = _attention(proj3, c.reshape(b, s, LANES), ct)
        x1, h, ids, aux = _post_attention(
            o_sb.reshape(n, ATTN_WIDTH), o_fx.reshape(n, ATTN_WIDTH), proj, x2,
            w_o_sb[l].astype(bf16), w_o_fox[l].astype(bf16), w_out[l].astype(bf16),
            norm_ffn[l].reshape(1, d), wr_hi, wr_lo, br)
        plan = _sort_plan(ids[0] // EXPERTS_PER_GROUP, n)
        hg, wg = _moe_sort(plan, h, ids, aux)
        yg = _moe_experts(plan, hg, wg, w1[l].astype(bf16), w3[l].astype(bf16), w2[l].astype(bf16))
        x2 = _moe_unsort(plan, x1, aux, norm_final.reshape(1, d), yg)
    return x2.reshape(b, s, d)
```

```python
import functools
import math
from typing import Callable, NamedTuple

import jax
import jax.numpy as jnp
from jax import lax
from jax.experimental import pallas as pl
from jax.experimental.pallas import tpu as pltpu

HEAD_DIM = 64
N_HEADS = 8
ATTN_WIDTH = N_HEADS * HEAD_DIM
N_GROUPS = 4
EXPERTS_PER_GROUP = 8
N_EXPERTS = N_GROUPS * EXPERTS_PER_GROUP
RMS_EPS = 1e-6
LANES = 128
HEADS_PER_BLOCK = LANES // HEAD_DIM
N_HEAD_BLOCKS = N_HEADS // HEADS_PER_BLOCK
ROUTER_ROWS = 8 + N_EXPERTS
VMEM_LIMIT = 56 * 1024 * 1024

TM_PROJ = 512
TQ = 256
TK = 256
TM_POST = 512

_NT = (((1,), (1,)), ((), ()))
LOG2E = math.log2(math.e)
Q_SCALE = LOG2E / math.sqrt(HEAD_DIM)
SKIP_LOG2 = 160.0


def _split_bf16(v, parts):
    out = []
    r = v
    for i in range(parts):
        p = r.astype(jnp.bfloat16)
        out.append(p)
        if i + 1 < parts:
            r = r - p.astype(jnp.float32)
    return out


def _rmsnorm_f32(x, g):
    ms = jnp.mean(x * x, axis=-1, keepdims=True)
    return x * lax.rsqrt(ms + RMS_EPS) * g


def _inproj_kernel(x_ref, g_ref, w_ref, wf_ref, bf_ref,
                   proj_ref, c_ref, ct_ref, carry_ref, *, tiles_per_seq, n_chunks, chunk, q_chunks):
    i = pl.program_id(0)

    @pl.when(i % tiles_per_seq == 0)
    def _():
        carry_ref[...] = jnp.zeros_like(carry_ref)

    y = _rmsnorm_f32(x_ref[...], g_ref[...])
    h_hi, h_lo = _split_bf16(y, 2)
    for c in range(n_chunks):
        sl = slice(c * chunk, (c + 1) * chunk)
        p = jnp.dot(h_hi, w_ref[:, sl], preferred_element_type=jnp.float32)
        if c in q_chunks:
            p = p * Q_SCALE
        proj_ref[:, sl] = p.astype(proj_ref.dtype)

    ff = jnp.dot(h_hi, wf_ref[...], preferred_element_type=jnp.float32)
    f = (ff[:, :LANES] + ff[:, LANES:]
         + jnp.dot(h_lo, wf_ref[:, :LANES], preferred_element_type=jnp.float32))
    f = f + bf_ref[...]
    logf = (jnp.minimum(f, 0.0) - jnp.log1p(jnp.exp(-jnp.abs(f)))) * LOG2E

    tm = logf.shape[0]
    row = lax.broadcasted_iota(jnp.int32, (tm, tm), 0)
    col = lax.broadcasted_iota(jnp.int32, (tm, tm), 1)
    tri = (col <= row).astype(jnp.bfloat16)
    parts = jnp.dot(tri, jnp.concatenate(_split_bf16(logf, 3), axis=1),
                    preferred_element_type=jnp.float32)
    cum = carry_ref[...] + parts[:, :LANES] + parts[:, LANES:2 * LANES] + parts[:, 2 * LANES:]
    c_ref[...] = cum
    ct_ref[...] = cum.T[:N_HEADS, :]
    carry_ref[...] = cum[tm - 1:tm, :]


def _inproj(x2, g, w_main, wf_parts, bf, seq):
    n, d = x2.shape
    cols = w_main.shape[1]
    tm = TM_PROJ
    chunk = ATTN_WIDTH
    kern = functools.partial(_inproj_kernel, tiles_per_seq=seq // tm,
                             n_chunks=cols // chunk, chunk=chunk, q_chunks=(0, 3))
    const = dict(pipeline_mode=pl.Buffered(1))
    return pl.pallas_call(
        kern,
        grid=(n // tm,),
        in_specs=[
            pl.BlockSpec((tm, d), lambda i: (i, 0)),
            pl.BlockSpec((1, d), lambda i: (0, 0), **const),
            pl.BlockSpec((d, cols), lambda i: (0, 0), **const),
            pl.BlockSpec((d, 2 * LANES), lambda i: (0, 0), **const),
            pl.BlockSpec((1, LANES), lambda i: (0, 0), **const),
        ],
        out_specs=[
            pl.BlockSpec((tm, cols), lambda i: (i, 0)),
            pl.BlockSpec((tm, LANES), lambda i: (i, 0)),
            pl.BlockSpec((N_HEADS, tm), lambda i: (0, i)),
        ],
        out_shape=[
            jax.ShapeDtypeStruct((n, cols), jnp.bfloat16),
            jax.ShapeDtypeStruct((n, LANES), jnp.float32),
            jax.ShapeDtypeStruct((N_HEADS, n), jnp.float32),
        ],
        scratch_shapes=[pltpu.VMEM((1, LANES), jnp.float32)],
        compiler_params=pltpu.CompilerParams(
            dimension_semantics=("arbitrary",), vmem_limit_bytes=VMEM_LIMIT),
        name="inproj",
    )(x2, g, w_main, wf_parts, bf)


def _stack_heads(q):
    lane = lax.broadcasted_iota(jnp.int32, q.shape, 1)
    return jnp.concatenate(
        [jnp.where((lane // HEAD_DIM) == h, q, jnp.zeros_like(q)) for h in range(HEADS_PER_BLOCK)], axis=0)


def _unstack_heads(acc):
    tq = acc.shape[0] // HEADS_PER_BLOCK
    lane = lax.broadcasted_iota(jnp.int32, (tq, LANES), 1)
    out = acc[:tq]
    for h in range(1, HEADS_PER_BLOCK):
        out = jnp.where((lane // HEAD_DIM) == h, acc[h * tq:(h + 1) * tq], out)
    return out


def _lane_tile(x, width):
    return jnp.concatenate([x] * (width // LANES), axis=1)


class _Chain(NamedTuple):
    init: Callable
    score: Callable
    finish: Callable
    live: Callable


def _pipelined_tiles(k0, chains, diag):
    for ch in chains:
        ch.score(k0, 0, diag)

    def overlap(ch, kj, slot):
        ch.score(kj - 1, 1 - slot, False)
        ch.finish(kj, slot)

    def cond(state):
        p, gos = state[0], state[1:]
        return jnp.logical_and(p < k0 // 2, functools.reduce(jnp.logical_or, [g != 0 for g in gos]))

    def pair(state):
        p, gos = state[0], state[1:]
        kj = k0 - 2 * p
        for mask in range(1, 2 ** len(chains)):
            active = [c for i, c in enumerate(chains) if mask >> i & 1]
            preds = [(gos[i] != 0) if mask >> i & 1 else (gos[i] == 0) for i in range(len(chains))]

            @pl.when(functools.reduce(jnp.logical_and, preds))
            def _():
                for slot in (0, 1):
                    for ch in active:
                        overlap(ch, kj - slot, slot)

        new = [jnp.where(jnp.logical_and(g != 0, ch.live(kj - 1)), 1, 0).astype(jnp.int32)
               for g, ch in zip(gos, chains)]
        return (p + 1, *new)

    state = lax.while_loop(cond, pair, (jnp.int32(0),) + (jnp.int32(1),) * len(chains))

    for go, ch in zip(state[1:], chains):
        @pl.when(jnp.logical_and(go != 0, k0 % 2 == 1))
        def _():
            overlap(ch, 1, 0)
            ch.finish(0, 1)

        @pl.when(jnp.logical_and(go != 0, k0 % 2 == 0))
        def _():
            ch.finish(0, 0)


def _sb_chain(q_ref, k_ref, v_ref, acc_ref, carry_ref, lb_ref, lk_ref, rs_ref):
    qcat = _stack_heads(q_ref[...])
    m = HEADS_PER_BLOCK * TQ
    urow = lax.broadcasted_iota(jnp.int32, (TK, TK), 0)
    ucol = lax.broadcasted_iota(jnp.int32, (TK, TK), 1)
    upper = (ucol < urow).astype(jnp.bfloat16)

    def init():
        acc_ref[...] = jnp.zeros_like(acc_ref)
        carry_ref[...] = jnp.zeros_like(carry_ref)

    def score(kj, slot, diag):
        k = k_ref[pl.ds(kj * TK, TK), :]
        z = lax.dot_general(qcat, k, _NT, preferred_element_type=jnp.float32)
        t = jnp.log2(1.0 + jnp.exp2(-jnp.abs(z)))
        log_beta = jnp.minimum(z, 0.0) - t
        log_keep = log_beta - z
        if diag:
            row = lax.broadcasted_iota(jnp.int32, (m, TK), 0)
            col = lax.broadcasted_iota(jnp.int32, (m, TK), 1)
            below = col < (row % TQ)
            log_keep = jnp.where(below, log_keep, 0.0)
            log_beta = jnp.where(below, log_beta, -jnp.inf)
        lb_ref[slot] = log_beta
        lk_ref[slot] = log_keep.astype(jnp.bfloat16)
        rs_ref[slot] = jnp.broadcast_to(jnp.sum(log_keep, axis=-1, keepdims=True), (m, LANES))

    def finish(kj, slot, first=False):
        v = v_ref[pl.ds(kj * TK, TK), :]
        rest = jnp.dot(lk_ref[slot], upper, preferred_element_type=jnp.float32)
        if first:
            a = jnp.exp2(lb_ref[slot] + rest)
            acc_ref[...] = jnp.dot(a.astype(v.dtype), v, preferred_element_type=jnp.float32)
            carry_ref[...] = rs_ref[slot]
            return
        carry = carry_ref[...]
        a = jnp.exp2(lb_ref[slot] + rest + _lane_tile(carry, TK))
        acc_ref[...] += jnp.dot(a.astype(v.dtype), v, preferred_element_type=jnp.float32)
        carry_ref[...] = carry + rs_ref[slot]

    def live(kj):
        return jnp.max(carry_ref[...]) > -SKIP_LOG2

    return _Chain(init, score, finish, live)


def _fox_chain(hb, qi, q_ref, k_ref, v_ref, c_ref, ct_ref, acc_ref, m_ref, l_ref, z_ref, rm_ref,
               kn_ref):
    qcat = _stack_heads(q_ref[...])
    lane = lax.broadcasted_iota(jnp.int32, (TQ, LANES), 1)
    cblk = c_ref[...]
    cq = jnp.concatenate(
        [jnp.broadcast_to(
            jnp.sum(jnp.where(lane == hb * HEADS_PER_BLOCK + h, cblk, 0.0), axis=-1, keepdims=True),
            (TQ, TK)) for h in range(HEADS_PER_BLOCK)], axis=0)
    m = HEADS_PER_BLOCK * TQ
    ones = jnp.ones((TK, LANES), jnp.bfloat16)

    @pl.when(qi == 0)
    def _():
        kf = k_ref[...].astype(jnp.float32)
        ksq = kf * kf
        klane = lax.broadcasted_iota(jnp.int32, ksq.shape, 1)
        for h in range(HEADS_PER_BLOCK):
            n2 = jnp.sum(jnp.where((klane // HEAD_DIM) == h, ksq, 0.0), axis=-1, keepdims=True)
            kn_ref[h * TQ:(h + 1) * TQ, :] = jnp.broadcast_to(
                jnp.sqrt(jnp.max(n2, axis=0, keepdims=True)), (TQ, LANES))

    qf = qcat.astype(jnp.float32)
    qn = jnp.broadcast_to(jnp.sqrt(jnp.sum(qf * qf, axis=-1, keepdims=True)), (m, LANES))

    def init():
        acc_ref[...] = jnp.zeros_like(acc_ref)
        l_ref[...] = jnp.zeros_like(l_ref)
        m_ref[...] = jnp.full_like(m_ref, -jnp.inf)

    def score(kj, slot, diag):
        k = k_ref[pl.ds(kj * TK, TK), :]
        ck = jnp.concatenate(
            [jnp.broadcast_to(ct_ref[pl.ds(hb * HEADS_PER_BLOCK + h, 1), pl.ds(kj * TK, TK)], (TQ, TK))
             for h in range(HEADS_PER_BLOCK)], axis=0)
        z = lax.dot_general(qcat, k, _NT, preferred_element_type=jnp.float32)
        z = (z + cq) - ck
        if diag:
            row = lax.broadcasted_iota(jnp.int32, (m, TK), 0)
            col = lax.broadcasted_iota(jnp.int32, (m, TK), 1)
            z = jnp.where(col <= (row % TQ), z, -jnp.inf)
        z_ref[slot] = z
        rm_ref[slot] = jnp.broadcast_to(jnp.max(z, axis=-1, keepdims=True), (m, LANES))

    def finish(kj, slot, first=False):
        v = v_ref[pl.ds(kj * TK, TK), :]
        if first:
            m_new = rm_ref[slot]
            p = jnp.exp2(z_ref[slot] - _lane_tile(m_new, TK))
            pv = jnp.dot(p.astype(v.dtype), jnp.concatenate([v, ones], axis=1),
                         preferred_element_type=jnp.float32)
            acc_ref[...] = pv[:, :LANES]
            l_ref[...] = pv[:, LANES:]
            m_ref[...] = m_new
            return
        m_old = m_ref[...]
        m_new = jnp.maximum(m_old, rm_ref[slot])
        alpha = jnp.exp2(m_old - m_new)
        p = jnp.exp2(z_ref[slot] - _lane_tile(m_new, TK))
        pv = jnp.dot(p.astype(v.dtype), jnp.concatenate([v, ones], axis=1),
                     preferred_element_type=jnp.float32)
        acc_ref[...] = alpha * acc_ref[...] + pv[:, :LANES]
        l_ref[...] = alpha * l_ref[...] + pv[:, LANES:]
        m_ref[...] = m_new

    def live(kj):
        cb = jnp.concatenate(
            [jnp.broadcast_to(
                jnp.max(ct_ref[pl.ds(hb * HEADS_PER_BLOCK + h, 1), pl.ds(kj * TK, TK)],
                        axis=-1, keepdims=True), (TQ, LANES)) for h in range(HEADS_PER_BLOCK)], axis=0)
        bound = qn * kn_ref[...] + cq[:, :LANES] - cb - m_ref[...]
        return jnp.max(bound) > -SKIP_LOG2

    return _Chain(init, score, finish, live)


SB_HEAD_TILES = 2
FOX_HEAD_TILES = 4


def _attn_kernel(qs_ref, ks_ref, vs_ref, qf_ref, kf_ref, vf_ref, c_ref, ct_ref, osb_ref, ofx_ref,
                 sb_acc, sb_carry, sb_lb, sb_lk, sb_rs, fx_acc, fx_m, fx_l, fx_z, fx_rm, fx_kn):
    hb = pl.program_id(1)
    qi = pl.program_id(2)
    sb = _sb_chain(qs_ref, ks_ref, vs_ref, sb_acc, sb_carry, sb_lb, sb_lk, sb_rs)
    fx = _fox_chain(hb, qi, qf_ref, kf_ref, vf_ref, c_ref, ct_ref, fx_acc, fx_m, fx_l, fx_z, fx_rm, fx_kn)
    head = max(SB_HEAD_TILES, FOX_HEAD_TILES)

    @pl.when(qi < head)
    def _():
        sb.init()
        fx.init()
        _pipelined_tiles(qi, [sb, fx], True)

    @pl.when(qi >= head)
    def _():
        for i in range(head + 1):
            for ch, count in ((sb, SB_HEAD_TILES), (fx, FOX_HEAD_TILES)):
                if i < count:
                    ch.score(qi - i, i, i == 0)
                if 1 <= i <= count:
                    ch.finish(qi - i + 1, i - 1, i == 1)
        heads = ((sb, SB_HEAD_TILES), (fx, FOX_HEAD_TILES))
        alive = [ch.live(qi - count + 1) for ch, count in heads]
        for (ch, count), go in zip(heads, alive):
            @pl.when(go)
            def _():
                k1 = qi - count
                ch.score(k1, 0, False)
                ch.finish(k1, 0)

                @pl.when(jnp.logical_and(k1 >= 1, ch.live(k1)))
                def _():
                    _pipelined_tiles(k1 - 1, [ch], False)

    osb_ref[...] = _unstack_heads(sb_acc[...]).astype(osb_ref.dtype)
    ofx_ref[...] = _unstack_heads(fx_acc[...] / fx_l[...]).astype(ofx_ref.dtype)


def _attention(proj3, c3, ct):
    b, s, _ = proj3.shape
    m = HEADS_PER_BLOCK * TQ

    def q_spec(blk):
        return pl.BlockSpec((None, TQ, LANES), lambda bi, hb, qi: (bi, qi, blk * N_HEAD_BLOCKS + hb))

    def kv_spec(blk):
        return pl.BlockSpec((None, s, LANES), lambda bi, hb, qi: (bi, 0, blk * N_HEAD_BLOCKS + hb))

    out_spec = pl.BlockSpec((None, TQ, LANES), lambda bi, hb, qi: (bi, qi, hb))
    out_shape = jax.ShapeDtypeStruct((b, s, ATTN_WIDTH), jnp.bfloat16)
    f32 = jnp.float32
    return pl.pallas_call(
        _attn_kernel,
        grid=(b, N_HEAD_BLOCKS, s // TQ),
        in_specs=[q_spec(0), kv_spec(1), kv_spec(2), q_spec(3), kv_spec(4), kv_spec(5),
                  pl.BlockSpec((None, TQ, LANES), lambda bi, hb, qi: (bi, qi, 0)),
                  pl.BlockSpec((N_HEADS, s), lambda bi, hb, qi: (0, bi))],
        out_specs=[out_spec, out_spec],
        out_shape=[out_shape, out_shape],
        scratch_shapes=[pltpu.VMEM((m, LANES), f32),
                        pltpu.VMEM((m, LANES), f32),
                        pltpu.VMEM((2, m, TK), f32),
                        pltpu.VMEM((2, m, TK), jnp.bfloat16),
                        pltpu.VMEM((2, m, LANES), f32),
                        pltpu.VMEM((m, LANES), f32),
                        pltpu.VMEM((m, LANES), f32),
                        pltpu.VMEM((m, LANES), f32),
                        pltpu.VMEM((FOX_HEAD_TILES, m, TK), f32),
                        pltpu.VMEM((FOX_HEAD_TILES, m, LANES), f32),
                        pltpu.VMEM((m, LANES), f32)],
        compiler_params=pltpu.CompilerParams(
            dimension_semantics=("arbitrary", "arbitrary", "arbitrary"),
            vmem_limit_bytes=VMEM_LIMIT),
        name="attn",
    )(proj3, proj3, proj3, proj3, proj3, proj3, c3, ct)


def _post_kernel(osb_ref, ofx_ref, gsb_ref, gfx_ref, x_ref, wosb_ref, wofx_ref, wout_ref,
                 gn_ref, wrh_ref, wrl_ref, br_ref,
                 x1_ref, h_ref, ids_ref, aux_ref):
    y_sb = jnp.dot(osb_ref[...], wosb_ref[...], preferred_element_type=jnp.float32)
    y_fx = jnp.dot(ofx_ref[...], wofx_ref[...], preferred_element_type=jnp.float32)
    mixed = (jax.nn.sigmoid(gsb_ref[...].astype(jnp.float32)) * y_sb
             + jax.nn.sigmoid(gfx_ref[...].astype(jnp.float32)) * y_fx)
    x1 = x_ref[...] + jnp.dot(mixed.astype(jnp.bfloat16), wout_ref[...],
                              preferred_element_type=jnp.float32)
    x1_ref[...] = x1
    y = _rmsnorm_f32(x1, gn_ref[...])
    h_hi, h_lo = _split_bf16(y, 2)
    h_ref[...] = h_hi

    lg = (lax.dot_general(wrh_ref[...], h_hi, _NT, preferred_element_type=jnp.float32)
          + lax.dot_general(wrh_ref[...], h_lo, _NT, preferred_element_type=jnp.float32)
          + lax.dot_general(wrl_ref[...], h_hi, _NT, preferred_element_type=jnp.float32))
    lg = lg + br_ref[...]
    tm = lg.shape[1]
    sub = lax.broadcasted_iota(jnp.int32, (8, tm), 0)
    neg = -jnp.inf

    gl = jnp.where(sub < N_GROUPS, lg[0:8], neg)
    gm = jnp.max(gl, axis=0, keepdims=True)
    g_w = 1.0 / jnp.sum(jnp.exp(gl - gm), axis=0, keepdims=True)
    g_idx = jnp.min(jnp.where(gl == gm, sub, 8), axis=0, keepdims=True)

    e_sel = lg[8:16]
    for g in range(1, N_GROUPS):
        e_sel = jnp.where(g_idx == g, lg[8 + 8 * g:16 + 8 * g], e_sel)
    m1 = jnp.max(e_sel, axis=0, keepdims=True)
    i1 = jnp.min(jnp.where(e_sel == m1, sub, 8), axis=0, keepdims=True)
    e_rest = jnp.where(sub == i1, neg, e_sel)
    m2 = jnp.max(e_rest, axis=0, keepdims=True)
    i2 = jnp.min(jnp.where(e_rest == m2, sub, 8), axis=0, keepdims=True)
    p2 = jnp.exp(m2 - m1)
    w1 = g_w / (1.0 + p2)
    w2 = g_w * p2 / (1.0 + p2)
    base = g_idx * EXPERTS_PER_GROUP
    ids_ref[...] = jnp.where(sub == 0, base + i1, jnp.where(sub == 1, base + i2, 0))
    dense_w = jnp.where(sub == i1, w1, jnp.where(sub == i2, w2, 0.0))
    group_hot = jnp.where(sub == g_idx, 1.0, 0.0)
    record = jnp.concatenate(
        [dense_w, group_hot, jnp.zeros((LANES - 16, tm), jnp.float32)], axis=0)
    aux_ref[...] = record.T


def _post_attention(o_sb, o_fx, proj, x2, wosb, wofx, wout, gn, wr_hi, wr_lo, br):
    n, d = x2.shape
    tm = TM_POST
    gate_blk = (3 * ATTN_WIDTH * 2) // d
    const = dict(pipeline_mode=pl.Buffered(1))
    return pl.pallas_call(
        _post_kernel,
        grid=(n // tm,),
        in_specs=[
            pl.BlockSpec((tm, ATTN_WIDTH), lambda i: (i, 0)),
            pl.BlockSpec((tm, ATTN_WIDTH), lambda i: (i, 0)),
            pl.BlockSpec((tm, d), lambda i: (i, gate_blk)),
            pl.BlockSpec((tm, d), lambda i: (i, gate_blk + 1)),
            pl.BlockSpec((tm, d), lambda i: (i, 0)),
            pl.BlockSpec((ATTN_WIDTH, d), lambda i: (0, 0), **const),
            pl.BlockSpec((ATTN_WIDTH, d), lambda i: (0, 0), **const),
            pl.BlockSpec((d, d), lambda i: (0, 0), **const),
            pl.BlockSpec((1, d), lambda i: (0, 0), **const),
            pl.BlockSpec((ROUTER_ROWS, d), lambda i: (0, 0), **const),
            pl.BlockSpec((ROUTER_ROWS, d), lambda i: (0, 0), **const),
            pl.BlockSpec((ROUTER_ROWS, 1), lambda i: (0, 0), **const),
        ],
        out_specs=[
            pl.BlockSpec((tm, d), lambda i: (i, 0)),
            pl.BlockSpec((tm, d), lambda i: (i, 0)),
            pl.BlockSpec((8, tm), lambda i: (0, i)),
            pl.BlockSpec((tm, LANES), lambda i: (i, 0)),
        ],
        out_shape=[
            jax.ShapeDtypeStruct((n, d), jnp.float32),
            jax.ShapeDtypeStruct((n, d), jnp.bfloat16),
            jax.ShapeDtypeStruct((8, n), jnp.int32),
            jax.ShapeDtypeStruct((n, LANES), jnp.float32),
        ],
        compiler_params=pltpu.CompilerParams(
            dimension_semantics=("arbitrary",), vmem_limit_bytes=VMEM_LIMIT),
        name="post_attn",
    )(o_sb, o_fx, proj, proj, x2, wosb, wofx, wout, gn, wr_hi, wr_lo, br)


TM_SORT = 256
T_GROUP = 256
ROW_ALIGN = 16
SORT_ROWS = TM_SORT + N_GROUPS * ROW_ALIGN
SMALL_CHUNK = TM_SORT // 2
GROUP_LANE0 = EXPERTS_PER_GROUP


def _max_group_tiles(n):
    padded = n + (ROW_ALIGN - 1) * N_GROUPS * (n // TM_SORT)
    return padded // T_GROUP + N_GROUPS * (TM_SORT // T_GROUP + 2)


def _sort_plan(token_group, n):
    i32 = jnp.int32
    tiles = n // TM_SORT
    hot = token_group.reshape(tiles, TM_SORT, 1) == jnp.arange(N_GROUPS, dtype=i32)
    padded = (jnp.sum(hot, axis=1, dtype=i32) + (ROW_ALIGN - 1)) // ROW_ALIGN * ROW_ALIGN
    block = jnp.cumsum(padded, axis=1) - padded
    total = jnp.sum(padded, axis=0)
    group_tiles = (total + (TM_SORT + T_GROUP - 1)) // T_GROUP
    region = group_tiles * T_GROUP
    base = jnp.cumsum(region) - region
    dest = base[None, :] + jnp.cumsum(padded, axis=0) - padded
    ends = jnp.cumsum(group_tiles)
    steps = jnp.arange(_max_group_tiles(n), dtype=i32)
    tile_group = jnp.minimum(jnp.sum(steps[:, None] >= ends[None, :], axis=1), N_GROUPS - 1)
    return dict(block=block.reshape(-1).astype(i32), dest=dest.reshape(-1).astype(i32),
                small=(padded <= SMALL_CHUNK).reshape(-1).astype(i32),
                zero_from=(base + total).astype(i32), zero_to=(base + region).astype(i32),
                used_tiles=ends[-1:].astype(i32), tile_group=tile_group.astype(i32))


def _sort_kernel(block_ref, dest_ref, small_ref, zfrom_ref, zto_ref, used_ref,
                 h_ref, ids_ref, aux_ref, hg_ref, wg_ref, hbuf, wbuf, sems):
    i = pl.program_id(0)
    tm = TM_SORT
    rows = SORT_ROWS
    slot = i % 2

    def copies(g, slot, src, off, nrows):
        return (pltpu.make_async_copy(hbuf.at[slot, pl.ds(src, nrows)], hg_ref.at[pl.ds(off, nrows)],
                                      sems.at[0, g]),
                pltpu.make_async_copy(wbuf.at[slot, pl.ds(src, nrows)], wg_ref.at[pl.ds(off, nrows)],
                                      sems.at[1, g]))

    def tile_chunks(step, slot, act):
        for g in range(N_GROUPS):
            idx = step * N_GROUPS + g
            src = pl.multiple_of(block_ref[idx], ROW_ALIGN)
            off = pl.multiple_of(dest_ref[idx], ROW_ALIGN)
            for nrows, is_small in ((SMALL_CHUNK, True), (tm, False)):
                @pl.when((small_ref[idx] != 0) == is_small)
                def _():
                    for c in copies(g, slot, src, off, nrows):
                        act(c)

    def zero_rows(off, g):
        for c in copies(g, 0, 0, pl.multiple_of(off, ROW_ALIGN), tm):
            c.start()
        for c in copies(g, 0, 0, pl.multiple_of(off, ROW_ALIGN), tm):
            c.wait()

    @pl.when(i == 0)
    def _():
        hbuf[...] = jnp.zeros_like(hbuf)
        wbuf[...] = jnp.zeros_like(wbuf)

    gid = ids_ref[0:1, :] // EXPERTS_PER_GROUP
    sub = lax.broadcasted_iota(jnp.int32, (8, tm), 0)
    hot = sub == gid
    r_io = lax.broadcasted_iota(jnp.int32, (tm, tm), 0)
    c_io = lax.broadcasted_iota(jnp.int32, (tm, tm), 1)
    before = (r_io < c_io).astype(jnp.bfloat16)
    seen = jnp.dot(jnp.where(hot, 1.0, 0.0).astype(jnp.bfloat16), before,
                   preferred_element_type=jnp.float32)
    pos = jnp.sum(jnp.where(hot, seen, 0.0), axis=0, keepdims=True).astype(jnp.int32)
    for g in range(N_GROUPS):
        pos = pos + jnp.where(gid == g, block_ref[i * N_GROUPS + g], 0)

    s_io = lax.broadcasted_iota(jnp.int32, (rows, tm), 0)
    sel = jnp.where(pos == s_io, 1.0, 0.0).astype(jnp.bfloat16)
    hbuf[slot, 0:rows] = jnp.dot(sel, h_ref[...], preferred_element_type=jnp.float32).astype(hbuf.dtype)
    aux3 = jnp.concatenate(_split_bf16(aux_ref[...], 3), axis=1)
    w3 = jnp.dot(sel, aux3, preferred_element_type=jnp.float32)
    wbuf[slot, 0:rows] = w3[:, :LANES] + w3[:, LANES:2 * LANES] + w3[:, 2 * LANES:]

    @pl.when(i > 0)
    def _():
        tile_chunks(i - 1, 1 - slot, lambda c: c.wait())

    tile_chunks(i, slot, lambda c: c.start())

    @pl.when(i == pl.num_programs(0) - 1)
    def _():
        tile_chunks(i, slot, lambda c: c.wait())
        hbuf[...] = jnp.zeros_like(hbuf)
        wbuf[...] = jnp.zeros_like(wbuf)
        for g in range(N_GROUPS):
            zero_rows(zfrom_ref[g], g)
            zero_rows(zto_ref[g] - tm, g)

        def tail(t, _):
            zero_rows(t * T_GROUP, 0)
            return 0

        lax.fori_loop(used_ref[0], hg_ref.shape[0] // T_GROUP, tail, 0)


def _moe_sort(plan, h, ids, aux):
    n, d = h.shape
    tm = TM_SORT
    rows = _max_group_tiles(n) * T_GROUP
    grid_spec = pltpu.PrefetchScalarGridSpec(
        num_scalar_prefetch=6,
        grid=(n // tm,),
        in_specs=[
            pl.BlockSpec((tm, d), lambda i, *_: (i, 0)),
            pl.BlockSpec((8, tm), lambda i, *_: (0, i)),
            pl.BlockSpec((tm, LANES), lambda i, *_: (i, 0)),
        ],
        out_specs=[pl.BlockSpec(memory_space=pl.ANY), pl.BlockSpec(memory_space=pl.ANY)],
        scratch_shapes=[pltpu.VMEM((2, SORT_ROWS + tm, d), jnp.bfloat16),
                        pltpu.VMEM((2, SORT_ROWS + tm, LANES), jnp.float32),
                        pltpu.SemaphoreType.DMA((2, N_GROUPS))],
    )
    return pl.pallas_call(
        _sort_kernel,
        grid_spec=grid_spec,
        out_shape=[jax.ShapeDtypeStruct((rows, d), jnp.bfloat16),
                   jax.ShapeDtypeStruct((rows, LANES), jnp.float32)],
        compiler_params=pltpu.CompilerParams(
            dimension_semantics=("arbitrary",), vmem_limit_bytes=VMEM_LIMIT),
        name="moe_sort",
    )(plan["block"], plan["dest"], plan["small"], plan["zero_from"], plan["zero_to"],
      plan["used_tiles"], h, ids, aux)


def _experts_kernel(tg_ref, used_ref, h_ref, w_ref, w1_ref, w3_ref, w2_ref, y_ref):
    i = pl.program_id(0)

    @pl.when(i < used_ref[0])
    def _():
        h = h_ref[...]
        w = w_ref[...]
        hidden = []
        for e in range(EXPERTS_PER_GROUP):
            a = jnp.dot(h, w1_ref[e], preferred_element_type=jnp.float32)
            b = jnp.dot(h, w3_ref[e], preferred_element_type=jnp.float32)
            hidden.append(((a * jax.nn.sigmoid(a)) * b * w[:, e:e + 1]).astype(jnp.bfloat16))
        ne, de, d = w2_ref.shape
        y = jnp.dot(jnp.concatenate(hidden, axis=1), w2_ref[...].reshape(ne * de, d),
                    preferred_element_type=jnp.float32)
        y_ref[...] = y.astype(y_ref.dtype)

    @pl.when(i >= used_ref[0])
    def _():
        y_ref[...] = jnp.zeros_like(y_ref)


def _moe_experts(plan, hg, wg, w1, w3, w2):
    rows, d = hg.shape
    t = T_GROUP
    de = w1.shape[-1]

    def row_tile(i, tg, used):
        return (jnp.minimum(i, used[0] - 1), 0)

    def group_weights(i, tg, used):
        return (tg[i], 0, 0)

    grid_spec = pltpu.PrefetchScalarGridSpec(
        num_scalar_prefetch=2,
        grid=(rows // t,),
        in_specs=[
            pl.BlockSpec((t, d), row_tile),
            pl.BlockSpec((t, LANES), row_tile),
            pl.BlockSpec((EXPERTS_PER_GROUP, d, de), group_weights),
            pl.BlockSpec((EXPERTS_PER_GROUP, d, de), group_weights),
            pl.BlockSpec((EXPERTS_PER_GROUP, de, d), group_weights),
        ],
        out_specs=pl.BlockSpec((t, d), lambda i, tg, used: (i, 0)),
    )
    return pl.pallas_call(
        _experts_kernel,
        grid_spec=grid_spec,
        out_shape=jax.ShapeDtypeStruct((rows, d), jnp.bfloat16),
        compiler_params=pltpu.CompilerParams(
            dimension_semantics=("arbitrary",), vmem_limit_bytes=VMEM_LIMIT),
        name="moe_experts",
    )(plan["tile_group"], plan["used_tiles"], hg, wg, w1, w3, w2)


def _unsort_kernel(dest_ref, small_ref, x1_ref, aux_ref, gn_ref, yg_ref, o_ref, ybuf, sems):
    i = pl.program_id(0)
    tm = TM_SORT
    slot = i % 2

    def chunks(step, slot, act):
        for g in range(N_GROUPS):
            idx = step * N_GROUPS + g
            off = pl.multiple_of(dest_ref[idx], ROW_ALIGN)
            for nrows, is_small in ((SMALL_CHUNK, True), (tm, False)):
                @pl.when((small_ref[idx] != 0) == is_small)
                def _():
                    act(pltpu.make_async_copy(yg_ref.at[pl.ds(off, nrows)],
                                              ybuf.at[slot, g, pl.ds(0, nrows)], sems.at[slot, g]))

    @pl.when(i == 0)
    def _():
        ybuf[...] = jnp.zeros_like(ybuf)
        chunks(0, 0, lambda c: c.start())

    @pl.when(i + 1 < pl.num_programs(0))
    def _():
        chunks(i + 1, 1 - slot, lambda c: c.start())

    chunks(i, slot, lambda c: c.wait())

    aux = aux_ref[...]
    lane = lax.broadcasted_iota(jnp.int32, (tm, LANES), 1)
    is_group_lane = jnp.logical_and(lane >= GROUP_LANE0, lane < GROUP_LANE0 + N_GROUPS)
    hot = jnp.where(is_group_lane, aux, 0.0)
    r_io = lax.broadcasted_iota(jnp.int32, (tm, tm), 0)
    c_io = lax.broadcasted_iota(jnp.int32, (tm, tm), 1)
    before = (c_io < r_io).astype(jnp.bfloat16)
    seen = jnp.dot(before, hot.astype(jnp.bfloat16), preferred_element_type=jnp.float32)
    rank = jnp.sum(hot * seen, axis=-1, keepdims=True).astype(jnp.int32)
    y = x1_ref[...]
    for g in range(N_GROUPS):
        mine = aux[:, GROUP_LANE0 + g:GROUP_LANE0 + g + 1] > 0.5
        sel = jnp.where(jnp.logical_and(rank == c_io, mine), 1.0, 0.0).astype(jnp.bfloat16)
        y = y + jnp.dot(sel, ybuf[slot, g], preferred_element_type=jnp.float32)
    o_ref[...] = _rmsnorm_f32(y, gn_ref[...])


def _moe_unsort(plan, x1, aux, gn, yg):
    n, d = x1.shape
    tm = TM_SORT
    grid_spec = pltpu.PrefetchScalarGridSpec(
        num_scalar_prefetch=2,
        grid=(n // tm,),
        in_specs=[
            pl.BlockSpec((tm, d), lambda i, *_: (i, 0)),
            pl.BlockSpec((tm, LANES), lambda i, *_: (i, 0)),
            pl.BlockSpec((1, d), lambda i, *_: (0, 0)),
            pl.BlockSpec(memory_space=pl.ANY),
        ],
        out_specs=pl.BlockSpec((tm, d), lambda i, *_: (i, 0)),
        scratch_shapes=[pltpu.VMEM((2, N_GROUPS, tm, d), jnp.bfloat16),
                        pltpu.SemaphoreType.DMA((2, N_GROUPS))],
    )
    return pl.pallas_call(
        _unsort_kernel,
        grid_spec=grid_spec,
        out_shape=jax.ShapeDtypeStruct((n, d), jnp.float32),
        compiler_params=pltpu.CompilerParams(
            dimension_semantics=("arbitrary",), vmem_limit_bytes=VMEM_LIMIT),
        name="moe_unsort",
    )(plan["dest"], plan["small"], x1, aux, gn, yg)


def kernel(x, norm_attn, w_in, b_forget, w_o_sb, w_o_fox, w_out, norm_ffn, w_router_group,
           b_router_group, w_router_expert, b_router_expert, w1, w3, w2, norm_final):
    b, s, d = x.shape
    n = b * s
    depth = w_in.shape[0]
    assert depth == 1, "the final norm is fused into the MoE un-sort of a single layer"
    assert s % TM_PROJ == 0 and s % TQ == 0 and n % TM_SORT == 0 and T_GROUP == TM_SORT
    bf16 = jnp.bfloat16
    n_main = 6 * ATTN_WIDTH
    x2 = x.reshape(n, d)
    for l in range(depth):
        w_l = w_in[l]
        w_main = jnp.concatenate([w_l[:, :n_main], w_l[:, n_main + N_HEADS:]], axis=1).astype(bf16)
        wf = jnp.pad(w_l[:, n_main:n_main + N_HEADS], ((0, 0), (0, LANES - N_HEADS)))
        wf_hi = wf.astype(bf16)
        wf_parts = jnp.concatenate([wf_hi, (wf - wf_hi.astype(jnp.float32)).astype(bf16)], axis=1)
        bf = jnp.pad(b_forget[l], (0, LANES - N_HEADS)).reshape(1, LANES)
        wr = jnp.concatenate([w_router_group[l].T, jnp.zeros((8 - N_GROUPS, d), jnp.float32),
                              w_router_expert[l].T], axis=0)
        wr_hi = wr.astype(bf16)
        wr_lo = (wr - wr_hi.astype(jnp.float32)).astype(bf16)
        br = jnp.concatenate([b_router_group[l], jnp.zeros((8 - N_GROUPS,), jnp.float32),
                              b_router_expert[l]]).reshape(ROUTER_ROWS, 1)

        proj, c, ct = _inproj(x2, norm_attn[l].reshape(1, d), w_main, wf_parts, bf, s)
        proj3 = proj.reshape(b, s, proj.shape[1])
        o_sb, o_fx = _attention(proj3, c.reshape(b, s, LANES), ct)
        x1, h, ids, aux = _post_attention(
            o_sb.reshape(n, ATTN_WIDTH), o_fx.reshape(n, ATTN_WIDTH), proj, x2,
            w_o_sb[l].astype(bf16), w_o_fox[l].astype(bf16), w_out[l].astype(bf16),
            norm_ffn[l].reshape(1, d), wr_hi, wr_lo, br)
        plan = _sort_plan(ids[0] // EXPERTS_PER_GROUP, n)
        hg, wg = _moe_sort(plan, h, ids, aux)
        yg = _moe_experts(plan, hg, wg, w1[l].astype(bf16), w3[l].astype(bf16), w2[l].astype(bf16))
        x2 = _moe_unsort(plan, x1, aux, norm_final.reshape(1, d), yg)
    return x2.reshape(b, s, d)
```

```python
import functools
import math
from typing import Callable, NamedTuple

import jax
import jax.numpy as jnp
from jax import lax
from jax.experimental import pallas as pl
from jax.experimental.pallas import tpu as pltpu

HEAD_DIM = 64
N_HEADS = 8
ATTN_WIDTH = N_HEADS * HEAD_DIM
N_GROUPS = 4
EXPERTS_PER_GROUP = 8
N_EXPERTS = N_GROUPS * EXPERTS_PER_GROUP
RMS_EPS = 1e-6
LANES = 128
HEADS_PER_BLOCK = LANES // HEAD_DIM
N_HEAD_BLOCKS = N_HEADS // HEADS_PER_BLOCK
ROUTER_ROWS = 8 + N_EXPERTS
VMEM_LIMIT = 56 * 1024 * 1024

TM_PROJ = 512
TQ = 256
TK = 256
TM_POST = 512

_NT = (((1,), (1,)), ((), ()))
LOG2E = math.log2(math.e)
Q_SCALE = LOG2E / math.sqrt(HEAD_DIM)
SKIP_LOG2 = 160.0


def _split_bf16(v, parts):
    out = []
    r = v
    for i in range(parts):
        p = r.astype(jnp.bfloat16)
        out.append(p)
        if i + 1 < parts:
            r = r - p.astype(jnp.float32)
    return out


def _rmsnorm_f32(x, g):
    ms = jnp.mean(x * x, axis=-1, keepdims=True)
    return x * lax.rsqrt(ms + RMS_EPS) * g


def _inproj_kernel(x_ref, g_ref, w_ref, wf_ref, bf_ref,
                   proj_ref, c_ref, ct_ref, carry_ref, *, tiles_per_seq, n_chunks, chunk, q_chunks):
    i = pl.program_id(0)

    @pl.when(i % tiles_per_seq == 0)
    def _():
        carry_ref[...] = jnp.zeros_like(carry_ref)

    y = _rmsnorm_f32(x_ref[...], g_ref[...])
    h_hi, h_lo = _split_bf16(y, 2)
    for c in range(n_chunks):
        sl = slice(c * chunk, (c + 1) * chunk)
        p = jnp.dot(h_hi, w_ref[:, sl], preferred_element_type=jnp.float32)
        if c in q_chunks:
            p = p * Q_SCALE
        proj_ref[:, sl] = p.astype(proj_ref.dtype)

    ff = jnp.dot(h_hi, wf_ref[...], preferred_element_type=jnp.float32)
    f = (ff[:, :LANES] + ff[:, LANES:]
         + jnp.dot(h_lo, wf_ref[:, :LANES], preferred_element_type=jnp.float32))
    f = f + bf_ref[...]
    logf = (jnp.minimum(f, 0.0) - jnp.log1p(jnp.exp(-jnp.abs(f)))) * LOG2E

    tm = logf.shape[0]
    row = lax.broadcasted_iota(jnp.int32, (tm, tm), 0)
    col = lax.broadcasted_iota(jnp.int32, (tm, tm), 1)
    tri = (col <= row).astype(jnp.bfloat16)
    parts = jnp.dot(tri, jnp.concatenate(_split_bf16(logf, 3), axis=1),
                    preferred_element_type=jnp.float32)
    cum = carry_ref[...] + parts[:, :LANES] + parts[:, LANES:2 * LANES] + parts[:, 2 * LANES:]
    c_ref[...] = cum
    ct_ref[...] = cum.T[:N_HEADS, :]
    carry_ref[...] = cum[tm - 1:tm, :]


def _inproj(x2, g, w_main, wf_parts, bf, seq):
    n, d = x2.shape
    cols = w_main.shape[1]
    tm = TM_PROJ
    chunk = ATTN_WIDTH
    kern = functools.partial(_inproj_kernel, tiles_per_seq=seq // tm,
                             n_chunks=cols // chunk, chunk=chunk, q_chunks=(0, 3))
    const = dict(pipeline_mode=pl.Buffered(1))
    return pl.pallas_call(
        kern,
        grid=(n // tm,),
        in_specs=[
            pl.BlockSpec((tm, d), lambda i: (i, 0)),
            pl.BlockSpec((1, d), lambda i: (0, 0), **const),
            pl.BlockSpec((d, cols), lambda i: (0, 0), **const),
            pl.BlockSpec((d, 2 * LANES), lambda i: (0, 0), **const),
            pl.BlockSpec((1, LANES), lambda i: (0, 0), **const),
        ],
        out_specs=[
            pl.BlockSpec((tm, cols), lambda i: (i, 0)),
            pl.BlockSpec((tm, LANES), lambda i: (i, 0)),
            pl.BlockSpec((N_HEADS, tm), lambda i: (0, i)),
        ],
        out_shape=[
            jax.ShapeDtypeStruct((n, cols), jnp.bfloat16),
            jax.ShapeDtypeStruct((n, LANES), jnp.float32),
            jax.ShapeDtypeStruct((N_HEADS, n), jnp.float32),
        ],
        scratch_shapes=[pltpu.VMEM((1, LANES), jnp.float32)],
        compiler_params=pltpu.CompilerParams(
            dimension_semantics=("arbitrary",), vmem_limit_bytes=VMEM_LIMIT),
        name="inproj",
    )(x2, g, w_main, wf_parts, bf)


def _stack_heads(q):
    lane = lax.broadcasted_iota(jnp.int32, q.shape, 1)
    return jnp.concatenate(
        [jnp.where((lane // HEAD_DIM) == h, q, jnp.zeros_like(q)) for h in range(HEADS_PER_BLOCK)], axis=0)


def _unstack_heads(acc):
    tq = acc.shape[0] // HEADS_PER_BLOCK
    lane = lax.broadcasted_iota(jnp.int32, (tq, LANES), 1)
    out = acc[:tq]
    for h in range(1, HEADS_PER_BLOCK):
        out = jnp.where((lane // HEAD_DIM) == h, acc[h * tq:(h + 1) * tq], out)
    return out


def _lane_tile(x, width):
    return jnp.concatenate([x] * (width // LANES), axis=1)


class _Chain(NamedTuple):
    init: Callable
    score: Callable
    finish: Callable
    live: Callable


def _pipelined_tiles(k0, chains, diag):
    for ch in chains:
        ch.score(k0, 0, diag)

    def overlap(ch, kj, slot):
        ch.score(kj - 1, 1 - slot, False)
        ch.finish(kj, slot)

    def cond(state):
        p, gos = state[0], state[1:]
        return jnp.logical_and(p < k0 // 2, functools.reduce(jnp.logical_or, [g != 0 for g in gos]))

    def pair(state):
        p, gos = state[0], state[1:]
        kj = k0 - 2 * p
        for mask in range(1, 2 ** len(chains)):
            active = [c for i, c in enumerate(chains) if mask >> i & 1]
            preds = [(gos[i] != 0) if mask >> i & 1 else (gos[i] == 0) for i in range(len(chains))]

            @pl.when(functools.reduce(jnp.logical_and, preds))
            def _():
                for slot in (0, 1):
                    for ch in active:
                        overlap(ch, kj - slot, slot)

        new = [jnp.where(jnp.logical_and(g != 0, ch.live(kj - 1)), 1, 0).astype(jnp.int32)
               for g, ch in zip(gos, chains)]
        return (p + 1, *new)

    state = lax.while_loop(cond, pair, (jnp.int32(0),) + (jnp.int32(1),) * len(chains))

    for go, ch in zip(state[1:], chains):
        @pl.when(jnp.logical_and(go != 0, k0 % 2 == 1))
        def _():
            overlap(ch, 1, 0)
            ch.finish(0, 1)

        @pl.when(jnp.logical_and(go != 0, k0 % 2 == 0))
        def _():
            ch.finish(0, 0)


def _sb_chain(q_ref, k_ref, v_ref, acc_ref, carry_ref, lb_ref, lk_ref, rs_ref):
    qcat = _stack_heads(q_ref[...])
    m = HEADS_PER_BLOCK * TQ
    urow = lax.broadcasted_iota(jnp.int32, (TK, TK), 0)
    ucol = lax.broadcasted_iota(jnp.int32, (TK, TK), 1)
    upper = (ucol < urow).astype(jnp.bfloat16)

    def init():
        acc_ref[...] = jnp.zeros_like(acc_ref)
        carry_ref[...] = jnp.zeros_like(carry_ref)

    def score(kj, slot, diag):
        k = k_ref[pl.ds(kj * TK, TK), :]
        z = lax.dot_general(qcat, k, _NT, preferred_element_type=jnp.float32)
        t = jnp.log2(1.0 + jnp.exp2(-jnp.abs(z)))
        log_beta = jnp.minimum(z, 0.0) - t
        log_keep = log_beta - z
        if diag:
            row = lax.broadcasted_iota(jnp.int32, (m, TK), 0)
            col = lax.broadcasted_iota(jnp.int32, (m, TK), 1)
            below = col < (row % TQ)
            log_keep = jnp.where(below, log_keep, 0.0)
            log_beta = jnp.where(below, log_beta, -jnp.inf)
        lb_ref[slot] = log_beta
        lk_ref[slot] = log_keep.astype(jnp.bfloat16)
        rs_ref[slot] = jnp.broadcast_to(jnp.sum(log_keep, axis=-1, keepdims=True), (m, LANES))

    def finish(kj, slot, first=False):
        v = v_ref[pl.ds(kj * TK, TK), :]
        rest = jnp.dot(lk_ref[slot], upper, preferred_element_type=jnp.float32)
        if first:
            a = jnp.exp2(lb_ref[slot] + rest)
            acc_ref[...] = jnp.dot(a.astype(v.dtype), v, preferred_element_type=jnp.float32)
            carry_ref[...] = rs_ref[slot]
            return
        carry = carry_ref[...]
        a = jnp.exp2(lb_ref[slot] + rest + _lane_tile(carry, TK))
        acc_ref[...] += jnp.dot(a.astype(v.dtype), v, preferred_element_type=jnp.float32)
        carry_ref[...] = carry + rs_ref[slot]

    def live(kj):
        return jnp.max(carry_ref[...]) > -SKIP_LOG2

    return _Chain(init, score, finish, live)


def _fox_chain(hb, qi, q_ref, k_ref, v_ref, c_ref, ct_ref, acc_ref, m_ref, l_ref, z_ref, rm_ref,
               kn_ref):
    qcat = _stack_heads(q_ref[...])
    lane = lax.broadcasted_iota(jnp.int32, (TQ, LANES), 1)
    cblk = c_ref[...]
    cq = jnp.concatenate(
        [jnp.broadcast_to(
            jnp.sum(jnp.where(lane == hb * HEADS_PER_BLOCK + h, cblk, 0.0), axis=-1, keepdims=True),
            (TQ, TK)) for h in range(HEADS_PER_BLOCK)], axis=0)
    m = HEADS_PER_BLOCK * TQ
    ones = jnp.ones((TK, LANES), jnp.bfloat16)

    @pl.when(qi == 0)
    def _():
        kf = k_ref[...].astype(jnp.float32)
        ksq = kf * kf
        klane = lax.broadcasted_iota(jnp.int32, ksq.shape, 1)
        for h in range(HEADS_PER_BLOCK):
            n2 = jnp.sum(jnp.where((klane // HEAD_DIM) == h, ksq, 0.0), axis=-1, keepdims=True)
            kn_ref[h * TQ:(h + 1) * TQ, :] = jnp.broadcast_to(
                jnp.sqrt(jnp.max(n2, axis=0, keepdims=True)), (TQ, LANES))

    qf = qcat.astype(jnp.float32)
    qn = jnp.broadcast_to(jnp.sqrt(jnp.sum(qf * qf, axis=-1, keepdims=True)), (m, LANES))

    def init():
        acc_ref[...] = jnp.zeros_like(acc_ref)
        l_ref[...] = jnp.zeros_like(l_ref)
        m_ref[...] = jnp.full_like(m_ref, -jnp.inf)

    def score(kj, slot, diag):
        k = k_ref[pl.ds(kj * TK, TK), :]
        ck = jnp.concatenate(
            [jnp.broadcast_to(ct_ref[pl.ds(hb * HEADS_PER_BLOCK + h, 1), pl.ds(kj * TK, TK)], (TQ, TK))
             for h in range(HEADS_PER_BLOCK)], axis=0)
        z = lax.dot_general(qcat, k, _NT, preferred_element_type=jnp.float32)
        z = (z + cq) - ck
        if diag:
            row = lax.broadcasted_iota(jnp.int32, (m, TK), 0)
            col = lax.broadcasted_iota(jnp.int32, (m, TK), 1)
            z = jnp.where(col <= (row % TQ), z, -jnp.inf)
        z_ref[slot] = z
        rm_ref[slot] = jnp.broadcast_to(jnp.max(z, axis=-1, keepdims=True), (m, LANES))

    def finish(kj, slot, first=False):
        v = v_ref[pl.ds(kj * TK, TK), :]
        if first:
            m_new = rm_ref[slot]
            p = jnp.exp2(z_ref[slot] - _lane_tile(m_new, TK))
            pv = jnp.dot(p.astype(v.dtype), jnp.concatenate([v, ones], axis=1),
                         preferred_element_type=jnp.float32)
            acc_ref[...] = pv[:, :LANES]
            l_ref[...] = pv[:, LANES:]
            m_ref[...] = m_new
            return
        m_old = m_ref[...]
        m_new = jnp.maximum(m_old, rm_ref[slot])
        alpha = jnp.exp2(m_old - m_new)
        p = jnp.exp2(z_ref[slot] - _lane_tile(m_new, TK))
        pv = jnp.dot(p.astype(v.dtype), jnp.concatenate([v, ones], axis=1),
                     preferred_element_type=jnp.float32)
        acc_ref[...] = alpha * acc_ref[...] + pv[:, :LANES]
        l_ref[...] = alpha * l_ref[...] + pv[:, LANES:]
        m_ref[...] = m_new

    def live(kj):
        cb = jnp.concatenate(
            [jnp.broadcast_to(
                jnp.max(ct_ref[pl.ds(hb * HEADS_PER_BLOCK + h, 1), pl.ds(kj * TK, TK)],
                        axis=-1, keepdims=True), (TQ, LANES)) for h in range(HEADS_PER_BLOCK)], axis=0)
        bound = qn * kn_ref[...] + cq[:, :LANES] - cb - m_ref[...]
        return jnp.max(bound) > -SKIP_LOG2

    return _Chain(init, score, finish, live)


SB_HEAD_TILES = 2
FOX_HEAD_TILES = 4


def _attn_kernel(qs_ref, ks_ref, vs_ref, qf_ref, kf_ref, vf_ref, c_ref, ct_ref, osb_ref, ofx_ref,
                 sb_acc, sb_carry, sb_lb, sb_lk, sb_rs, fx_acc, fx_m, fx_l, fx_z, fx_rm, fx_kn):
    hb = pl.program_id(1)
    qi = pl.program_id(2)
    sb = _sb_chain(qs_ref, ks_ref, vs_ref, sb_acc, sb_carry, sb_lb, sb_lk, sb_rs)
    fx = _fox_chain(hb, qi, qf_ref, kf_ref, vf_ref, c_ref, ct_ref, fx_acc, fx_m, fx_l, fx_z, fx_rm, fx_kn)
    head = max(SB_HEAD_TILES, FOX_HEAD_TILES)

    @pl.when(qi < head)
    def _():
        sb.init()
        fx.init()
        _pipelined_tiles(qi, [sb, fx], True)

    @pl.when(qi >= head)
    def _():
        for i in range(head + 1):
            for ch, count in ((fx, FOX_HEAD_TILES), (sb, SB_HEAD_TILES)):
                if i < count:
                    ch.score(qi - i, i, i == 0)
                if 1 <= i <= count:
                    ch.finish(qi - i + 1, i - 1, i == 1)
        heads = ((sb, SB_HEAD_TILES), (fx, FOX_HEAD_TILES))
        alive = [ch.live(qi - count + 1) for ch, count in heads]
        for (ch, count), go in zip(heads, alive):
            @pl.when(go)
            def _():
                k1 = qi - count
                ch.score(k1, 0, False)
                ch.finish(k1, 0)

                @pl.when(jnp.logical_and(k1 >= 1, ch.live(k1)))
                def _():
                    _pipelined_tiles(k1 - 1, [ch], False)

    osb_ref[...] = _unstack_heads(sb_acc[...]).astype(osb_ref.dtype)
    ofx_ref[...] = _unstack_heads(fx_acc[...] / fx_l[...]).astype(ofx_ref.dtype)


def _attention(proj3, c3, ct):
    b, s, _ = proj3.shape
    m = HEADS_PER_BLOCK * TQ

    def q_spec(blk):
        return pl.BlockSpec((None, TQ, LANES), lambda bi, hb, qi: (bi, qi, blk * N_HEAD_BLOCKS + hb))

    def kv_spec(blk):
        return pl.BlockSpec((None, s, LANES), lambda bi, hb, qi: (bi, 0, blk * N_HEAD_BLOCKS + hb))

    out_spec = pl.BlockSpec((None, TQ, LANES), lambda bi, hb, qi: (bi, qi, hb))
    out_shape = jax.ShapeDtypeStruct((b, s, ATTN_WIDTH), jnp.bfloat16)
    f32 = jnp.float32
    return pl.pallas_call(
        _attn_kernel,
        grid=(b, N_HEAD_BLOCKS, s // TQ),
        in_specs=[q_spec(0), kv_spec(1), kv_spec(2), q_spec(3), kv_spec(4), kv_spec(5),
                  pl.BlockSpec((None, TQ, LANES), lambda bi, hb, qi: (bi, qi, 0)),
                  pl.BlockSpec((N_HEADS, s), lambda bi, hb, qi: (0, bi))],
        out_specs=[out_spec, out_spec],
        out_shape=[out_shape, out_shape],
        scratch_shapes=[pltpu.VMEM((m, LANES), f32),
                        pltpu.VMEM((m, LANES), f32),
                        pltpu.VMEM((2, m, TK), f32),
                        pltpu.VMEM((2, m, TK), jnp.bfloat16),
                        pltpu.VMEM((2, m, LANES), f32),
                        pltpu.VMEM((m, LANES), f32),
                        pltpu.VMEM((m, LANES), f32),
                        pltpu.VMEM((m, LANES), f32),
                        pltpu.VMEM((FOX_HEAD_TILES, m, TK), f32),
                        pltpu.VMEM((FOX_HEAD_TILES, m, LANES), f32),
                        pltpu.VMEM((m, LANES), f32)],
        compiler_params=pltpu.CompilerParams(
            dimension_semantics=("arbitrary", "arbitrary", "arbitrary"),
            vmem_limit_bytes=VMEM_LIMIT),
        name="attn",
    )(proj3, proj3, proj3, proj3, proj3, proj3, c3, ct)


def _post_kernel(osb_ref, ofx_ref, gsb_ref, gfx_ref, x_ref, wosb_ref, wofx_ref, wout_ref,
                 gn_ref, wrh_ref, wrl_ref, br_ref,
                 x1_ref, h_ref, ids_ref, aux_ref):
    y_sb = jnp.dot(osb_ref[...], wosb_ref[...], preferred_element_type=jnp.float32)
    y_fx = jnp.dot(ofx_ref[...], wofx_ref[...], preferred_element_type=jnp.float32)
    mixed = (jax.nn.sigmoid(gsb_ref[...].astype(jnp.float32)) * y_sb
             + jax.nn.sigmoid(gfx_ref[...].astype(jnp.float32)) * y_fx)
    x1 = x_ref[...] + jnp.dot(mixed.astype(jnp.bfloat16), wout_ref[...],
                              preferred_element_type=jnp.float32)
    x1_ref[...] = x1
    y = _rmsnorm_f32(x1, gn_ref[...])
    h_hi, h_lo = _split_bf16(y, 2)
    h_ref[...] = h_hi

    lg = (lax.dot_general(wrh_ref[...], h_hi, _NT, preferred_element_type=jnp.float32)
          + lax.dot_general(wrh_ref[...], h_lo, _NT, preferred_element_type=jnp.float32)
          + lax.dot_general(wrl_ref[...], h_hi, _NT, preferred_element_type=jnp.float32))
    lg = lg + br_ref[...]
    tm = lg.shape[1]
    sub = lax.broadcasted_iota(jnp.int32, (8, tm), 0)
    neg = -jnp.inf

    gl = jnp.where(sub < N_GROUPS, lg[0:8], neg)
    gm = jnp.max(gl, axis=0, keepdims=True)
    g_w = 1.0 / jnp.sum(jnp.exp(gl - gm), axis=0, keepdims=True)
    g_idx = jnp.min(jnp.where(gl == gm, sub, 8), axis=0, keepdims=True)

    e_sel = lg[8:16]
    for g in range(1, N_GROUPS):
        e_sel = jnp.where(g_idx == g, lg[8 + 8 * g:16 + 8 * g], e_sel)
    m1 = jnp.max(e_sel, axis=0, keepdims=True)
    i1 = jnp.min(jnp.where(e_sel == m1, sub, 8), axis=0, keepdims=True)
    e_rest = jnp.where(sub == i1, neg, e_sel)
    m2 = jnp.max(e_rest, axis=0, keepdims=True)
    i2 = jnp.min(jnp.where(e_rest == m2, sub, 8), axis=0, keepdims=True)
    p2 = jnp.exp(m2 - m1)
    w1 = g_w / (1.0 + p2)
    w2 = g_w * p2 / (1.0 + p2)
    base = g_idx * EXPERTS_PER_GROUP
    ids_ref[...] = jnp.where(sub == 0, base + i1, jnp.where(sub == 1, base + i2, 0))
    dense_w = jnp.where(sub == i1, w1, jnp.where(sub == i2, w2, 0.0))
    group_hot = jnp.where(sub == g_idx, 1.0, 0.0)
    record = jnp.concatenate(
        [dense_w, group_hot, jnp.zeros((LANES - 16, tm), jnp.float32)], axis=0)
    aux_ref[...] = record.T


def _post_attention(o_sb, o_fx, proj, x2, wosb, wofx, wout, gn, wr_hi, wr_lo, br):
    n, d = x2.shape
    tm = TM_POST
    gate_blk = (3 * ATTN_WIDTH * 2) // d
    const = dict(pipeline_mode=pl.Buffered(1))
    return pl.pallas_call(
        _post_kernel,
        grid=(n // tm,),
        in_specs=[
            pl.BlockSpec((tm, ATTN_WIDTH), lambda i: (i, 0)),
            pl.BlockSpec((tm, ATTN_WIDTH), lambda i: (i, 0)),
            pl.BlockSpec((tm, d), lambda i: (i, gate_blk)),
            pl.BlockSpec((tm, d), lambda i: (i, gate_blk + 1)),
            pl.BlockSpec((tm, d), lambda i: (i, 0)),
            pl.BlockSpec((ATTN_WIDTH, d), lambda i: (0, 0), **const),
            pl.BlockSpec((ATTN_WIDTH, d), lambda i: (0, 0), **const),
            pl.BlockSpec((d, d), lambda i: (0, 0), **const),
            pl.BlockSpec((1, d), lambda i: (0, 0), **const),
            pl.BlockSpec((ROUTER_ROWS, d), lambda i: (0, 0), **const),
            pl.BlockSpec((ROUTER_ROWS, d), lambda i: (0, 0), **const),
            pl.BlockSpec((ROUTER_ROWS, 1), lambda i: (0, 0), **const),
        ],
        out_specs=[
            pl.BlockSpec((tm, d), lambda i: (i, 0)),
            pl.BlockSpec((tm, d), lambda i: (i, 0)),
            pl.BlockSpec((8, tm), lambda i: (0, i)),
            pl.BlockSpec((tm, LANES), lambda i: (i, 0)),
        ],
        out_shape=[
            jax.ShapeDtypeStruct((n, d), jnp.float32),
            jax.ShapeDtypeStruct((n, d), jnp.bfloat16),
            jax.ShapeDtypeStruct((8, n), jnp.int32),
            jax.ShapeDtypeStruct((n, LANES), jnp.float32),
        ],
        compiler_params=pltpu.CompilerParams(
            dimension_semantics=("arbitrary",), vmem_limit_bytes=VMEM_LIMIT),
        name="post_attn",
    )(o_sb, o_fx, proj, proj, x2, wosb, wofx, wout, gn, wr_hi, wr_lo, br)


TM_SORT = 256
T_GROUP = 256
ROW_ALIGN = 16
SORT_ROWS = TM_SORT + N_GROUPS * ROW_ALIGN
SMALL_CHUNK = TM_SORT // 2
GROUP_LANE0 = EXPERTS_PER_GROUP


def _max_group_tiles(n):
    padded = n + (ROW_ALIGN - 1) * N_GROUPS * (n // TM_SORT)
    return padded // T_GROUP + N_GROUPS * (TM_SORT // T_GROUP + 2)


def _sort_plan(token_group, n):
    i32 = jnp.int32
    tiles = n // TM_SORT
    hot = token_group.reshape(tiles, TM_SORT, 1) == jnp.arange(N_GROUPS, dtype=i32)
    padded = (jnp.sum(hot, axis=1, dtype=i32) + (ROW_ALIGN - 1)) // ROW_ALIGN * ROW_ALIGN
    block = jnp.cumsum(padded, axis=1) - padded
    total = jnp.sum(padded, axis=0)
    group_tiles = (total + (TM_SORT + T_GROUP - 1)) // T_GROUP
    region = group_tiles * T_GROUP
    base = jnp.cumsum(region) - region
    dest = base[None, :] + jnp.cumsum(padded, axis=0) - padded
    ends = jnp.cumsum(group_tiles)
    steps = jnp.arange(_max_group_tiles(n), dtype=i32)
    tile_group = jnp.minimum(jnp.sum(steps[:, None] >= ends[None, :], axis=1), N_GROUPS - 1)
    return dict(block=block.reshape(-1).astype(i32), dest=dest.reshape(-1).astype(i32),
                small=(padded <= SMALL_CHUNK).reshape(-1).astype(i32),
                zero_from=(base + total).astype(i32), zero_to=(base + region).astype(i32),
                used_tiles=ends[-1:].astype(i32), tile_group=tile_group.astype(i32))


def _sort_kernel(block_ref, dest_ref, small_ref, zfrom_ref, zto_ref, used_ref,
                 h_ref, ids_ref, aux_ref, hg_ref, wg_ref, hbuf, wbuf, sems):
    i = pl.program_id(0)
    tm = TM_SORT
    rows = SORT_ROWS
    slot = i % 2

    def copies(g, slot, src, off, nrows):
        return (pltpu.make_async_copy(hbuf.at[slot, pl.ds(src, nrows)], hg_ref.at[pl.ds(off, nrows)],
                                      sems.at[0, g]),
                pltpu.make_async_copy(wbuf.at[slot, pl.ds(src, nrows)], wg_ref.at[pl.ds(off, nrows)],
                                      sems.at[1, g]))

    def tile_chunks(step, slot, act):
        for g in range(N_GROUPS):
            idx = step * N_GROUPS + g
            src = pl.multiple_of(block_ref[idx], ROW_ALIGN)
            off = pl.multiple_of(dest_ref[idx], ROW_ALIGN)
            for nrows, is_small in ((SMALL_CHUNK, True), (tm, False)):
                @pl.when((small_ref[idx] != 0) == is_small)
                def _():
                    for c in copies(g, slot, src, off, nrows):
                        act(c)

    def zero_rows(off, g):
        for c in copies(g, 0, 0, pl.multiple_of(off, ROW_ALIGN), tm):
            c.start()
        for c in copies(g, 0, 0, pl.multiple_of(off, ROW_ALIGN), tm):
            c.wait()

    @pl.when(i == 0)
    def _():
        hbuf[...] = jnp.zeros_like(hbuf)
        wbuf[...] = jnp.zeros_like(wbuf)

    gid = ids_ref[0:1, :] // EXPERTS_PER_GROUP
    sub = lax.broadcasted_iota(jnp.int32, (8, tm), 0)
    hot = sub == gid
    r_io = lax.broadcasted_iota(jnp.int32, (tm, tm), 0)
    c_io = lax.broadcasted_iota(jnp.int32, (tm, tm), 1)
    before = (r_io < c_io).astype(jnp.bfloat16)
    seen = jnp.dot(jnp.where(hot, 1.0, 0.0).astype(jnp.bfloat16), before,
                   preferred_element_type=jnp.float32)
    pos = jnp.sum(jnp.where(hot, seen, 0.0), axis=0, keepdims=True).astype(jnp.int32)
    for g in range(N_GROUPS):
        pos = pos + jnp.where(gid == g, block_ref[i * N_GROUPS + g], 0)

    s_io = lax.broadcasted_iota(jnp.int32, (rows, tm), 0)
    sel = jnp.where(pos == s_io, 1.0, 0.0).astype(jnp.bfloat16)
    hbuf[slot, 0:rows] = jnp.dot(sel, h_ref[...], preferred_element_type=jnp.float32).astype(hbuf.dtype)
    aux3 = jnp.concatenate(_split_bf16(aux_ref[...], 3), axis=1)
    w3 = jnp.dot(sel, aux3, preferred_element_type=jnp.float32)
    wbuf[slot, 0:rows] = w3[:, :LANES] + w3[:, LANES:2 * LANES] + w3[:, 2 * LANES:]

    @pl.when(i > 0)
    def _():
        tile_chunks(i - 1, 1 - slot, lambda c: c.wait())

    tile_chunks(i, slot, lambda c: c.start())

    @pl.when(i == pl.num_programs(0) - 1)
    def _():
        tile_chunks(i, slot, lambda c: c.wait())
        hbuf[...] = jnp.zeros_like(hbuf)
        wbuf[...] = jnp.zeros_like(wbuf)
        for g in range(N_GROUPS):
            zero_rows(zfrom_ref[g], g)
            zero_rows(zto_ref[g] - tm, g)

        def tail(t, _):
            zero_rows(t * T_GROUP, 0)
            return 0

        lax.fori_loop(used_ref[0], hg_ref.shape[0] // T_GROUP, tail, 0)


def _moe_sort(plan, h, ids, aux):
    n, d = h.shape
    tm = TM_SORT
    rows = _max_group_tiles(n) * T_GROUP
    grid_spec = pltpu.PrefetchScalarGridSpec(
        num_scalar_prefetch=6,
        grid=(n // tm,),
        in_specs=[
            pl.BlockSpec((tm, d), lambda i, *_: (i, 0)),
            pl.BlockSpec((8, tm), lambda i, *_: (0, i)),
            pl.BlockSpec((tm, LANES), lambda i, *_: (i, 0)),
        ],
        out_specs=[pl.BlockSpec(memory_space=pl.ANY), pl.BlockSpec(memory_space=pl.ANY)],
        scratch_shapes=[pltpu.VMEM((2, SORT_ROWS + tm, d), jnp.bfloat16),
                        pltpu.VMEM((2, SORT_ROWS + tm, LANES), jnp.float32),
                        pltpu.SemaphoreType.DMA((2, N_GROUPS))],
    )
    return pl.pallas_call(
        _sort_kernel,
        grid_spec=grid_spec,
        out_shape=[jax.ShapeDtypeStruct((rows, d), jnp.bfloat16),
                   jax.ShapeDtypeStruct((rows, LANES), jnp.float32)],
        compiler_params=pltpu.CompilerParams(
            dimension_semantics=("arbitrary",), vmem_limit_bytes=VMEM_LIMIT),
        name="moe_sort",
    )(plan["block"], plan["dest"], plan["small"], plan["zero_from"], plan["zero_to"],
      plan["used_tiles"], h, ids, aux)


def _experts_kernel(tg_ref, used_ref, h_ref, w_ref, w1_ref, w3_ref, w2_ref, y_ref):
    i = pl.program_id(0)

    @pl.when(i < used_ref[0])
    def _():
        h = h_ref[...]
        w = w_ref[...]
        hidden = []
        for e in range(EXPERTS_PER_GROUP):
            a = jnp.dot(h, w1_ref[e], preferred_element_type=jnp.float32)
            b = jnp.dot(h, w3_ref[e], preferred_element_type=jnp.float32)
            hidden.append(((a * jax.nn.sigmoid(a)) * b * w[:, e:e + 1]).astype(jnp.bfloat16))
        ne, de, d = w2_ref.shape
        y = jnp.dot(jnp.concatenate(hidden, axis=1), w2_ref[...].reshape(ne * de, d),
                    preferred_element_type=jnp.float32)
        y_ref[...] = y.astype(y_ref.dtype)

    @pl.when(i >= used_ref[0])
    def _():
        y_ref[...] = jnp.zeros_like(y_ref)


def _moe_experts(plan, hg, wg, w1, w3, w2):
    rows, d = hg.shape
    t = T_GROUP
    de = w1.shape[-1]

    def row_tile(i, tg, used):
        return (jnp.minimum(i, used[0] - 1), 0)

    def group_weights(i, tg, used):
        return (tg[i], 0, 0)

    grid_spec = pltpu.PrefetchScalarGridSpec(
        num_scalar_prefetch=2,
        grid=(rows // t,),
        in_specs=[
            pl.BlockSpec((t, d), row_tile),
            pl.BlockSpec((t, LANES), row_tile),
            pl.BlockSpec((EXPERTS_PER_GROUP, d, de), group_weights),
            pl.BlockSpec((EXPERTS_PER_GROUP, d, de), group_weights),
            pl.BlockSpec((EXPERTS_PER_GROUP, de, d), group_weights),
        ],
        out_specs=pl.BlockSpec((t, d), lambda i, tg, used: (i, 0)),
    )
    return pl.pallas_call(
        _experts_kernel,
        grid_spec=grid_spec,
        out_shape=jax.ShapeDtypeStruct((rows, d), jnp.bfloat16),
        compiler_params=pltpu.CompilerParams(
            dimension_semantics=("arbitrary",), vmem_limit_bytes=VMEM_LIMIT),
        name="moe_experts",
    )(plan["tile_group"], plan["used_tiles"], hg, wg, w1, w3, w2)


def _unsort_kernel(dest_ref, small_ref, x1_ref, aux_ref, gn_ref, yg_ref, o_ref, ybuf, sems):
    i = pl.program_id(0)
    tm = TM_SORT
    slot = i % 2

    def chunks(step, slot, act):
        for g in range(N_GROUPS):
            idx = step * N_GROUPS + g
            off = pl.multiple_of(dest_ref[idx], ROW_ALIGN)
            for nrows, is_small in ((SMALL_CHUNK, True), (tm, False)):
                @pl.when((small_ref[idx] != 0) == is_small)
                def _():
                    act(pltpu.make_async_copy(yg_ref.at[pl.ds(off, nrows)],
                                              ybuf.at[slot, g, pl.ds(0, nrows)], sems.at[slot, g]))

    @pl.when(i == 0)
    def _():
        ybuf[...] = jnp.zeros_like(ybuf)
        chunks(0, 0, lambda c: c.start())

    @pl.when(i + 1 < pl.num_programs(0))
    def _():
        chunks(i + 1, 1 - slot, lambda c: c.start())

    chunks(i, slot, lambda c: c.wait())

    aux = aux_ref[...]
    lane = lax.broadcasted_iota(jnp.int32, (tm, LANES), 1)
    is_group_lane = jnp.logical_and(lane >= GROUP_LANE0, lane < GROUP_LANE0 + N_GROUPS)
    hot = jnp.where(is_group_lane, aux, 0.0)
    r_io = lax.broadcasted_iota(jnp.int32, (tm, tm), 0)
    c_io = lax.broadcasted_iota(jnp.int32, (tm, tm), 1)
    before = (c_io < r_io).astype(jnp.bfloat16)
    seen = jnp.dot(before, hot.astype(jnp.bfloat16), preferred_element_type=jnp.float32)
    rank = jnp.sum(hot * seen, axis=-1, keepdims=True).astype(jnp.int32)
    y = x1_ref[...]
    for g in range(N_GROUPS):
        mine = aux[:, GROUP_LANE0 + g:GROUP_LANE0 + g + 1] > 0.5
        sel = jnp.where(jnp.logical_and(rank == c_io, mine), 1.0, 0.0).astype(jnp.bfloat16)
        y = y + jnp.dot(sel, ybuf[slot, g], preferred_element_type=jnp.float32)
    o_ref[...] = _rmsnorm_f32(y, gn_ref[...])


def _moe_unsort(plan, x1, aux, gn, yg):
    n, d = x1.shape
    tm = TM_SORT
    grid_spec = pltpu.PrefetchScalarGridSpec(
        num_scalar_prefetch=2,
        grid=(n // tm,),
        in_specs=[
            pl.BlockSpec((tm, d), lambda i, *_: (i, 0)),
            pl.BlockSpec((tm, LANES), lambda i, *_: (i, 0)),
            pl.BlockSpec((1, d), lambda i, *_: (0, 0)),
            pl.BlockSpec(memory_space=pl.ANY),
        ],
        out_specs=pl.BlockSpec((tm, d), lambda i, *_: (i, 0)),
        scratch_shapes=[pltpu.VMEM((2, N_GROUPS, tm, d), jnp.bfloat16),
                        pltpu.SemaphoreType.DMA((2, N_GROUPS))],
    )
    return pl.pallas_call(
        _unsort_kernel,
        grid_spec=grid_spec,
        out_shape=jax.ShapeDtypeStruct((n, d), jnp.float32),
        compiler_params=pltpu.CompilerParams(
            dimension_semantics=("arbitrary",), vmem_limit_bytes=VMEM_LIMIT),
        name="moe_unsort",
    )(plan["dest"], plan["small"], x1, aux, gn, yg)


def kernel(x, norm_attn, w_in, b_forget, w_o_sb, w_o_fox, w_out, norm_ffn, w_router_group,
           b_router_group, w_router_expert, b_router_expert, w1, w3, w2, norm_final):
    b, s, d = x.shape
    n = b * s
    depth = w_in.shape[0]
    assert depth == 1, "the final norm is fused into the MoE un-sort of a single layer"
    assert s % TM_PROJ == 0 and s % TQ == 0 and n % TM_SORT == 0 and T_GROUP == TM_SORT
    bf16 = jnp.bfloat16
    n_main = 6 * ATTN_WIDTH
    x2 = x.reshape(n, d)
    for l in range(depth):
        w_l = w_in[l]
        w_main = jnp.concatenate([w_l[:, :n_main], w_l[:, n_main + N_HEADS:]], axis=1).astype(bf16)
        wf = jnp.pad(w_l[:, n_main:n_main + N_HEADS], ((0, 0), (0, LANES - N_HEADS)))
        wf_hi = wf.astype(bf16)
        wf_parts = jnp.concatenate([wf_hi, (wf - wf_hi.astype(jnp.float32)).astype(bf16)], axis=1)
        bf = jnp.pad(b_forget[l], (0, LANES - N_HEADS)).reshape(1, LANES)
        wr = jnp.concatenate([w_router_group[l].T, jnp.zeros((8 - N_GROUPS, d), jnp.float32),
                              w_router_expert[l].T], axis=0)
        wr_hi = wr.astype(bf16)
        wr_lo = (wr - wr_hi.astype(jnp.float32)).astype(bf16)
        br = jnp.concatenate([b_router_group[l], jnp.zeros((8 - N_GROUPS,), jnp.float32),
                              b_router_expert[l]]).reshape(ROUTER_ROWS, 1)

        proj, c, ct = _inproj(x2, norm_attn[l].reshape(1, d), w_main, wf_parts, bf, s)
        proj3 = proj.reshape(b, s, proj.shape[1])
        o_sb, o_fx = _attention(proj3, c.reshape(b, s, LANES), ct)
        x1, h, ids, aux = _post_attention(
            o_sb.reshape(n, ATTN_WIDTH), o_fx.reshape(n, ATTN_WIDTH), proj, x2,
            w_o_sb[l].astype(bf16), w_o_fox[l].astype(bf16), w_out[l].astype(bf16),
            norm_ffn[l].reshape(1, d), wr_hi, wr_lo, br)
        plan = _sort_plan(ids[0] // EXPERTS_PER_GROUP, n)
        hg, wg = _moe_sort(plan, h, ids, aux)
        yg = _moe_experts(plan, hg, wg, w1[l].astype(bf16), w3[l].astype(bf16), w2[l].astype(bf16))
        x2 = _moe_unsort(plan, x1, aux, norm_final.reshape(1, d), yg)
    return x2.reshape(b, s, d)
```
